```python
import functools
import jax, jax.numpy as jnp
from jax import lax
import numpy as np

D_MODEL = 1024
BATCH = 16
SEQ = 2048
DEPTH = 1
DEC_BATCH = 128
DEC_SEQ = 1
PAST_LEN = 16384
PAGE_SIZE = 128

PLE_DIM = 256
N_HEADS = 16
QK_NOPE = 64
QK_ROPE = 32
V_HEAD = 64
Q_LORA = 384
KV_LORA = 256
ROPE_THETA = 10000.0
ATTN_SCALE = (QK_NOPE + QK_ROPE) ** -0.5
Q_BLOCK = 128
C_CONV = 512
CONV_WIDTH = 31
CONV_STATE = CONV_WIDTH - 1
N_GROUPS = 4
EXPERTS_PER_GROUP = 8
N_EXPERTS = N_GROUPS * EXPERTS_PER_GROUP
TOP_K = 2
D_EXPERT = 256
MOE_BLOCK = 128
DN_ALPHA = (2 * DEPTH) ** 0.25
DN_BETA = (8 * DEPTH) ** -0.25
LN_EPS = 1e-5
RMS_EPS = 1e-6
NEG_INF = -1e30
OFF_KV = Q_LORA
OFF_KR = OFF_KV + KV_LORA
OFF_CONV = OFF_KR + QK_ROPE
OFF_GA = OFF_CONV + 2 * C_CONV
OFF_GB = OFF_GA + D_MODEL
D_IN = OFF_GB + D_MODEL
IN_OFFSETS = (OFF_KV, OFF_KR, OFF_CONV, OFF_GA, OFF_GB)

kernel_name = "hybrid_conv_mla_hiermoe_deepnorm_step"


def layer_norm(x, g, b):
    xf = x.astype(jnp.float32)
    mu = jnp.mean(xf, axis=-1, keepdims=True)
    xc = xf - mu
    var = jnp.mean(xc * xc, axis=-1, keepdims=True)
    return (xc * lax.rsqrt(var + LN_EPS) * g + b).astype(x.dtype)


def rms_norm(x, g):
    xf = x.astype(jnp.float32)
    y = xf * lax.rsqrt(jnp.mean(xf * xf, axis=-1, keepdims=True) + RMS_EPS)
    return (y * g).astype(x.dtype)


def rope(x, pos):
    half = x.shape[-1] // 2
    inv = 1.0 / (ROPE_THETA ** (jnp.arange(half, dtype=jnp.float32) / half))
    ang = pos.astype(jnp.float32)[:, None] * inv[None, :]
    shp = (pos.shape[0],) + (1,) * (x.ndim - 3) + (half,)
    cos, sin = jnp.cos(ang).reshape(shp), jnp.sin(ang).reshape(shp)
    x1, x2 = x[..., :half], x[..., half:]
    return jnp.concatenate([x1 * cos - x2 * sin, x2 * cos + x1 * sin], axis=-1).astype(x.dtype)


def latent_scores(q_lat, q_rope, c_kv, k_rope):
    s = jnp.einsum('bqhl,bkl->bhqk', q_lat, c_kv, preferred_element_type=jnp.float32)
    s = s + jnp.einsum('bqhr,bkr->bhqk', q_rope, k_rope, preferred_element_type=jnp.float32)
    return s * ATTN_SCALE


def mla_prompt_attend(q_lat, q_rope, c_kv, k_rope):
    s_len = q_lat.shape[1]
    pos = jnp.arange(s_len)
    outs = []
    for i in range(s_len // Q_BLOCK):
        q0, q1 = i * Q_BLOCK, (i + 1) * Q_BLOCK
        s = latent_scores(q_lat[:, q0:q1], q_rope[:, q0:q1], c_kv[:, :q1], k_rope[:, :q1])
        mask = pos[None, :q1] <= pos[q0:q1, None]
        p = jax.nn.softmax(jnp.where(mask, s, NEG_INF), axis=-1).astype(c_kv.dtype)
        outs.append(jnp.einsum('bhqk,bkl->bqhl', p, c_kv[:, :q1]))
    return jnp.concatenate(outs, axis=1)


def mla_sample_attend(q_lat, q_rope, c_kv, k_rope, cache_ckv, cache_krope, page_table, layer):
    bd, n_pages = page_table.shape
    past = n_pages * PAGE_SIZE
    ckv_past = cache_ckv[layer, page_table].reshape(bd, past, KV_LORA).astype(c_kv.dtype)
    kr_past = cache_krope[layer, page_table].reshape(bd, past, QK_ROPE).astype(k_rope.dtype)
    q_len = q_lat.shape[1]
    s_past = latent_scores(q_lat, q_rope, ckv_past, kr_past)
    s_new = latent_scores(q_lat, q_rope, c_kv, k_rope)
    causal = jnp.tril(jnp.ones((q_len, q_len), dtype=bool))
    s_new = jnp.where(causal, s_new, NEG_INF)
    p = jax.nn.softmax(jnp.concatenate([s_past, s_new], axis=-1), axis=-1).astype(c_kv.dtype)
    return (jnp.einsum('bhqk,bkl->bqhl', p[..., :past], ckv_past)
            + jnp.einsum('bhqk,bkl->bqhl', p[..., past:], c_kv))


def hier_moe(x, w_group, w_router, w_gate, w_up, w_down):
    b, s, d = x.shape
    h = x.reshape(b * s, d)
    t = h.shape[0]
    lg = (h @ w_group).astype(jnp.float32)
    pg = jax.nn.softmax(lg, axis=-1)
    g_idx = jnp.argmax(lg, axis=-1)
    pg_sel = jnp.take_along_axis(pg, g_idx[:, None], axis=1)
    le = (h @ w_router).astype(jnp.float32).reshape(t, N_GROUPS, EXPERTS_PER_GROUP)
    le_sel = jnp.take_along_axis(le, g_idx[:, None, None], axis=1)[:, 0]
    top_v, top_i = lax.top_k(le_sel, TOP_K)
    gates = (pg_sel * jax.nn.softmax(top_v, axis=-1)).astype(x.dtype)
    eid = (g_idx[:, None] * EXPERTS_PER_GROUP + top_i).astype(jnp.int32)
    n_rows = t * TOP_K
    n_blocks = (n_rows + N_EXPERTS * (MOE_BLOCK - 1) + MOE_BLOCK - 1) // MOE_BLOCK
    e_flat = eid.reshape(n_rows)
    order = jnp.argsort(e_flat)
    e_sorted = e_flat[order]
    counts = jnp.bincount(e_flat, length=N_EXPERTS).astype(jnp.int32)
    starts = jnp.cumsum(counts) - counts
    pcounts = ((counts + MOE_BLOCK - 1) // MOE_BLOCK) * MOE_BLOCK
    pend = jnp.cumsum(pcounts)
    pstarts = pend - pcounts
    dest = pstarts[e_sorted] + (jnp.arange(n_rows, dtype=jnp.int32) - starts[e_sorted])
    rows = jnp.zeros((n_blocks * MOE_BLOCK, d), h.dtype).at[dest].set(h[order // TOP_K])
    block_start = jnp.arange(n_blocks, dtype=jnp.int32) * MOE_BLOCK
    block_expert = jnp.clip(jnp.searchsorted(pend, block_start, side='right'), 0, N_EXPERTS - 1)

    def block_ffn(args):
        xb, e = args
        return (jax.nn.silu(xb @ w_gate[e]) * (xb @ w_up[e])) @ w_down[e]

    yb = lax.map(block_ffn, (rows.reshape(n_blocks, MOE_BLOCK, d), block_expert))
    y_sorted = yb.reshape(n_blocks * MOE_BLOCK, d)[dest]
    y_assign = jnp.zeros((n_rows, d), yb.dtype).at[order].set(y_sorted)
    y = jnp.sum(y_assign.reshape(t, TOP_K, d) * gates[..., None], axis=1)
    return y.reshape(b, s, d).astype(x.dtype)


def trunk_layer(x, p, pos, conv_left, attend, lw):
    b, s, _ = x.shape
    h = x @ lw['w_in']
    cq, ckv, kr, conv_in, ga, gb = jnp.split(h, IN_OFFSETS, axis=-1)
    q = (rms_norm(cq, lw['q_norm_g']) @ lw['w_uq']).reshape(b, s, N_HEADS, QK_NOPE + QK_ROPE)
    q_nope = q[..., :QK_NOPE]
    q_rope = rope(q[..., QK_NOPE:], pos)
    c_kv = rms_norm(ckv, lw['kv_norm_g'])
    k_rope = rope(kr, pos)
    q_lat = jnp.einsum('bshn,lhn->bshl', q_nope, lw['w_uk'])
    o_lat = attend(q_lat, q_rope, c_kv, k_rope)
    o = jnp.einsum('bshl,lhv->bshv', o_lat, lw['w_uv']).reshape(b, s, N_HEADS * V_HEAD)
    branch_b = o @ lw['w_o_attn']
    ca, cb = jnp.split(conv_in, 2, axis=-1)
    u = ca * jax.nn.sigmoid(cb)
    ext = jnp.concatenate([conv_left.astype(u.dtype), u], axis=1)
    y = lax.conv_general_dilated(ext, lw['w_dw'][:, None, :].astype(ext.dtype), window_strides=(1,),
                                 padding='VALID', dimension_numbers=('NWC', 'WIO', 'NWC'),
                                 feature_group_count=C_CONV) + lw['b_dw']
    new_conv = ext[:, -CONV_STATE:]
    branch_a = jax.nn.silu(layer_norm(y, lw['conv_ln_g'], lw['conv_ln_b'])) @ lw['w_pw2']
    mix = (jax.nn.sigmoid(ga) * branch_a + jax.nn.sigmoid(gb) * branch_b) @ lw['w_out']
    x = layer_norm(DN_ALPHA * x + mix, lw['ln1_g'], lw['ln1_b'])
    ffn = hier_moe(x, lw['w_group'], lw['w_router'], lw['w_gate'], lw['w_up'], lw['w_down'])
    x = layer_norm(DN_ALPHA * x + ffn, lw['ln2_g'], lw['ln2_b'])
    ple = jax.nn.sigmoid(x @ lw['w_ple_gate']) * (p.astype(x.dtype) @ lw['w_ple_proj'])
    x = x + rms_norm(ple, lw['ple_norm_g'])
    return x, c_kv, k_rope, new_conv


def setup_inputs(seed: int = 0) -> dict:
    key = jax.random.key(seed)
    ks = iter(jax.random.split(key, 48))
    f32 = jnp.float32

    def nrm(shape, scale):
        return jax.random.normal(next(ks), shape, f32) * scale

    def gain(n):
        return 1.0 + nrm((DEPTH, n), 0.02)

    n_pages = PAST_LEN // PAGE_SIZE
    n_used = DEC_BATCH * n_pages
    n_pool = (5 * n_used) // 4
    d = D_MODEL
    out = {}
    out['x_prompt'] = nrm((BATCH, SEQ, d), 1.0)
    out['x_sample'] = nrm((DEC_BATCH, DEC_SEQ, d), 1.0)
    out['cache_ckv'] = nrm((DEPTH, n_pool, PAGE_SIZE, KV_LORA), 1.0)
    out['cache_krope'] = nrm((DEPTH, n_pool, PAGE_SIZE, QK_ROPE), 1.0)
    out['state_conv'] = nrm((DEPTH, DEC_BATCH, CONV_STATE, C_CONV), 0.5)
    out['page_table'] = jax.random.permutation(next(ks), n_pool)[:n_used].reshape(DEC_BATCH, n_pages).astype(jnp.int32)
    out['p_prompt'] = nrm((DEPTH, BATCH, SEQ, PLE_DIM), 1.0)
    out['p_sample'] = nrm((DEPTH, DEC_BATCH, DEC_SEQ, PLE_DIM), 1.0)
    out['w_in'] = nrm((DEPTH, d, D_IN), d ** -0.5)
    out['q_norm_g'] = gain(Q_LORA)
    out['w_uq'] = nrm((DEPTH, Q_LORA, N_HEADS * (QK_NOPE + QK_ROPE)), Q_LORA ** -0.5)
    out['kv_norm_g'] = gain(KV_LORA)
    out['w_uk'] = nrm((DEPTH, KV_LORA, N_HEADS, QK_NOPE), KV_LORA ** -0.5)
    out['w_uv'] = nrm((DEPTH, KV_LORA, N_HEADS, V_HEAD), KV_LORA ** -0.5)
    out['w_o_attn'] = nrm((DEPTH, N_HEADS * V_HEAD, d), (N_HEADS * V_HEAD) ** -0.5)
    out['w_dw'] = nrm((DEPTH, CONV_WIDTH, C_CONV), CONV_WIDTH ** -0.5)
    out['b_dw'] = nrm((DEPTH, C_CONV), 0.02)
    out['conv_ln_g'] = gain(C_CONV)
    out['conv_ln_b'] = nrm((DEPTH, C_CONV), 0.02)
    out['w_pw2'] = nrm((DEPTH, C_CONV, d), C_CONV ** -0.5)
    out['w_out'] = nrm((DEPTH, d, d), d ** -0.5 * DN_BETA)
    out['ln1_g'] = gain(d)
    out['ln1_b'] = nrm((DEPTH, d), 0.02)
    out['w_group'] = nrm((DEPTH, d, N_GROUPS), d ** -0.5)
    out['w_router'] = nrm((DEPTH, d, N_EXPERTS), d ** -0.5)
    out['w_gate'] = nrm((DEPTH, N_EXPERTS, d, D_EXPERT), d ** -0.5)
    out['w_up'] = nrm((DEPTH, N_EXPERTS, d, D_EXPERT), d ** -0.5)
    out['w_down'] = nrm((DEPTH, N_EXPERTS, D_EXPERT, d), D_EXPERT ** -0.5 * DN_BETA)
    out['ln2_g'] = gain(d)
    out['ln2_b'] = nrm((DEPTH, d), 0.02)
    out['w_ple_gate'] = nrm((DEPTH, d, d), d ** -0.5)
    out['w_ple_proj'] = nrm((DEPTH, PLE_DIM, d), PLE_DIM ** -0.5)
    out['ple_norm_g'] = gain(d)
    return out


def reference(x_prompt, x_sample, cache_ckv, cache_krope, state_conv, page_table, p_prompt, p_sample,
              w_in, q_norm_g, w_uq, kv_norm_g, w_uk, w_uv, w_o_attn, w_dw, b_dw, conv_ln_g, conv_ln_b,
              w_pw2, w_out, ln1_g, ln1_b, w_group, w_router, w_gate, w_up, w_down, ln2_g, ln2_b,
              w_ple_gate, w_ple_proj, ple_norm_g):
    past_len = page_table.shape[1] * PAGE_SIZE
    pos_prompt = jnp.arange(x_prompt.shape[1], dtype=jnp.int32)
    pos_sample = past_len + jnp.arange(x_sample.shape[1], dtype=jnp.int32)
    hp, hs = x_prompt, x_sample
    ckv_p, kr_p, conv_p, ckv_s, kr_s, conv_s = [], [], [], [], [], []
    for i in range(DEPTH):
        lw = dict(w_in=w_in[i], q_norm_g=q_norm_g[i], w_uq=w_uq[i], kv_norm_g=kv_norm_g[i],
                  w_uk=w_uk[i], w_uv=w_uv[i], w_o_attn=w_o_attn[i], w_dw=w_dw[i], b_dw=b_dw[i],
                  conv_ln_g=conv_ln_g[i], conv_ln_b=conv_ln_b[i], w_pw2=w_pw2[i], w_out=w_out[i],
                  ln1_g=ln1_g[i], ln1_b=ln1_b[i], w_group=w_group[i], w_router=w_router[i],
                  w_gate=w_gate[i], w_up=w_up[i], w_down=w_down[i], ln2_g=ln2_g[i], ln2_b=ln2_b[i],
                  w_ple_gate=w_ple_gate[i], w_ple_proj=w_ple_proj[i], ple_norm_g=ple_norm_g[i])
        zeros_left = jnp.zeros((hp.shape[0], CONV_STATE, C_CONV), hp.dtype)
        hp, c1, k1, v1 = trunk_layer(hp, p_prompt[i], pos_prompt, zeros_left, mla_prompt_attend, lw)
        sample_attend = functools.partial(mla_sample_attend, cache_ckv=cache_ckv, cache_krope=cache_krope,
                                          page_table=page_table, layer=i)
        hs, c2, k2, v2 = trunk_layer(hs, p_sample[i], pos_sample, state_conv[i], sample_attend, lw)
        ckv_p.append(c1); kr_p.append(k1); conv_p.append(v1)
        ckv_s.append(c2); kr_s.append(k2); conv_s.append(v2)
    return (hp, hs, jnp.stack(ckv_p), jnp.stack(kr_p), jnp.stack(conv_p),
            jnp.stack(ckv_s), jnp.stack(kr_s), jnp.stack(conv_s))
```

```python
import functools

import numpy as np
import jax
import jax.numpy as jnp
from jax import lax
from jax.experimental import pallas as pl
from jax.experimental.pallas import tpu as pltpu

F32 = jnp.float32
BF16 = jnp.bfloat16

QK_NOPE = 64
QK_ROPE = 32
HALF_ROPE = QK_ROPE // 2
V_HEAD = 64
ROPE_THETA = 10000.0
ATTN_SCALE = (QK_NOPE + QK_ROPE) ** -0.5
N_GROUPS = 4
EXPERTS_PER_GROUP = 8
DEPTH = 1
DN_ALPHA = (2 * DEPTH) ** 0.25
LN_EPS = 1e-5
RMS_EPS = 1e-6
NEG_INF = -1e30

LANES = 128
SUBLANES = 8
VMEM_LIMIT = 56 * 1024 * 1024

TOKEN_TILE = 512
ATTN_TILE = 512
CONV_CHUNK = 64
CONV_HALO = 32
MOE_BLOCK = 256
PAGES_PER_CHUNK = 16


def _cparams(sem, vmem=VMEM_LIMIT):
    return pltpu.CompilerParams(dimension_semantics=sem, vmem_limit_bytes=vmem)


def _full(shape):
    n = len(shape)
    return pl.BlockSpec(shape, lambda *_: (0,) * n)


def _layer_norm(x, g, b):
    mu = jnp.mean(x, axis=-1, keepdims=True)
    xc = x - mu
    var = jnp.mean(xc * xc, axis=-1, keepdims=True)
    return xc * lax.rsqrt(var + LN_EPS) * g + b


def _rms_norm(x, g):
    return x * lax.rsqrt(jnp.mean(x * x, axis=-1, keepdims=True) + RMS_EPS) * g


def _bdot(a, b):
    return jnp.dot(a.astype(BF16), b.astype(BF16), preferred_element_type=F32)


def _rope_table_kernel(inv_ref, sign_ref, valid_ref, cos_ref, sin_ref, *, offset, step):
    rows = cos_ref.shape[0]
    r = lax.broadcasted_iota(jnp.int32, (rows, LANES), 0) + pl.program_id(0) * rows
    pos = (offset + step * r).astype(F32)
    ang = pos * inv_ref[...]
    cos_ref[...] = jnp.cos(ang) * valid_ref[...]
    sin_ref[...] = jnp.sin(ang) * sign_ref[...]


def _rope_tables(n_rows, offset, step, inv_l, sign_l, valid_l):
    tr = min(n_rows, 512)
    kern = functools.partial(_rope_table_kernel, offset=offset, step=step)
    return pl.pallas_call(
        kern,
        grid=(n_rows // tr,),
        in_specs=[_full((1, LANES))] * 3,
        out_specs=[pl.BlockSpec((tr, LANES), lambda i: (i, 0))] * 2,
        out_shape=[jax.ShapeDtypeStruct((n_rows, LANES), F32)] * 2,
        compiler_params=_cparams(("arbitrary",)),
        name="rope_tables",
    )(inv_l, sign_l, valid_l)


C_Q, C_KV, C_CA, C_CB, C_GA, C_GB, C_KA, C_KB, C_END = 0, 384, 640, 1152, 1664, 2688, 3712, 3840, 3968


def _inproj_kernel(x_ref, win_ref, qg_ref, kvg_ref, wuq_ref, *rest, n_rep, make_kv, bcast):
    if make_kv:
        (wukv_ref, cos_ref, sin_ref, qn_o, qr_o, kn_o, v_o, krd_o, ckv_o, kr_o, u_o, sa_o, sb_o) = rest
    else:
        (cos_ref, sin_ref, qn_o, qr_o, ckv_o, kr_o, u_o, sa_o, sb_o) = rest
    xb = x_ref[...].astype(BF16)

    def proj(a, b):
        return jnp.dot(xb, win_ref[:, a:b], preferred_element_type=F32)

    if bcast:
        c1, s1 = cos_ref[0:1, :], sin_ref[0:1, :]
    else:
        c1, s1 = cos_ref[...], sin_ref[...]
    cn = jnp.concatenate([c1] * n_rep, axis=1)
    sn = jnp.concatenate([s1] * n_rep, axis=1)
    d_nope = qn_o.shape[1]
    nr = LANES * n_rep

    cqn = _rms_norm(proj(C_Q, C_KV), qg_ref[...])
    qall = jnp.dot(cqn.astype(BF16), wuq_ref[...], preferred_element_type=F32)
    qn_o[...] = (qall[:, :d_nope] * ATTN_SCALE).astype(qn_o.dtype)
    qa = qall[:, d_nope:d_nope + nr]
    qb = qall[:, d_nope + nr:d_nope + 2 * nr]
    qr_o[...] = ((qa * cn + qb * sn) * ATTN_SCALE).astype(qr_o.dtype)

    ckvn = _rms_norm(proj(C_KV, C_CA), kvg_ref[...])
    ckv_o[...] = ckvn
    krot = proj(C_KA, C_KB) * c1 + proj(C_KB, C_END) * s1
    kr_o[...] = krot[:, :QK_ROPE]
    if make_kv:
        kv = jnp.dot(ckvn.astype(BF16), wukv_ref[...], preferred_element_type=F32)
        half = kv.shape[1] // 2
        kn_o[...] = kv[:, :half].astype(BF16)
        v_o[...] = kv[:, half:].astype(BF16)
        krd_o[...] = krot.astype(BF16)

    u_o[...] = proj(C_CA, C_CB) * jax.nn.sigmoid(proj(C_CB, C_GA))
    sa_o[...] = jax.nn.sigmoid(proj(C_GA, C_GB)).astype(BF16)
    sb_o[...] = jax.nn.sigmoid(proj(C_GB, C_KA)).astype(BF16)


def _inproj(x, win_r, qg, kvg, wuq_r, wukv, cos_t, sin_t, *, n_rep, seq, make_kv):
    t, d = x.shape
    tm = min(TOKEN_TILE, t)
    d_nope = wuq_r.shape[1] - 2 * LANES * n_rep
    nr = LANES * n_rep
    c_conv = C_CB - C_CA
    kvl = C_CA - C_KV
    bcast = seq < tm
    if bcast:
        tab_spec = _full(cos_t.shape)
    else:
        nst = seq // tm
        tab_spec = pl.BlockSpec((tm, LANES), lambda i: (i % nst, 0))
    row = lambda w: pl.BlockSpec((tm, w), lambda i: (i, 0))
    in_specs = [row(d), _full(win_r.shape), _full(qg.shape), _full(kvg.shape), _full(wuq_r.shape)]
    args = [x, win_r, qg, kvg, wuq_r]
    if make_kv:
        in_specs.append(_full(wukv.shape))
        args.append(wukv)
    in_specs += [tab_spec, tab_spec]
    args += [cos_t, sin_t]
    q_dt = BF16 if make_kv else F32
    outs = [(d_nope, BF16), (nr, q_dt)]
    if make_kv:
        outs += [(d_nope, BF16), (d_nope, BF16), (LANES, BF16)]
    outs += [(kvl, F32), (QK_ROPE, F32), (c_conv, F32), (d, BF16), (d, BF16)]
    kern = functools.partial(_inproj_kernel, n_rep=n_rep, make_kv=make_kv, bcast=bcast)
    return pl.pallas_call(
        kern,
        grid=(t // tm,),
        in_specs=in_specs,
        out_specs=[row(w) for w, _ in outs],
        out_shape=[jax.ShapeDtypeStruct((t, w), dt) for w, dt in outs],
        compiler_params=_cparams(("arbitrary",)),
        name="inproj_kv" if make_kv else "inproj_q",
    )(*args)


def _pattn_kernel(qn_ref, qr_ref, kn_ref, krd_ref, v_ref, o_ref, m_s, l_s, acc_s, *, blk):
    i = pl.program_id(2)
    lane = lax.broadcasted_iota(jnp.int32, (blk, LANES), 1)
    qn = qn_ref[...]
    qr = qr_ref[...]
    zero = jnp.zeros_like(qn)
    q_heads = (
        jnp.concatenate([jnp.where(lane < QK_NOPE, qn, zero), jnp.where(lane < QK_ROPE, qr, zero)], axis=1),
        jnp.concatenate([jnp.where(lane >= QK_NOPE, qn, zero),
                         jnp.where((lane >= QK_ROPE) & (lane < 2 * QK_ROPE), qr, zero)], axis=1),
    )
    m_s[...] = jnp.full(m_s.shape, NEG_INF, F32)
    l_s[...] = jnp.zeros(l_s.shape, F32)
    acc_s[...] = jnp.zeros(acc_s.shape, F32)
    n_rep = blk // LANES

    def step(j, masked):
        start = pl.multiple_of(j * blk, blk)
        k = jnp.concatenate([kn_ref[pl.ds(start, blk), :], krd_ref[pl.ds(start, blk), :]], axis=1)
        v = v_ref[pl.ds(start, blk), :]
        for h in range(2):
            s = lax.dot_general(q_heads[h], k, (((1,), (1,)), ((), ())), preferred_element_type=F32)
            if masked:
                row = lax.broadcasted_iota(jnp.int32, (blk, blk), 0)
                col = lax.broadcasted_iota(jnp.int32, (blk, blk), 1)
                s = jnp.where(col <= row, s, NEG_INF)
            m_prev = m_s[h]
            m_new = jnp.maximum(m_prev, jnp.max(s, axis=1, keepdims=True))
            alpha = jnp.exp(m_prev - m_new)
            p = jnp.exp(s - jnp.concatenate([m_new] * n_rep, axis=1))
            l_s[h] = alpha * l_s[h] + jnp.sum(p, axis=1, keepdims=True)
            acc_s[h] = alpha * acc_s[h] + jnp.dot(p.astype(BF16), v, preferred_element_type=F32)
            m_s[h] = m_new

    def body(j, carry):
        step(j, False)
        return carry

    lax.fori_loop(0, i, body, 0)
    step(i, True)
    o = jnp.where(lane < V_HEAD, acc_s[0] / l_s[0], acc_s[1] / l_s[1])
    o_ref[...] = o.astype(o_ref.dtype)


def _prompt_attention(qn, qr, kn, krd, v, *, batch, seq):
    t, d = qn.shape
    blk = min(ATTN_TILE, seq)
    nq = seq // blk
    n_pairs = d // LANES
    qspec = pl.BlockSpec((blk, LANES), lambda b, j, i: (b * nq + i, j))
    kspec = pl.BlockSpec((seq, LANES), lambda b, j, i: (b, j))
    kern = functools.partial(_pattn_kernel, blk=blk)
    return pl.pallas_call(
        kern,
        grid=(batch, n_pairs, nq),
        in_specs=[qspec, qspec, kspec, pl.BlockSpec((seq, LANES), lambda b, j, i: (b, 0)), kspec],
        out_specs=qspec,
        out_shape=jax.ShapeDtypeStruct((t, d), BF16),
        scratch_shapes=[pltpu.VMEM((2, blk, LANES), F32)] * 3,
        compiler_params=_cparams(("arbitrary",) * 3),
        name="prompt_attention",
    )(qn, qr, kn, krd, v)


def _dense_kernel(a_ref, w_ref, o_ref):
    o_ref[...] = jnp.dot(a_ref[...].astype(BF16), w_ref[...], preferred_element_type=F32).astype(o_ref.dtype)


def _dense(a, w, out_dtype):
    m, n = a.shape[0], w.shape[1]
    return pl.pallas_call(
        _dense_kernel,
        grid=(1,),
        in_specs=[_full(a.shape), _full(w.shape)],
        out_specs=_full((m, n)),
        out_shape=jax.ShapeDtypeStruct((m, n), out_dtype),
        compiler_params=_cparams(("arbitrary",)),
        name="dense",
    )(a, w)


def _sattn_kernel(pt_ref, ql_ref, qr_ref, cnew_ref, knew_ref, ckv_hbm, kr_hbm, o_ref,
                  cbuf, kbuf, sems, *, n_chunks, ppc):
    page_rows = cbuf.shape[2]
    kc = ppc * page_rows
    ql = ql_ref[0].astype(BF16)
    qr = qr_ref[0].astype(BF16)
    n_heads = ql.shape[0]

    def copies(chunk, slot):
        out = []
        for pg in range(ppc):
            page = pt_ref[0, 0, chunk * ppc + pg]
            out.append(pltpu.make_async_copy(ckv_hbm.at[page], cbuf.at[slot, pg], sems.at[0, slot]))
            out.append(pltpu.make_async_copy(kr_hbm.at[page], kbuf.at[slot, pg], sems.at[1, slot]))
        return out

    for cp in copies(0, 0):
        cp.start()

    def body(c, carry):
        m, l, acc = carry
        slot = c % 2

        @pl.when(c + 1 < n_chunks)
        def _():
            for cp in copies(c + 1, 1 - slot):
                cp.start()

        for cp in copies(c, slot):
            cp.wait()
        ck = cbuf[slot].reshape(kc, cbuf.shape[3]).astype(BF16)
        kr = kbuf[slot].reshape(kc, kbuf.shape[3]).astype(BF16)
        dn = (((1,), (1,)), ((), ()))
        s = lax.dot_general(ql, ck, dn, preferred_element_type=F32)
        s = s + lax.dot_general(qr, kr, dn, preferred_element_type=F32)
        m_new = jnp.maximum(m, jnp.max(s, axis=1, keepdims=True))
        alpha = jnp.exp(m - m_new)
        p = jnp.exp(s - m_new)
        l = alpha * l + jnp.sum(p, axis=1, keepdims=True)
        acc = alpha * acc + jnp.dot(p.astype(BF16), ck, preferred_element_type=F32)
        return m_new, l, acc

    init = (jnp.full((n_heads, 1), NEG_INF, F32), jnp.zeros((n_heads, 1), F32),
            jnp.zeros((n_heads, cbuf.shape[3]), F32))
    m, l, acc = lax.fori_loop(0, n_chunks, body, init)

    cnew = cnew_ref[0]
    knew = knew_ref[0]
    s_new = (jnp.sum(ql_ref[0] * cnew, axis=1, keepdims=True)
             + jnp.sum(qr_ref[0] * knew, axis=1, keepdims=True))
    m_new = jnp.maximum(m, s_new)
    alpha = jnp.exp(m - m_new)
    p_new = jnp.exp(s_new - m_new)
    l = alpha * l + p_new
    acc = alpha * acc + p_new * cnew
    o_ref[0] = acc / l


def _sample_attention(page_table, q_lat, q_rope, c_new, k_new, cache_ckv, cache_krope):
    bd, n_heads, kvl = q_lat.shape
    n_pages = page_table.shape[1]
    page_rows = cache_ckv.shape[1]
    ppc = min(PAGES_PER_CHUNK, n_pages)
    n_chunks = n_pages // ppc
    pt3 = page_table.reshape(bd, 1, n_pages)
    per_b = lambda s: pl.BlockSpec((1,) + s, lambda b: (b, 0, 0))
    kern = functools.partial(_sattn_kernel, n_chunks=n_chunks, ppc=ppc)
    return pl.pallas_call(
        kern,
        grid=(bd,),
        in_specs=[
            pl.BlockSpec((1, 1, n_pages), lambda b: (b, 0, 0), memory_space=pltpu.SMEM),
            per_b((n_heads, kvl)), per_b((n_heads, QK_ROPE)), per_b((1, kvl)), per_b((1, QK_ROPE)),
            pl.BlockSpec(memory_space=pl.ANY), pl.BlockSpec(memory_space=pl.ANY),
        ],
        out_specs=per_b((n_heads, kvl)),
        out_shape=jax.ShapeDtypeStruct((bd, n_heads, kvl), F32),
        scratch_shapes=[
            pltpu.VMEM((2, ppc, page_rows, kvl), F32),
            pltpu.VMEM((2, ppc, page_rows, QK_ROPE), F32),
            pltpu.SemaphoreType.DMA((2, 2)),
        ],
        compiler_params=_cparams(("arbitrary",)),
        name="sample_attention",
    )(pt3, q_lat, q_rope, c_new, k_new, cache_ckv, cache_krope)


def _conv_prompt_kernel(u_ref, halo_ref, w_ref, b_ref, y_ref, ext_s, *, ts, width):
    i = pl.program_id(1)
    left = jnp.where(i > 0, halo_ref[...], jnp.zeros_like(halo_ref))
    ext_s[0:CONV_HALO, :] = left
    ext_s[CONV_HALO:CONV_HALO + ts, :] = u_ref[...]
    base = CONV_HALO - (width - 1)
    for c in range(ts // CONV_CHUNK):
        r0 = c * CONV_CHUNK
        acc = jnp.broadcast_to(b_ref[...], (CONV_CHUNK, u_ref.shape[1]))
        for k in range(width):
            acc = acc + w_ref[k:k + 1, :] * ext_s[r0 + base + k:r0 + base + k + CONV_CHUNK, :]
        y_ref[r0:r0 + CONV_CHUNK, :] = acc


def _conv_prompt(u, w_pad, b, *, batch, seq, width):
    t, c = u.shape
    ts = min(TOKEN_TILE, seq)
    ns = seq // ts
    hb = ts // CONV_HALO
    kern = functools.partial(_conv_prompt_kernel, ts=ts, width=width)
    return pl.pallas_call(
        kern,
        grid=(batch, ns),
        in_specs=[
            pl.BlockSpec((ts, c), lambda b_, i: (b_ * ns + i, 0)),
            pl.BlockSpec((CONV_HALO, c), lambda b_, i: (jnp.maximum((b_ * ns + i) * hb - 1, 0), 0)),
            _full(w_pad.shape), _full(b.shape),
        ],
        out_specs=pl.BlockSpec((ts, c), lambda b_, i: (b_ * ns + i, 0)),
        out_shape=jax.ShapeDtypeStruct((t, c), F32),
        scratch_shapes=[pltpu.VMEM((CONV_HALO + ts, c), F32)],
        compiler_params=_cparams(("arbitrary", "arbitrary")),
        name="conv_prompt",
    )(u, u, w_pad, b)


def _conv_sample_kernel(ext_ref, w_ref, b_ref, y_ref, *, width):
    acc = jnp.broadcast_to(b_ref[...], y_ref.shape)
    for k in range(width):
        acc = acc + w_ref[k:k + 1, :] * ext_ref[k]
    y_ref[...] = acc


def _conv_sample(ext_t, w_pad, b, *, width):
    _, bd, c = ext_t.shape
    kern = functools.partial(_conv_sample_kernel, width=width)
    return pl.pallas_call(
        kern,
        grid=(1,),
        in_specs=[_full(ext_t.shape), _full(w_pad.shape), _full(b.shape)],
        out_specs=_full((bd, c)),
        out_shape=jax.ShapeDtypeStruct((bd, c), F32),
        compiler_params=_cparams(("arbitrary",)),
        name="conv_sample",
    )(ext_t, w_pad, b)


def _mix_kernel(x_ref, o_ref, y_ref, sa_ref, sb_ref, wo_ref, wpw_ref, wout_ref, cg_ref, cb_ref,
                g1_ref, b1_ref, wgr_ref, x1_ref, ri_ref):
    branch_b = jnp.dot(o_ref[...], wo_ref[...], preferred_element_type=F32)
    z = _layer_norm(y_ref[...], cg_ref[...], cb_ref[...])
    z = z * jax.nn.sigmoid(z)
    branch_a = jnp.dot(z.astype(BF16), wpw_ref[...], preferred_element_type=F32)
    mixin = sa_ref[...].astype(F32) * branch_a + sb_ref[...].astype(F32) * branch_b
    mix = jnp.dot(mixin.astype(BF16), wout_ref[...], preferred_element_type=F32)
    x1 = _layer_norm(DN_ALPHA * x_ref[...] + mix, g1_ref[...], b1_ref[...])
    x1_ref[...] = x1

    lg = jnp.dot(x1, wgr_ref[...], precision=lax.Precision.HIGHEST, preferred_element_type=F32)
    n_exp = N_GROUPS * EXPERTS_PER_GROUP
    lane = lax.broadcasted_iota(jnp.int32, lg.shape, 1)
    lane_f = lane.astype(F32)
    big = float(LANES)
    gmask = lane < N_GROUPS
    lgm = jnp.where(gmask, lg, NEG_INF)
    gmax = jnp.max(lgm, axis=1, keepdims=True)
    gidx = jnp.min(jnp.where(lgm == gmax, lane_f, big), axis=1, keepdims=True)
    pg_sel = 1.0 / jnp.sum(jnp.where(gmask, jnp.exp(lgm - gmax), 0.0), axis=1, keepdims=True)
    egroup = jnp.floor((lane_f - N_GROUPS) * (1.0 / EXPERTS_PER_GROUP))
    emask = (lane >= N_GROUPS) & (lane < N_GROUPS + n_exp) & (egroup == gidx)
    le = jnp.where(emask, lg, NEG_INF)
    v1 = jnp.max(le, axis=1, keepdims=True)
    i1 = jnp.min(jnp.where(le == v1, lane_f, big), axis=1, keepdims=True)
    le2 = jnp.where(lane_f == i1, NEG_INF, le)
    v2 = jnp.max(le2, axis=1, keepdims=True)
    i2 = jnp.min(jnp.where(le2 == v2, lane_f, big), axis=1, keepdims=True)
    e = jnp.exp(v2 - v1)
    gate1 = pg_sel / (1.0 + e)
    gate2 = pg_sel * e / (1.0 + e)
    ri = jnp.where(lane == 0, i1 - N_GROUPS,
                   jnp.where(lane == 1, i2 - N_GROUPS,
                             jnp.where(lane == 2, gate1, jnp.where(lane == 3, gate2, 0.0))))
    ri_ref[...] = ri


def _mix(x, o, y, sa, sb, wo, wpw, wout, cg, cb, g1, b1, wgr):
    t, d = x.shape
    tm = min(TOKEN_TILE, t)
    row = lambda w: pl.BlockSpec((tm, w), lambda i: (i, 0))
    consts = [wo, wpw, wout, cg, cb, g1, b1, wgr]
    return pl.pallas_call(
        _mix_kernel,
        grid=(t // tm,),
        in_specs=[row(d), row(d), row(y.shape[1]), row(d), row(d)] + [_full(a.shape) for a in consts],
        out_specs=[row(d), row(LANES)],
        out_shape=[jax.ShapeDtypeStruct((t, d), F32), jax.ShapeDtypeStruct((t, LANES), F32)],
        compiler_params=_cparams(("arbitrary",)),
        name="mix_ln1_router",
    )(x, o, y, sa, sb, *consts)


def _route_kernel(ri_ref, pos_ref, be_ref, cnt_s, base_s, *, tm, bm, n_exp):
    ph = pl.program_id(0)
    i = pl.program_id(1)
    lane_f = lax.broadcasted_iota(jnp.int32, (tm, LANES), 1).astype(F32)
    ri = ri_ref[...]
    oh0 = lane_f == ri[:, 0:1]
    oh1 = lane_f == ri[:, 1:2]
    c = jnp.where(oh0 | oh1, 1.0, 0.0)
    csum = jnp.sum(c, axis=0, keepdims=True)

    @pl.when((ph == 0) & (i == 0))
    def _():
        cnt_s[...] = jnp.zeros(cnt_s.shape, F32)

    @pl.when(ph == 0)
    def _():
        cnt_s[0:1, :] = cnt_s[0:1, :] + csum

    @pl.when((ph == 1) & (i == 0))
    def _():
        cnt = cnt_s[...]
        pc = jnp.floor((cnt + (bm - 1)) * (1.0 / bm)) * bm
        r = lax.broadcasted_iota(jnp.int32, (LANES, LANES), 0)
        cc = lax.broadcasted_iota(jnp.int32, (LANES, LANES), 1)
        upper = jnp.where(r <= cc, 1.0, 0.0)
        pend = jnp.dot(pc, upper, precision=lax.Precision.HIGHEST, preferred_element_type=F32)
        base_s[...] = pend - pc
        nbp = be_ref.shape[0]
        bstart = (lax.broadcasted_iota(jnp.int32, (nbp, LANES), 0) * bm).astype(F32)
        lane_b = lax.broadcasted_iota(jnp.int32, (nbp, LANES), 1)
        hit = jnp.where((lane_b < n_exp) & (pend[0:1, :] <= bstart), 1.0, 0.0)
        be = jnp.minimum(jnp.sum(hit, axis=1, keepdims=True), float(n_exp - 1))
        be_ref[...] = jnp.broadcast_to(be, be_ref.shape).astype(jnp.int32)

    @pl.when(ph == 1)
    def _():
        r = lax.broadcasted_iota(jnp.int32, (tm, tm), 0)
        cc = lax.broadcasted_iota(jnp.int32, (tm, tm), 1)
        lower = jnp.where(cc < r, 1.0, 0.0).astype(BF16)
        cum = jnp.dot(lower, c.astype(BF16), preferred_element_type=F32)
        tot = cum + base_s[0:1, :]
        p0 = jnp.sum(jnp.where(oh0, tot, 0.0), axis=1, keepdims=True)
        p1 = jnp.sum(jnp.where(oh1, tot, 0.0), axis=1, keepdims=True)
        pos = jnp.where(lane_f == 0.0, p0, jnp.where(lane_f == 1.0, p1, 0.0))
        pos_ref[...] = pos.astype(jnp.int32)
        base_s[0:1, :] = base_s[0:1, :] + csum


def _route(ri, *, n_blocks, n_exp):
    t = ri.shape[0]
    tm = min(TOKEN_TILE, t)
    nbp = -(-n_blocks // SUBLANES) * SUBLANES
    kern = functools.partial(_route_kernel, tm=tm, bm=MOE_BLOCK, n_exp=n_exp)
    return pl.pallas_call(
        kern,
        grid=(2, t // tm),
        in_specs=[pl.BlockSpec((tm, LANES), lambda p, i: (i, 0))],
        out_specs=[pl.BlockSpec((tm, LANES), lambda p, i: (i * p, 0)), _full((nbp, LANES))],
        out_shape=[jax.ShapeDtypeStruct((t, LANES), jnp.int32), jax.ShapeDtypeStruct((nbp, LANES), jnp.int32)],
        scratch_shapes=[pltpu.VMEM((SUBLANES, LANES), F32)] * 2,
        compiler_params=_cparams(("arbitrary", "arbitrary")),
        name="moe_route",
    )(ri)


def _scatter_kernel(pos_ref, x_ref, xs_in, xs_out, sem, *, tm):
    del xs_in

    def row_copy(t, k):
        return pltpu.make_async_copy(x_ref.at[pl.ds(t, 1), :], xs_out.at[pl.ds(pos_ref[2 * t + k], 1), :], sem)

    def start(t, carry):
        row_copy(t, 0).start()
        row_copy(t, 1).start()
        return carry

    def wait(t, carry):
        row_copy(t, 0).wait()
        row_copy(t, 1).wait()
        return carry

    lax.fori_loop(0, tm, start, 0)
    lax.fori_loop(0, tm, wait, 0)


def _scatter(pos_flat, x1, n_rows_padded):
    t, d = x1.shape
    tm = min(TOKEN_TILE, t)
    zeros = jnp.zeros((n_rows_padded, d), F32)
    kern = functools.partial(_scatter_kernel, tm=tm)
    return pl.pallas_call(
        kern,
        grid=(t // tm,),
        in_specs=[
            pl.BlockSpec((2 * tm,), lambda i: (i,), memory_space=pltpu.SMEM),
            pl.BlockSpec((tm, d), lambda i: (i, 0)),
            pl.BlockSpec(memory_space=pl.ANY),
        ],
        out_specs=pl.BlockSpec(memory_space=pl.ANY),
        out_shape=jax.ShapeDtypeStruct((n_rows_padded, d), F32),
        scratch_shapes=[pltpu.SemaphoreType.DMA(())],
        input_output_aliases={2: 0},
        compiler_params=_cparams(("arbitrary",)),
        name="moe_scatter",
    )(pos_flat, x1, zeros)


def _expert_kernel(be_ref, x_ref, wg_ref, wu_ref, wd_ref, y_ref):
    del be_ref
    xb = x_ref[...].astype(BF16)
    g = jnp.dot(xb, wg_ref[0], preferred_element_type=F32)
    u = jnp.dot(xb, wu_ref[0], preferred_element_type=F32)
    h = g * jax.nn.sigmoid(g) * u
    y_ref[...] = jnp.dot(h.astype(BF16), wd_ref[0], preferred_element_type=F32)


def _experts(block_expert, xs, wg, wu, wd):
    n_rows, d = xs.shape
    nb = n_rows // MOE_BLOCK
    de = wg.shape[2]
    grid_spec = pltpu.PrefetchScalarGridSpec(
        num_scalar_prefetch=1,
        grid=(nb,),
        in_specs=[
            pl.BlockSpec((MOE_BLOCK, d), lambda i, be: (i, 0)),
            pl.BlockSpec((1, d, de), lambda i, be: (be[i], 0, 0)),
            pl.BlockSpec((1, d, de), lambda i, be: (be[i], 0, 0)),
            pl.BlockSpec((1, de, d), lambda i, be: (be[i], 0, 0)),
        ],
        out_specs=pl.BlockSpec((MOE_BLOCK, d), lambda i, be: (i, 0)),
    )
    return pl.pallas_call(
        _expert_kernel,
        grid_spec=grid_spec,
        out_shape=jax.ShapeDtypeStruct((n_rows, d), F32),
        compiler_params=_cparams(("arbitrary",)),
        name="moe_experts",
    )(block_expert, xs, wg, wu, wd)


def _final_kernel(pos_ref, x1_ref, ri_ref, p_ref, ys_hbm, wpg_ref, wpp_ref, g2_ref, b2_ref, pg_ref,
                  out_ref, ybuf, sem, *, tm):
    def row_copy(t, k):
        return pltpu.make_async_copy(ys_hbm.at[pl.ds(pos_ref[2 * t + k], 1), :], ybuf.at[k, pl.ds(t, 1), :], sem)

    def start(t, carry):
        row_copy(t, 0).start()
        row_copy(t, 1).start()
        return carry

    def wait(t, carry):
        row_copy(t, 0).wait()
        row_copy(t, 1).wait()
        return carry

    lax.fori_loop(0, tm, start, 0)
    lax.fori_loop(0, tm, wait, 0)

    ri = ri_ref[...]
    ffn = ri[:, 2:3] * ybuf[0] + ri[:, 3:4] * ybuf[1]
    x2 = _layer_norm(DN_ALPHA * x1_ref[...] + ffn, g2_ref[...], b2_ref[...])
    gate = jax.nn.sigmoid(jnp.dot(x2.astype(BF16), wpg_ref[...], preferred_element_type=F32))
    proj = jnp.dot(p_ref[...].astype(BF16), wpp_ref[...], preferred_element_type=F32)
    out_ref[...] = x2 + _rms_norm(gate * proj, pg_ref[...])


def _final(pos_flat, x1, ri, p, ys, wpg, wpp, g2, b2, pg):
    t, d = x1.shape
    tm = min(TOKEN_TILE, t)
    row = lambda w: pl.BlockSpec((tm, w), lambda i: (i, 0))
    consts = [wpg, wpp, g2, b2, pg]
    kern = functools.partial(_final_kernel, tm=tm)
    return pl.pallas_call(
        kern,
        grid=(t // tm,),
        in_specs=[pl.BlockSpec((2 * tm,), lambda i: (i,), memory_space=pltpu.SMEM),
                  row(d), row(LANES), row(p.shape[1]), pl.BlockSpec(memory_space=pl.ANY)]
                 + [_full(a.shape) for a in consts],
        out_specs=row(d),
        out_shape=jax.ShapeDtypeStruct((t, d), F32),
        scratch_shapes=[pltpu.VMEM((2, tm, d), F32), pltpu.SemaphoreType.DMA(())],
        compiler_params=_cparams(("arbitrary",)),
        name="combine_ln2_ple",
    )(pos_flat, x1, ri, p, ys, *consts)


def _lane_patterns(pair_layout):
    lane = np.arange(LANES)
    if pair_layout:
        valid = (lane < 2 * QK_ROPE).astype(np.float32)
    else:
        valid = np.ones(LANES, np.float32)
    sign = np.where((lane % QK_ROPE) < HALF_ROPE, -1.0, 1.0).astype(np.float32) * valid
    return lane % HALF_ROPE, sign, valid


def _prep_weights(w_in, w_uq, w_uk, w_uv, w_group, w_router):
    d = w_in.shape[0]
    ql = w_uq.shape[0]
    kvl, n_heads, _ = w_uk.shape
    o_kv, o_kr = ql, ql + kvl
    o_conv = o_kr + QK_ROPE
    c_conv = (w_in.shape[1] - o_conv - 2 * d) // 2
    o_ga = o_conv + 2 * c_conv
    o_gb = o_ga + d
    kr = w_in[:, o_kr:o_conv]
    kr_sw = jnp.concatenate([kr[:, HALF_ROPE:], kr[:, :HALF_ROPE]], axis=1)
    zpad = jnp.zeros((d, LANES - 2 * QK_ROPE), w_in.dtype)
    win_r = jnp.concatenate([
        w_in[:, :o_kv], w_in[:, o_kv:o_kr], w_in[:, o_conv:o_ga], w_in[:, o_ga:o_gb], w_in[:, o_gb:],
        kr, kr, zpad, kr_sw, kr_sw, zpad], axis=1).astype(BF16)
    assert win_r.shape[1] == C_END and c_conv == C_CB - C_CA and kvl == C_CA - C_KV and ql == C_KV

    hd = QK_NOPE + QK_ROPE
    wq = w_uq.reshape(ql, n_heads, hd)
    nope = wq[:, :, :QK_NOPE].reshape(ql, n_heads * QK_NOPE)
    x1 = wq[:, :, QK_NOPE:QK_NOPE + HALF_ROPE]
    x2 = wq[:, :, QK_NOPE + HALF_ROPE:]
    rope_a = jnp.concatenate([x1, x2], axis=2)
    rope_b = jnp.concatenate([x2, x1], axis=2)
    wuq_s = jnp.concatenate([nope, rope_a.reshape(ql, -1), rope_b.reshape(ql, -1)], axis=1).astype(BF16)

    def pair_layout(r):
        r = r.reshape(ql, n_heads // 2, 2 * QK_ROPE)
        z = jnp.zeros((ql, n_heads // 2, LANES - 2 * QK_ROPE), r.dtype)
        return jnp.concatenate([r, z], axis=2).reshape(ql, -1)

    wuq_p = jnp.concatenate([nope, pair_layout(rope_a), pair_layout(rope_b)], axis=1).astype(BF16)

    wuk_flat = w_uk.reshape(kvl, n_heads * QK_NOPE)
    wuv_flat = w_uv.reshape(kvl, n_heads * V_HEAD)
    wukv = jnp.concatenate([wuk_flat, wuv_flat], axis=1).astype(BF16)
    eye = jnp.eye(n_heads, dtype=w_uk.dtype)
    wk_bd = jnp.einsum('lhn,hg->hngl', w_uk, eye).reshape(n_heads * QK_NOPE, n_heads * kvl).astype(BF16)
    wv_bd = jnp.einsum('lhv,hg->hlgv', w_uv, eye).reshape(n_heads * kvl, n_heads * V_HEAD).astype(BF16)

    n_exp = w_router.shape[1]
    wgr = jnp.concatenate([w_group, w_router,
                           jnp.zeros((d, LANES - N_GROUPS - n_exp), w_group.dtype)], axis=1)
    return win_r, wuq_p, wuq_s, wukv, wk_bd, wv_bd, wgr


def _moe(x1, ri, wg, wu, wd, n_exp):
    t = x1.shape[0]
    n_assign = 2 * t
    n_blocks = (n_assign + n_exp * (MOE_BLOCK - 1) + MOE_BLOCK - 1) // MOE_BLOCK
    pos, be = _route(ri, n_blocks=n_blocks, n_exp=n_exp)
    pos_flat = pos[:, :2].reshape(-1)
    block_expert = be[:n_blocks, 0]
    xs = _scatter(pos_flat, x1, n_blocks * MOE_BLOCK)
    ys = _experts(block_expert, xs, wg, wu, wd)
    return pos_flat, ys


def kernel(x_prompt, x_sample, cache_ckv, cache_krope, state_conv, page_table, p_prompt, p_sample, w_in, q_norm_g, w_uq, kv_norm_g, w_uk, w_uv, w_o_attn, w_dw, b_dw, conv_ln_g, conv_ln_b, w_pw2, w_out, ln1_g, ln1_b, w_group, w_router, w_gate, w_up, w_down, ln2_g, ln2_b, w_ple_gate, w_ple_proj, ple_norm_g):
    assert w_in.shape[0] == DEPTH
    b, s, d = x_prompt.shape
    bd, sd, _ = x_sample.shape
    assert sd == 1
    n_pages = page_table.shape[1]
    page_rows = cache_ckv.shape[2]
    past = n_pages * page_rows
    n_heads = w_uk.shape[2]
    kvl = w_uk.shape[1]
    width = w_dw.shape[1]
    n_exp = w_router.shape[2]
    c_conv = w_dw.shape[2]

    win_r, wuq_p, wuq_s, wukv, wk_bd, wv_bd, wgr = _prep_weights(
        w_in[0], w_uq[0], w_uk[0], w_uv[0], w_group[0], w_router[0])
    qg, kvg = q_norm_g, kv_norm_g
    wo, wpw, wout = w_o_attn[0].astype(BF16), w_pw2[0].astype(BF16), w_out[0].astype(BF16)
    wgate, wup, wdown = w_gate[0].astype(BF16), w_up[0].astype(BF16), w_down[0].astype(BF16)
    wpg, wpp = w_ple_gate[0].astype(BF16), w_ple_proj[0].astype(BF16)
    w_dw_pad = jnp.concatenate([w_dw[0], jnp.zeros((CONV_HALO - width, c_conv), F32)], axis=0)

    inv16 = 1.0 / (ROPE_THETA ** (jnp.arange(HALF_ROPE, dtype=F32) / HALF_ROPE))

    def tables(pair_layout, n_rows, offset, step):
        idx, sign, valid = _lane_patterns(pair_layout)
        inv_l = (inv16[idx] * valid)[None, :]
        return _rope_tables(n_rows, offset, step, inv_l, jnp.asarray(sign)[None, :], jnp.asarray(valid)[None, :])

    def trunk_tail(x2d, o, y, sa, sb, p2d):
        x1, ri = _mix(x2d, o, y, sa, sb, wo, wpw, wout, conv_ln_g, conv_ln_b, ln1_g, ln1_b, wgr)
        pos_flat, ys = _moe(x1, ri, wgate, wup, wdown, n_exp)
        return _final(pos_flat, x1, ri, p2d, ys, wpg, wpp, ln2_g, ln2_b, ple_norm_g)

    xs_ = x_sample.reshape(bd, d)
    cos_s, sin_s = tables(False, SUBLANES, past, 0)
    qn_s, qr_s, ckv_s, kr_s, u_s, sa_s, sb_s = _inproj(
        xs_, win_r, qg, kvg, wuq_s, None, cos_s, sin_s, n_rep=n_heads * QK_ROPE // LANES, seq=1, make_kv=False)
    q_lat = _dense(qn_s, wk_bd, F32).reshape(bd, n_heads, kvl)
    o_lat = _sample_attention(page_table, q_lat, qr_s.reshape(bd, n_heads, QK_ROPE),
                              ckv_s.reshape(bd, 1, kvl), kr_s.reshape(bd, 1, QK_ROPE),
                              cache_ckv[0], cache_krope[0])
    o_s = _dense(o_lat.reshape(bd, n_heads * kvl), wv_bd, BF16)
    ext_s = jnp.concatenate([state_conv[0], u_s[:, None, :]], axis=1)
    y_s = _conv_sample(jnp.transpose(ext_s, (1, 0, 2)), w_dw_pad, b_dw, width=width)
    out_s = trunk_tail(xs_, o_s, y_s, sa_s, sb_s, p_sample[0].reshape(bd, -1))
    new_conv_s = ext_s[:, 1:, :]

    xp = x_prompt.reshape(b * s, d)
    cos_p, sin_p = tables(True, s, 0, 1)
    qn, qr, kn, v, krd, ckv_p, kr_p, u_p, sa_p, sb_p = _inproj(
        xp, win_r, qg, kvg, wuq_p, wukv, cos_p, sin_p, n_rep=n_heads // 2, seq=s, make_kv=True)
    o_p = _prompt_attention(qn, qr, kn, krd, v, batch=b, seq=s)
    y_p = _conv_prompt(u_p, w_dw_pad, b_dw, batch=b, seq=s, width=width)
    out_p = trunk_tail(xp, o_p, y_p, sa_p, sb_p, p_prompt[0].reshape(b * s, -1))
    u_p3 = u_p.reshape(b, s, c_conv)
    new_conv_p = u_p3[:, s - (width - 1):, :]

    return (out_p.reshape(b, s, d), out_s.reshape(bd, 1, d),
            ckv_p.reshape(1, b, s, kvl), kr_p.reshape(1, b, s, QK_ROPE), new_conv_p[None],
            ckv_s.reshape(1, bd, 1, kvl), kr_s.reshape(1, bd, 1, QK_ROPE), new_conv_s[None])
```

```python
import functools

import numpy as np
import jax
import jax.numpy as jnp
from jax import lax
from jax.experimental import pallas as pl
from jax.experimental.pallas import tpu as pltpu

F32 = jnp.float32
BF16 = jnp.bfloat16

QK_NOPE = 64
QK_ROPE = 32
HALF_ROPE = QK_ROPE // 2
V_HEAD = 64
ROPE_THETA = 10000.0
ATTN_SCALE = (QK_NOPE + QK_ROPE) ** -0.5
LOG2E = 1.4426950408889634
N_GROUPS = 4
EXPERTS_PER_GROUP = 8
DEPTH = 1
DN_ALPHA = (2 * DEPTH) ** 0.25
LN_EPS = 1e-5
RMS_EPS = 1e-6
NEG_INF = -1e30

LANES = 128
SUBLANES = 8
VMEM_LIMIT = 56 * 1024 * 1024

TOKEN_TILE = 512
ATTN_TILE = 512
CONV_CHUNK = 64
CONV_HALO = 32
MOE_BLOCK = 256
PAGES_PER_CHUNK = 16
SEQS_PER_STEP = 4
DMA_UNROLL = 8


def _cparams(sem, vmem=VMEM_LIMIT):
    return pltpu.CompilerParams(dimension_semantics=sem, vmem_limit_bytes=vmem)


def _full(shape):
    n = len(shape)
    return pl.BlockSpec(shape, lambda *_: (0,) * n)


def _layer_norm(x, g, b):
    mu = jnp.mean(x, axis=-1, keepdims=True)
    xc = x - mu
    var = jnp.mean(xc * xc, axis=-1, keepdims=True)
    return xc * lax.rsqrt(var + LN_EPS) * g + b


def _rms_norm(x, g):
    return x * lax.rsqrt(jnp.mean(x * x, axis=-1, keepdims=True) + RMS_EPS) * g


def _rope_table_kernel(inv_ref, sign_ref, valid_ref, cos_ref, sin_ref, *, offset, step):
    rows = cos_ref.shape[0]
    r = lax.broadcasted_iota(jnp.int32, (rows, LANES), 0) + pl.program_id(0) * rows
    pos = (offset + step * r).astype(F32)
    ang = pos * inv_ref[...]
    cos_ref[...] = jnp.cos(ang) * valid_ref[...]
    sin_ref[...] = jnp.sin(ang) * sign_ref[...]


def _rope_tables(n_rows, offset, step, inv_l, sign_l, valid_l):
    tr = min(n_rows, 512)
    kern = functools.partial(_rope_table_kernel, offset=offset, step=step)
    return pl.pallas_call(
        kern,
        grid=(n_rows // tr,),
        in_specs=[_full((1, LANES))] * 3,
        out_specs=[pl.BlockSpec((tr, LANES), lambda i: (i, 0))] * 2,
        out_shape=[jax.ShapeDtypeStruct((n_rows, LANES), F32)] * 2,
        compiler_params=_cparams(("arbitrary",)),
        name="rope_tables",
    )(inv_l, sign_l, valid_l)


C_Q, C_KV, C_CA, C_CB, C_GA, C_GB, C_KA, C_KB, C_END = 0, 384, 640, 1152, 1664, 2688, 3712, 3840, 3968


def _inproj_kernel(x_ref, win_ref, qg_ref, kvg_ref, wuq_ref, *rest, n_rep, make_kv, bcast, q_scale):
    if make_kv:
        (wukv_ref, cos_ref, sin_ref, qn_o, qr_o, kn_o, v_o, krd_o, ckv_o, kr_o, u_o, sa_o, sb_o) = rest
    else:
        (cos_ref, sin_ref, qn_o, qr_o, ckv_o, kr_o, u_o, sa_o, sb_o) = rest
    xb = x_ref[...].astype(BF16)

    def proj(a, b):
        return jnp.dot(xb, win_ref[:, a:b], preferred_element_type=F32)

    if bcast:
        c1, s1 = cos_ref[0:1, :], sin_ref[0:1, :]
    else:
        c1, s1 = cos_ref[...], sin_ref[...]
    cn = jnp.concatenate([c1] * n_rep, axis=1)
    sn = jnp.concatenate([s1] * n_rep, axis=1)
    d_nope = qn_o.shape[1]
    nr = LANES * n_rep

    cqn = _rms_norm(proj(C_Q, C_KV), qg_ref[...])
    qall = jnp.dot(cqn.astype(BF16), wuq_ref[...], preferred_element_type=F32)
    qn_o[...] = (qall[:, :d_nope] * q_scale).astype(qn_o.dtype)
    qa = qall[:, d_nope:d_nope + nr]
    qb = qall[:, d_nope + nr:d_nope + 2 * nr]
    qr_o[...] = ((qa * cn + qb * sn) * q_scale).astype(qr_o.dtype)

    ckvn = _rms_norm(proj(C_KV, C_CA), kvg_ref[...])
    ckv_o[...] = ckvn
    krot = proj(C_KA, C_KB) * c1 + proj(C_KB, C_END) * s1
    kr_o[...] = krot[:, :QK_ROPE]
    if make_kv:
        kv = jnp.dot(ckvn.astype(BF16), wukv_ref[...], preferred_element_type=F32)
        half = kv.shape[1] // 2
        kn_o[...] = kv[:, :half].astype(BF16)
        v_o[...] = kv[:, half:].astype(BF16)
        krd_o[...] = krot.astype(BF16)

    u_o[...] = proj(C_CA, C_CB) * jax.nn.sigmoid(proj(C_CB, C_GA))
    sa_o[...] = jax.nn.sigmoid(proj(C_GA, C_GB)).astype(BF16)
    sb_o[...] = jax.nn.sigmoid(proj(C_GB, C_KA)).astype(BF16)


def _inproj(x, win_r, qg, kvg, wuq_r, wukv, cos_t, sin_t, *, n_rep, seq, make_kv):
    t, d = x.shape
    tm = min(TOKEN_TILE, t)
    d_nope = wuq_r.shape[1] - 2 * LANES * n_rep
    nr = LANES * n_rep
    c_conv = C_CB - C_CA
    kvl = C_CA - C_KV
    bcast = seq < tm
    if bcast:
        tab_spec = _full(cos_t.shape)
    else:
        nst = seq // tm
        tab_spec = pl.BlockSpec((tm, LANES), lambda i: (i % nst, 0))
    row = lambda w: pl.BlockSpec((tm, w), lambda i: (i, 0))
    in_specs = [row(d), _full(win_r.shape), _full(qg.shape), _full(kvg.shape), _full(wuq_r.shape)]
    args = [x, win_r, qg, kvg, wuq_r]
    if make_kv:
        in_specs.append(_full(wukv.shape))
        args.append(wukv)
    in_specs += [tab_spec, tab_spec]
    args += [cos_t, sin_t]
    q_dt = BF16 if make_kv else F32
    outs = [(d_nope, BF16), (nr, q_dt)]
    if make_kv:
        outs += [(d_nope, BF16), (d_nope, BF16), (LANES, BF16)]
    outs += [(kvl, F32), (QK_ROPE, F32), (c_conv, F32), (d, BF16), (d, BF16)]
    q_scale = ATTN_SCALE * LOG2E if make_kv else ATTN_SCALE
    kern = functools.partial(_inproj_kernel, n_rep=n_rep, make_kv=make_kv, bcast=bcast, q_scale=q_scale)
    return pl.pallas_call(
        kern,
        grid=(t // tm,),
        in_specs=in_specs,
        out_specs=[row(w) for w, _ in outs],
        out_shape=[jax.ShapeDtypeStruct((t, w), dt) for w, dt in outs],
        compiler_params=_cparams(("arbitrary",)),
        name="inproj_kv" if make_kv else "inproj_q",
    )(*args)


def _pattn_kernel(qn_ref, qr_ref, kn_ref, krd_ref, v_ref, o_ref, m_s, l_s, acc_s, *, blk):
    i = pl.program_id(2)
    lane = lax.broadcasted_iota(jnp.int32, (blk, LANES), 1)
    qn = qn_ref[...]
    qr = qr_ref[...]
    zero = jnp.zeros_like(qn)
    q_heads = (
        jnp.concatenate([jnp.where(lane < QK_NOPE, qn, zero), jnp.where(lane < QK_ROPE, qr, zero)], axis=1),
        jnp.concatenate([jnp.where(lane >= QK_NOPE, qn, zero),
                         jnp.where((lane >= QK_ROPE) & (lane < 2 * QK_ROPE), qr, zero)], axis=1),
    )
    m_s[...] = jnp.full(m_s.shape, NEG_INF, F32)
    l_s[...] = jnp.zeros(l_s.shape, F32)
    acc_s[...] = jnp.zeros(acc_s.shape, F32)
    n_rep = blk // LANES

    def step(j, masked):
        start = pl.multiple_of(j * blk, blk)
        k = jnp.concatenate([kn_ref[pl.ds(start, blk), :], krd_ref[pl.ds(start, blk), :]], axis=1)
        v = v_ref[pl.ds(start, blk), :]
        for h in range(2):
            s = lax.dot_general(q_heads[h], k, (((1,), (1,)), ((), ())), preferred_element_type=F32)
            if masked:
                row = lax.broadcasted_iota(jnp.int32, (blk, blk), 0)
                col = lax.broadcasted_iota(jnp.int32, (blk, blk), 1)
                s = jnp.where(col <= row, s, NEG_INF)
            m_prev = m_s[h]
            m_new = jnp.maximum(m_prev, jnp.max(s, axis=1, keepdims=True))
            alpha = jnp.exp2(m_prev - m_new)
            p = jnp.exp2(s - jnp.concatenate([m_new] * n_rep, axis=1))
            l_s[h] = alpha * l_s[h] + jnp.sum(p, axis=1, keepdims=True)
            acc_s[h] = alpha * acc_s[h] + jnp.dot(p.astype(BF16), v, preferred_element_type=F32)
            m_s[h] = m_new

    def body(j, carry):
        step(j, False)
        return carry

    lax.fori_loop(0, i, body, 0)
    step(i, True)
    o = jnp.where(lane < V_HEAD, acc_s[0] / l_s[0], acc_s[1] / l_s[1])
    o_ref[...] = o.astype(o_ref.dtype)


def _prompt_attention(qn, qr, kn, krd, v, *, batch, seq):
    t, d = qn.shape
    blk = min(ATTN_TILE, seq)
    nq = seq // blk
    n_pairs = d // LANES
    qspec = pl.BlockSpec((blk, LANES), lambda b, j, i: (b * nq + i, j))
    kspec = pl.BlockSpec((seq, LANES), lambda b, j, i: (b, j))
    kern = functools.partial(_pattn_kernel, blk=blk)
    return pl.pallas_call(
        kern,
        grid=(batch, n_pairs, nq),
        in_specs=[qspec, qspec, kspec, pl.BlockSpec((seq, LANES), lambda b, j, i: (b, 0)), kspec],
        out_specs=qspec,
        out_shape=jax.ShapeDtypeStruct((t, d), BF16),
        scratch_shapes=[pltpu.VMEM((2, blk, LANES), F32)] * 3,
        compiler_params=_cparams(("arbitrary",) * 3),
        name="prompt_attention",
    )(qn, qr, kn, krd, v)


def _dense_kernel(a_ref, w_ref, o_ref):
    o_ref[...] = jnp.dot(a_ref[...].astype(BF16), w_ref[...], preferred_element_type=F32).astype(o_ref.dtype)


def _dense(a, w, out_dtype):
    m, n = a.shape[0], w.shape[1]
    return pl.pallas_call(
        _dense_kernel,
        grid=(1,),
        in_specs=[_full(a.shape), _full(w.shape)],
        out_specs=_full((m, n)),
        out_shape=jax.ShapeDtypeStruct((m, n), out_dtype),
        compiler_params=_cparams(("arbitrary",)),
        name="dense",
    )(a, w)


def _sattn_kernel(pt_ref, ptn_ref, ql_ref, qr_ref, cnew_ref, knew_ref, ckv_hbm, krt_hbm, o_ref,
                  cbuf, kbuf, sems, *, n_chunks, ppc, nb):
    step = pl.program_id(0)
    n_steps = pl.num_programs(0)
    kvl = cbuf.shape[4]
    kc = ppc * cbuf.shape[3]
    n_heads = ql_ref.shape[1]
    dn_t = (((1,), (1,)), ((), ()))

    def chunk_copies(tab_ref, chunk, slot):
        out = []
        for b in range(nb):
            for pg in range(ppc):
                page = tab_ref[b, 0, chunk * ppc + pg]
                out.append(pltpu.make_async_copy(ckv_hbm.at[page], cbuf.at[slot, b, pg], sems.at[0, slot]))
                out.append(pltpu.make_async_copy(krt_hbm.at[page], kbuf.at[slot, b, pg], sems.at[1, slot]))
        return out

    def start_chunk(tab_ref, chunk, slot):
        for cp in chunk_copies(tab_ref, chunk, slot):
            cp.start()

    @pl.when(step == 0)
    def _():
        start_chunk(pt_ref, 0, 0)

    first = step * n_chunks
    qls = [ql_ref[b].astype(BF16) for b in range(nb)]
    qrs = [qr_ref[b].astype(BF16) for b in range(nb)]

    def body(c, carry):
        slot = (first + c) % 2

        @pl.when(c + 1 < n_chunks)
        def _():
            start_chunk(pt_ref, c + 1, 1 - slot)

        @pl.when((c + 1 == n_chunks) & (step + 1 < n_steps))
        def _():
            start_chunk(ptn_ref, 0, 1 - slot)

        for cp in chunk_copies(pt_ref, c, slot):
            cp.wait()
        new = []
        for b in range(nb):
            m, l, acc = carry[3 * b:3 * b + 3]
            ck = cbuf[slot, b].reshape(kc, kvl).astype(BF16)
            krt = jnp.concatenate([kbuf[slot, b, pg] for pg in range(ppc)], axis=1).astype(BF16)
            s = lax.dot_general(qls[b], ck, dn_t, preferred_element_type=F32)
            s = s + jnp.dot(qrs[b], krt, preferred_element_type=F32)
            m_new = jnp.maximum(m, jnp.max(s, axis=1, keepdims=True))
            alpha = jnp.exp(m - m_new)
            p = jnp.exp(s - m_new)
            l = alpha * l + jnp.sum(p, axis=1, keepdims=True)
            acc = alpha * acc + jnp.dot(p.astype(BF16), ck, preferred_element_type=F32)
            new += [m_new, l, acc]
        return tuple(new)

    init = (jnp.full((n_heads, 1), NEG_INF, F32), jnp.zeros((n_heads, 1), F32),
            jnp.zeros((n_heads, kvl), F32)) * nb
    carry = lax.fori_loop(0, n_chunks, body, init)

    for b in range(nb):
        m, l, acc = carry[3 * b:3 * b + 3]
        cnew = cnew_ref[b]
        knew = knew_ref[b]
        s_new = (jnp.sum(ql_ref[b] * cnew, axis=1, keepdims=True)
                 + jnp.sum(qr_ref[b] * knew, axis=1, keepdims=True))
        m_new = jnp.maximum(m, s_new)
        alpha = jnp.exp(m - m_new)
        p_new = jnp.exp(s_new - m_new)
        l = alpha * l + p_new
        acc = alpha * acc + p_new * cnew
        o_ref[b] = acc / l


def _sample_attention(page_table, q_lat, q_rope, c_new, k_new, cache_ckv, cache_krope_t):
    bd, n_heads, kvl = q_lat.shape
    n_pages = page_table.shape[1]
    page_rows = cache_ckv.shape[1]
    ppc = min(PAGES_PER_CHUNK, n_pages)
    n_chunks = n_pages // ppc
    nb = min(SEQS_PER_STEP, bd)
    n_steps = bd // nb
    pt3 = page_table.reshape(bd, 1, n_pages)
    per_b = lambda s: pl.BlockSpec((nb,) + s, lambda i: (i, 0, 0))
    kern = functools.partial(_sattn_kernel, n_chunks=n_chunks, ppc=ppc, nb=nb)
    return pl.pallas_call(
        kern,
        grid=(n_steps,),
        in_specs=[
            pl.BlockSpec((nb, 1, n_pages), lambda i: (i, 0, 0), memory_space=pltpu.SMEM),
            pl.BlockSpec((nb, 1, n_pages), lambda i: (jnp.minimum(i + 1, n_steps - 1), 0, 0),
                         memory_space=pltpu.SMEM),
            per_b((n_heads, kvl)), per_b((n_heads, QK_ROPE)), per_b((1, kvl)), per_b((1, QK_ROPE)),
            pl.BlockSpec(memory_space=pl.ANY), pl.BlockSpec(memory_space=pl.ANY),
        ],
        out_specs=per_b((n_heads, kvl)),
        out_shape=jax.ShapeDtypeStruct((bd, n_heads, kvl), F32),
        scratch_shapes=[
            pltpu.VMEM((2, nb, ppc, page_rows, kvl), F32),
            pltpu.VMEM((2, nb, ppc, QK_ROPE, page_rows), F32),
            pltpu.SemaphoreType.DMA((2, 2)),
        ],
        compiler_params=_cparams(("arbitrary",)),
        name="sample_attention",
    )(pt3, pt3, q_lat, q_rope, c_new, k_new, cache_ckv, cache_krope_t)


def _conv_prompt_kernel(u_ref, halo_ref, w_ref, b_ref, y_ref, ext_s, *, ts, width):
    i = pl.program_id(1)
    left = jnp.where(i > 0, halo_ref[...], jnp.zeros_like(halo_ref))
    ext_s[0:CONV_HALO, :] = left
    ext_s[CONV_HALO:CONV_HALO + ts, :] = u_ref[...]
    base = CONV_HALO - (width - 1)
    for c in range(ts // CONV_CHUNK):
        r0 = c * CONV_CHUNK
        acc = jnp.broadcast_to(b_ref[...], (CONV_CHUNK, u_ref.shape[1]))
        for k in range(width):
            acc = acc + w_ref[k:k + 1, :] * ext_s[r0 + base + k:r0 + base + k + CONV_CHUNK, :]
        y_ref[r0:r0 + CONV_CHUNK, :] = acc


def _conv_prompt(u, w_pad, b, *, batch, seq, width):
    t, c = u.shape
    ts = min(TOKEN_TILE, seq)
    ns = seq // ts
    hb = ts // CONV_HALO
    kern = functools.partial(_conv_prompt_kernel, ts=ts, width=width)
    return pl.pallas_call(
        kern,
        grid=(batch, ns),
        in_specs=[
            pl.BlockSpec((ts, c), lambda b_, i: (b_ * ns + i, 0)),
            pl.BlockSpec((CONV_HALO, c), lambda b_, i: (jnp.maximum((b_ * ns + i) * hb - 1, 0), 0)),
            _full(w_pad.shape), _full(b.shape),
        ],
        out_specs=pl.BlockSpec((ts, c), lambda b_, i: (b_ * ns + i, 0)),
        out_shape=jax.ShapeDtypeStruct((t, c), F32),
        scratch_shapes=[pltpu.VMEM((CONV_HALO + ts, c), F32)],
        compiler_params=_cparams(("arbitrary", "arbitrary")),
        name="conv_prompt",
    )(u, u, w_pad, b)


def _conv_sample_kernel(ext_ref, w_ref, b_ref, y_ref, *, width):
    acc = jnp.broadcast_to(b_ref[...], y_ref.shape)
    for k in range(width):
        acc = acc + w_ref[k:k + 1, :] * ext_ref[k]
    y_ref[...] = acc


def _conv_sample(ext_t, w_pad, b, *, width):
    _, bd, c = ext_t.shape
    kern = functools.partial(_conv_sample_kernel, width=width)
    return pl.pallas_call(
        kern,
        grid=(1,),
        in_specs=[_full(ext_t.shape), _full(w_pad.shape), _full(b.shape)],
        out_specs=_full((bd, c)),
        out_shape=jax.ShapeDtypeStruct((bd, c), F32),
        compiler_params=_cparams(("arbitrary",)),
        name="conv_sample",
    )(ext_t, w_pad, b)


def _store_token_tiles(ref, x):
    rows, d = x.shape
    tiles = d // LANES
    for c in range(tiles):
        ref[pl.ds(c, rows, stride=tiles), :] = x[:, c * LANES:(c + 1) * LANES]


def _load_token_tiles(ref, rows, tiles):
    return jnp.concatenate([ref[pl.ds(c, rows, stride=tiles), :] for c in range(tiles)], axis=1)


def _mix_kernel(x_ref, o_ref, y_ref, sa_ref, sb_ref, wo_ref, wpw_ref, wout_ref, cg_ref, cb_ref,
                g1_ref, b1_ref, wgh_ref, wgl_ref, x1_ref, x1t_ref, ri_ref):
    branch_b = jnp.dot(o_ref[...], wo_ref[...], preferred_element_type=F32)
    z = _layer_norm(y_ref[...], cg_ref[...], cb_ref[...])
    z = z * jax.nn.sigmoid(z)
    branch_a = jnp.dot(z.astype(BF16), wpw_ref[...], preferred_element_type=F32)
    mixin = sa_ref[...].astype(F32) * branch_a + sb_ref[...].astype(F32) * branch_b
    mix = jnp.dot(mixin.astype(BF16), wout_ref[...], preferred_element_type=F32)
    x1 = _layer_norm(DN_ALPHA * x_ref[...] + mix, g1_ref[...], b1_ref[...])
    x1_ref[...] = x1
    _store_token_tiles(x1t_ref, x1)

    x_hi = x1.astype(BF16)
    x_lo = (x1 - x_hi.astype(F32)).astype(BF16)
    lg = (jnp.dot(x_hi, wgh_ref[...], preferred_element_type=F32)
          + jnp.dot(x_lo, wgh_ref[...], preferred_element_type=F32)
          + jnp.dot(x_hi, wgl_ref[...], preferred_element_type=F32))
    n_exp = N_GROUPS * EXPERTS_PER_GROUP
    lane = lax.broadcasted_iota(jnp.int32, lg.shape, 1)
    lane_f = lane.astype(F32)
    big = float(LANES)
    gmask = lane < N_GROUPS
    lgm = jnp.where(gmask, lg, NEG_INF)
    gmax = jnp.max(lgm, axis=1, keepdims=True)
    gidx = jnp.min(jnp.where(lgm == gmax, lane_f, big), axis=1, keepdims=True)
    pg_sel = 1.0 / jnp.sum(jnp.where(gmask, jnp.exp(lgm - gmax), 0.0), axis=1, keepdims=True)
    egroup = jnp.floor((lane_f - N_GROUPS) * (1.0 / EXPERTS_PER_GROUP))
    emask = (lane >= N_GROUPS) & (lane < N_GROUPS + n_exp) & (egroup == gidx)
    le = jnp.where(emask, lg, NEG_INF)
    v1 = jnp.max(le, axis=1, keepdims=True)
    i1 = jnp.min(jnp.where(le == v1, lane_f, big), axis=1, keepdims=True)
    le2 = jnp.where(lane_f == i1, NEG_INF, le)
    v2 = jnp.max(le2, axis=1, keepdims=True)
    i2 = jnp.min(jnp.where(le2 == v2, lane_f, big), axis=1, keepdims=True)
    e = jnp.exp(v2 - v1)
    gate1 = pg_sel / (1.0 + e)
    gate2 = pg_sel * e / (1.0 + e)
    ri = jnp.where(lane == 0, i1 - N_GROUPS,
                   jnp.where(lane == 1, i2 - N_GROUPS,
                             jnp.where(lane == 2, gate1, jnp.where(lane == 3, gate2, 0.0))))
    ri_ref[...] = ri


def _mix(x, o, y, sa, sb, wo, wpw, wout, cg, cb, g1, b1, wgh, wgl):
    t, d = x.shape
    tm = min(TOKEN_TILE, t)
    row = lambda w: pl.BlockSpec((tm, w), lambda i: (i, 0))
    consts = [wo, wpw, wout, cg, cb, g1, b1, wgh, wgl]
    tiles = d // LANES
    return pl.pallas_call(
        _mix_kernel,
        grid=(t // tm,),
        in_specs=[row(d), row(d), row(y.shape[1]), row(d), row(d)] + [_full(a.shape) for a in consts],
        out_specs=[row(d), pl.BlockSpec((tm * tiles, LANES), lambda i: (i, 0)), row(LANES)],
        out_shape=[jax.ShapeDtypeStruct((t, d), F32), jax.ShapeDtypeStruct((t * tiles, LANES), F32),
                   jax.ShapeDtypeStruct((t, LANES), F32)],
        compiler_params=_cparams(("arbitrary",)),
        name="mix_ln1_router",
    )(x, o, y, sa, sb, *consts)


def _route_kernel(ri_ref, pos_ref, be_ref, cnt_s, base_s, *, tm, bm, n_exp):
    ph = pl.program_id(0)
    i = pl.program_id(1)
    lane_f = lax.broadcasted_iota(jnp.int32, (tm, LANES), 1).astype(F32)
    ri = ri_ref[...]
    oh0 = lane_f == ri[:, 0:1]
    oh1 = lane_f == ri[:, 1:2]
    c = jnp.where(oh0 | oh1, 1.0, 0.0)
    csum = jnp.sum(c, axis=0, keepdims=True)

    @pl.when((ph == 0) & (i == 0))
    def _():
        cnt_s[...] = jnp.zeros(cnt_s.shape, F32)

    @pl.when(ph == 0)
    def _():
        cnt_s[0:1, :] = cnt_s[0:1, :] + csum

    @pl.when((ph == 1) & (i == 0))
    def _():
        cnt = cnt_s[...]
        pc = jnp.floor((cnt + (bm - 1)) * (1.0 / bm)) * bm
        r = lax.broadcasted_iota(jnp.int32, (LANES, LANES), 0)
        cc = lax.broadcasted_iota(jnp.int32, (LANES, LANES), 1)
        upper = jnp.where(r <= cc, 1.0, 0.0)
        pend = jnp.dot(pc, upper, precision=lax.Precision.HIGHEST, preferred_element_type=F32)
        base_s[...] = pend - pc
        nbp = be_ref.shape[0]
        bstart = (lax.broadcasted_iota(jnp.int32, (nbp, LANES), 0) * bm).astype(F32)
        lane_b = lax.broadcasted_iota(jnp.int32, (nbp, LANES), 1)
        hit = jnp.where((lane_b < n_exp) & (pend[0:1, :] <= bstart), 1.0, 0.0)
        be = jnp.minimum(jnp.sum(hit, axis=1, keepdims=True), float(n_exp - 1))
        pstart = pend[0:1, :] - pc[0:1, :]
        span = jnp.minimum(pstart + cnt[0:1, :], bstart + bm) - jnp.maximum(pstart, bstart)
        nvalid = jnp.sum(jnp.where(lane_b < n_exp, jnp.maximum(span, 0.0), 0.0), axis=1, keepdims=True)
        be_ref[...] = jnp.where(lane_b == 0, be, jnp.where(lane_b == 1, nvalid, 0.0)).astype(jnp.int32)

    @pl.when(ph == 1)
    def _():
        r = lax.broadcasted_iota(jnp.int32, (tm, tm), 0)
        cc = lax.broadcasted_iota(jnp.int32, (tm, tm), 1)
        lower = jnp.where(cc < r, 1.0, 0.0).astype(BF16)
        cum = jnp.dot(lower, c.astype(BF16), preferred_element_type=F32)
        tot = cum + base_s[0:1, :]
        p0 = jnp.sum(jnp.where(oh0, tot, 0.0), axis=1, keepdims=True)
        p1 = jnp.sum(jnp.where(oh1, tot, 0.0), axis=1, keepdims=True)
        pos = jnp.where(lane_f == 0.0, p0, jnp.where(lane_f == 1.0, p1, 0.0))
        pos_ref[...] = pos.astype(jnp.int32)
        base_s[0:1, :] = base_s[0:1, :] + csum


def _route(ri, *, n_blocks, n_exp):
    t = ri.shape[0]
    tm = min(TOKEN_TILE, t)
    nbp = -(-n_blocks // SUBLANES) * SUBLANES
    kern = functools.partial(_route_kernel, tm=tm, bm=MOE_BLOCK, n_exp=n_exp)
    return pl.pallas_call(
        kern,
        grid=(2, t // tm),
        in_specs=[pl.BlockSpec((tm, LANES), lambda p, i: (i, 0))],
        out_specs=[pl.BlockSpec((tm, LANES), lambda p, i: (i * p, 0)), _full((nbp, LANES))],
        out_shape=[jax.ShapeDtypeStruct((t, LANES), jnp.int32), jax.ShapeDtypeStruct((nbp, LANES), jnp.int32)],
        scratch_shapes=[pltpu.VMEM((SUBLANES, LANES), F32)] * 2,
        compiler_params=_cparams(("arbitrary", "arbitrary")),
        name="moe_route",
    )(ri)


def _scatter_kernel(pos_ref, nv_ref, x1t_hbm, xs_out, zbuf, sem, zsem, *, tm, tiles, n_blocks):
    step = pl.program_id(0)
    blk_rows = MOE_BLOCK * tiles

    def zero_copy(i):
        dst = pl.multiple_of(i * blk_rows, blk_rows)
        return pltpu.make_async_copy(zbuf, xs_out.at[pl.ds(dst, blk_rows), :], zsem)

    @pl.when(step == 0)
    def _():
        zbuf[...] = jnp.zeros(zbuf.shape, F32)

        def zstart(i, carry):
            @pl.when(nv_ref[i] < MOE_BLOCK)
            def _():
                zero_copy(i).start()
            return carry

        def zwait(i, carry):
            @pl.when(nv_ref[i] < MOE_BLOCK)
            def _():
                zero_copy(i).wait()
            return carry

        lax.fori_loop(0, n_blocks, zstart, 0)
        lax.fori_loop(0, n_blocks, zwait, 0)

    def row_copy(t, k):
        src = pl.multiple_of((step * tm + t) * tiles, tiles)
        dst = pl.multiple_of(pos_ref[2 * t + k] * tiles, tiles)
        return pltpu.make_async_copy(x1t_hbm.at[pl.ds(src, tiles), :], xs_out.at[pl.ds(dst, tiles), :], sem)

    def start(t, carry):
        row_copy(t, 0).start()
        row_copy(t, 1).start()
        return carry

    def wait(t, carry):
        row_copy(t, 0).wait()
        row_copy(t, 1).wait()
        return carry

    lax.fori_loop(0, tm, start, 0, unroll=DMA_UNROLL)
    lax.fori_loop(0, tm, wait, 0, unroll=DMA_UNROLL)


def _scatter(pos_flat, nvalid, x1t, n_blocks, tiles):
    t = x1t.shape[0] // tiles
    tm = min(TOKEN_TILE, t)
    kern = functools.partial(_scatter_kernel, tm=tm, tiles=tiles, n_blocks=n_blocks)
    return pl.pallas_call(
        kern,
        grid=(t // tm,),
        in_specs=[
            pl.BlockSpec((2 * tm,), lambda i: (i,), memory_space=pltpu.SMEM),
            pl.BlockSpec(memory_space=pltpu.SMEM),
            pl.BlockSpec(memory_space=pl.ANY),
        ],
        out_specs=pl.BlockSpec(memory_space=pl.ANY),
        out_shape=jax.ShapeDtypeStruct((n_blocks * MOE_BLOCK * tiles, LANES), F32),
        scratch_shapes=[pltpu.VMEM((MOE_BLOCK * tiles, LANES), F32),
                        pltpu.SemaphoreType.DMA(()), pltpu.SemaphoreType.DMA(())],
        compiler_params=_cparams(("arbitrary",)),
        name="moe_scatter",
    )(pos_flat, nvalid, x1t)


def _expert_kernel(be_ref, nv_ref, x_ref, wg_ref, wu_ref, wd_ref, y_ref, *, tiles):
    del be_ref
    n_valid = nv_ref[pl.program_id(0)]

    @pl.when(n_valid == 0)
    def _():
        y_ref[...] = jnp.zeros(y_ref.shape, F32)

    @pl.when(n_valid > 0)
    def _():
        xb = _load_token_tiles(x_ref, MOE_BLOCK, tiles).astype(BF16)
        g = jnp.dot(xb, wg_ref[0], preferred_element_type=F32)
        u = jnp.dot(xb, wu_ref[0], preferred_element_type=F32)
        h = g * jax.nn.sigmoid(g) * u
        _store_token_tiles(y_ref, jnp.dot(h.astype(BF16), wd_ref[0], preferred_element_type=F32))


def _experts(block_expert, nvalid, xs, wg, wu, wd):
    d, de = wg.shape[1], wg.shape[2]
    tiles = d // LANES
    blk_rows = MOE_BLOCK * tiles
    nb = xs.shape[0] // blk_rows
    grid_spec = pltpu.PrefetchScalarGridSpec(
        num_scalar_prefetch=2,
        grid=(nb,),
        in_specs=[
            pl.BlockSpec((blk_rows, LANES), lambda i, be, nv: (i, 0)),
            pl.BlockSpec((1, d, de), lambda i, be, nv: (be[i], 0, 0)),
            pl.BlockSpec((1, d, de), lambda i, be, nv: (be[i], 0, 0)),
            pl.BlockSpec((1, de, d), lambda i, be, nv: (be[i], 0, 0)),
        ],
        out_specs=pl.BlockSpec((blk_rows, LANES), lambda i, be, nv: (i, 0)),
    )
    return pl.pallas_call(
        functools.partial(_expert_kernel, tiles=tiles),
        grid_spec=grid_spec,
        out_shape=jax.ShapeDtypeStruct(xs.shape, F32),
        compiler_params=_cparams(("arbitrary",)),
        name="moe_experts",
    )(block_expert, nvalid, xs, wg, wu, wd)


def _final_kernel(pos_ref, posn_ref, x1_ref, ri_ref, p_ref, ys_hbm, wpg_ref, wpp_ref, g2_ref, b2_ref,
                  pg_ref, out_ref, ybuf, sems, *, tm, tiles):
    step = pl.program_id(0)
    n_steps = pl.num_programs(0)
    slot = step % 2

    def row_copy(tab_ref, t, k, sl):
        src = pl.multiple_of(tab_ref[2 * t + k] * tiles, tiles)
        dst = pl.multiple_of(t * tiles, tiles)
        return pltpu.make_async_copy(ys_hbm.at[pl.ds(src, tiles), :], ybuf.at[sl, k, pl.ds(dst, tiles), :],
                                     sems.at[sl])

    def start_tile(tab_ref, sl):
        def body(t, carry):
            row_copy(tab_ref, t, 0, sl).start()
            row_copy(tab_ref, t, 1, sl).start()
            return carry

        lax.fori_loop(0, tm, body, 0, unroll=DMA_UNROLL)

    @pl.when(step == 0)
    def _():
        start_tile(pos_ref, 0)

    @pl.when(step + 1 < n_steps)
    def _():
        start_tile(posn_ref, 1 - slot)

    def wait(t, carry):
        row_copy(pos_ref, t, 0, slot).wait()
        row_copy(pos_ref, t, 1, slot).wait()
        return carry

    lax.fori_loop(0, tm, wait, 0, unroll=DMA_UNROLL)

    ri = ri_ref[...]
    y0 = _load_token_tiles(ybuf.at[slot, 0], tm, tiles)
    y1 = _load_token_tiles(ybuf.at[slot, 1], tm, tiles)
    ffn = ri[:, 2:3] * y0 + ri[:, 3:4] * y1
    x2 = _layer_norm(DN_ALPHA * x1_ref[...] + ffn, g2_ref[...], b2_ref[...])
    gate = jax.nn.sigmoid(jnp.dot(x2.astype(BF16), wpg_ref[...], preferred_element_type=F32))
    proj = jnp.dot(p_ref[...].astype(BF16), wpp_ref[...], preferred_element_type=F32)
    out_ref[...] = x2 + _rms_norm(gate * proj, pg_ref[...])


def _final(pos_flat, x1, ri, p, ys, wpg, wpp, g2, b2, pg):
    t, d = x1.shape
    tm = min(TOKEN_TILE, t)
    n_steps = t // tm
    tiles = d // LANES
    row = lambda w: pl.BlockSpec((tm, w), lambda i: (i, 0))
    consts = [wpg, wpp, g2, b2, pg]
    kern = functools.partial(_final_kernel, tm=tm, tiles=tiles)
    return pl.pallas_call(
        kern,
        grid=(n_steps,),
        in_specs=[pl.BlockSpec((2 * tm,), lambda i: (i,), memory_space=pltpu.SMEM),
                  pl.BlockSpec((2 * tm,), lambda i: (jnp.minimum(i + 1, n_steps - 1),), memory_space=pltpu.SMEM),
                  row(d), row(LANES), row(p.shape[1]), pl.BlockSpec(memory_space=pl.ANY)]
                 + [_full(a.shape) for a in consts],
        out_specs=row(d),
        out_shape=jax.ShapeDtypeStruct((t, d), F32),
        scratch_shapes=[pltpu.VMEM((2, 2, tm * tiles, LANES), F32), pltpu.SemaphoreType.DMA((2,))],
        compiler_params=_cparams(("arbitrary",)),
        name="combine_ln2_ple",
    )(pos_flat, pos_flat, x1, ri, p, ys, *consts)


def _lane_patterns(pair_layout):
    lane = np.arange(LANES)
    if pair_layout:
        valid = (lane < 2 * QK_ROPE).astype(np.float32)
    else:
        valid = np.ones(LANES, np.float32)
    sign = np.where((lane % QK_ROPE) < HALF_ROPE, -1.0, 1.0).astype(np.float32) * valid
    return lane % HALF_ROPE, sign, valid


def _prep_weights(w_in, w_uq, w_uk, w_uv, w_group, w_router):
    d = w_in.shape[0]
    ql = w_uq.shape[0]
    kvl, n_heads, _ = w_uk.shape
    o_kv, o_kr = ql, ql + kvl
    o_conv = o_kr + QK_ROPE
    c_conv = (w_in.shape[1] - o_conv - 2 * d) // 2
    o_ga = o_conv + 2 * c_conv
    o_gb = o_ga + d
    kr = w_in[:, o_kr:o_conv]
    kr_sw = jnp.concatenate([kr[:, HALF_ROPE:], kr[:, :HALF_ROPE]], axis=1)
    zpad = jnp.zeros((d, LANES - 2 * QK_ROPE), w_in.dtype)
    win_r = jnp.concatenate([
        w_in[:, :o_kv], w_in[:, o_kv:o_kr], w_in[:, o_conv:o_ga], w_in[:, o_ga:o_gb], w_in[:, o_gb:],
        kr, kr, zpad, kr_sw, kr_sw, zpad], axis=1).astype(BF16)
    assert win_r.shape[1] == C_END and c_conv == C_CB - C_CA and kvl == C_CA - C_KV and ql == C_KV

    hd = QK_NOPE + QK_ROPE
    wq = w_uq.reshape(ql, n_heads, hd)
    nope = wq[:, :, :QK_NOPE].reshape(ql, n_heads * QK_NOPE)
    x1 = wq[:, :, QK_NOPE:QK_NOPE + HALF_ROPE]
    x2 = wq[:, :, QK_NOPE + HALF_ROPE:]
    rope_a = jnp.concatenate([x1, x2], axis=2)
    rope_b = jnp.concatenate([x2, x1], axis=2)
    wuq_s = jnp.concatenate([nope, rope_a.reshape(ql, -1), rope_b.reshape(ql, -1)], axis=1).astype(BF16)

    def pair_layout(r):
        r = r.reshape(ql, n_heads // 2, 2 * QK_ROPE)
        z = jnp.zeros((ql, n_heads // 2, LANES - 2 * QK_ROPE), r.dtype)
        return jnp.concatenate([r, z], axis=2).reshape(ql, -1)

    wuq_p = jnp.concatenate([nope, pair_layout(rope_a), pair_layout(rope_b)], axis=1).astype(BF16)

    wuk_flat = w_uk.reshape(kvl, n_heads * QK_NOPE)
    wuv_flat = w_uv.reshape(kvl, n_heads * V_HEAD)
    wukv = jnp.concatenate([wuk_flat, wuv_flat], axis=1).astype(BF16)
    eye = jnp.eye(n_heads, dtype=w_uk.dtype)
    wk_bd = jnp.einsum('lhn,hg->hngl', w_uk, eye).reshape(n_heads * QK_NOPE, n_heads * kvl).astype(BF16)
    wv_bd = jnp.einsum('lhv,hg->hlgv', w_uv, eye).reshape(n_heads * kvl, n_heads * V_HEAD).astype(BF16)

    n_exp = w_router.shape[1]
    wgr = jnp.concatenate([w_group, w_router,
                           jnp.zeros((d, LANES - N_GROUPS - n_exp), w_group.dtype)], axis=1)
    wgh = wgr.astype(BF16)
    wgl = (wgr - wgh.astype(F32)).astype(BF16)
    return win_r, wuq_p, wuq_s, wukv, wk_bd, wv_bd, wgh, wgl


def _moe(x1t, ri, wg, wu, wd, n_exp):
    t = ri.shape[0]
    tiles = x1t.shape[0] // t
    n_assign = 2 * t
    n_blocks = (n_assign + n_exp * (MOE_BLOCK - 1) + MOE_BLOCK - 1) // MOE_BLOCK
    pos, be = _route(ri, n_blocks=n_blocks, n_exp=n_exp)
    pos_flat = pos[:, :2].reshape(-1)
    block_expert = be[:n_blocks, 0]
    nvalid = be[:n_blocks, 1]
    xs = _scatter(pos_flat, nvalid, x1t, n_blocks, tiles)
    ys = _experts(block_expert, nvalid, xs, wg, wu, wd)
    return pos_flat, ys


def kernel(x_prompt, x_sample, cache_ckv, cache_krope, state_conv, page_table, p_prompt, p_sample, w_in, q_norm_g, w_uq, kv_norm_g, w_uk, w_uv, w_o_attn, w_dw, b_dw, conv_ln_g, conv_ln_b, w_pw2, w_out, ln1_g, ln1_b, w_group, w_router, w_gate, w_up, w_down, ln2_g, ln2_b, w_ple_gate, w_ple_proj, ple_norm_g):
    assert w_in.shape[0] == DEPTH
    b, s, d = x_prompt.shape
    bd, sd, _ = x_sample.shape
    assert sd == 1
    n_pages = page_table.shape[1]
    page_rows = cache_ckv.shape[2]
    past = n_pages * page_rows
    n_heads = w_uk.shape[2]
    kvl = w_uk.shape[1]
    width = w_dw.shape[1]
    n_exp = w_router.shape[2]
    c_conv = w_dw.shape[2]

    win_r, wuq_p, wuq_s, wukv, wk_bd, wv_bd, wgh, wgl = _prep_weights(
        w_in[0], w_uq[0], w_uk[0], w_uv[0], w_group[0], w_router[0])
    qg, kvg = q_norm_g, kv_norm_g
    wo, wpw, wout = w_o_attn[0].astype(BF16), w_pw2[0].astype(BF16), w_out[0].astype(BF16)
    wgate, wup, wdown = w_gate[0].astype(BF16), w_up[0].astype(BF16), w_down[0].astype(BF16)
    wpg, wpp = w_ple_gate[0].astype(BF16), w_ple_proj[0].astype(BF16)
    w_dw_pad = jnp.concatenate([w_dw[0], jnp.zeros((CONV_HALO - width, c_conv), F32)], axis=0)

    inv16 = 1.0 / (ROPE_THETA ** (jnp.arange(HALF_ROPE, dtype=F32) / HALF_ROPE))

    def tables(pair_layout, n_rows, offset, step):
        idx, sign, valid = _lane_patterns(pair_layout)
        inv_l = (inv16[idx] * valid)[None, :]
        return _rope_tables(n_rows, offset, step, inv_l, jnp.asarray(sign)[None, :], jnp.asarray(valid)[None, :])

    def trunk_tail(x2d, o, y, sa, sb, p2d):
        x1, x1t, ri = _mix(x2d, o, y, sa, sb, wo, wpw, wout, conv_ln_g, conv_ln_b, ln1_g, ln1_b, wgh, wgl)
        pos_flat, ys = _moe(x1t, ri, wgate, wup, wdown, n_exp)
        return _final(pos_flat, x1, ri, p2d, ys, wpg, wpp, ln2_g, ln2_b, ple_norm_g)

    xs_ = x_sample.reshape(bd, d)
    cos_s, sin_s = tables(False, SUBLANES, past, 0)
    qn_s, qr_s, ckv_s, kr_s, u_s, sa_s, sb_s = _inproj(
        xs_, win_r, qg, kvg, wuq_s, None, cos_s, sin_s, n_rep=n_heads * QK_ROPE // LANES, seq=1, make_kv=False)
    q_lat = _dense(qn_s, wk_bd, F32).reshape(bd, n_heads, kvl)
    o_lat = _sample_attention(page_table, q_lat, qr_s.reshape(bd, n_heads, QK_ROPE),
                              ckv_s.reshape(bd, 1, kvl), kr_s.reshape(bd, 1, QK_ROPE),
                              cache_ckv[0], jnp.swapaxes(cache_krope[0], 1, 2))
    o_s = _dense(o_lat.reshape(bd, n_heads * kvl), wv_bd, BF16)
    ext_s = jnp.concatenate([state_conv[0], u_s[:, None, :]], axis=1)
    y_s = _conv_sample(jnp.transpose(ext_s, (1, 0, 2)), w_dw_pad, b_dw, width=width)
    out_s = trunk_tail(xs_, o_s, y_s, sa_s, sb_s, p_sample[0].reshape(bd, -1))
    new_conv_s = ext_s[:, 1:, :]

    xp = x_prompt.reshape(b * s, d)
    cos_p, sin_p = tables(True, s, 0, 1)
    qn, qr, kn, v, krd, ckv_p, kr_p, u_p, sa_p, sb_p = _inproj(
        xp, win_r, qg, kvg, wuq_p, wukv, cos_p, sin_p, n_rep=n_heads // 2, seq=s, make_kv=True)
    o_p = _prompt_attention(qn, qr, kn, krd, v, batch=b, seq=s)
    y_p = _conv_prompt(u_p, w_dw_pad, b_dw, batch=b, seq=s, width=width)
    out_p = trunk_tail(xp, o_p, y_p, sa_p, sb_p, p_prompt[0].reshape(b * s, -1))
    u_p3 = u_p.reshape(b, s, c_conv)
    new_conv_p = u_p3[:, s - (width - 1):, :]

    return (out_p.reshape(b, s, d), out_s.reshape(bd, 1, d),
            ckv_p.reshape(1, b, s, kvl), kr_p.reshape(1, b, s, QK_ROPE), new_conv_p[None],
            ckv_s.reshape(1, bd, 1, kvl), kr_s.reshape(1, bd, 1, QK_ROPE), new_conv_s[None])
```

```python
import functools

import numpy as np
import jax
import jax.numpy as jnp
from jax import lax
from jax.experimental import pallas as pl
from jax.experimental.pallas import tpu as pltpu

F32 = jnp.float32
BF16 = jnp.bfloat16

QK_NOPE = 64
QK_ROPE = 32
HALF_ROPE = QK_ROPE // 2
V_HEAD = 64
ROPE_THETA = 10000.0
ATTN_SCALE = (QK_NOPE + QK_ROPE) ** -0.5
LOG2E = 1.4426950408889634
N_GROUPS = 4
EXPERTS_PER_GROUP = 8
DEPTH = 1
DN_ALPHA = (2 * DEPTH) ** 0.25
LN_EPS = 1e-5
RMS_EPS = 1e-6
NEG_INF = -1e30

LANES = 128
SUBLANES = 8
VMEM_LIMIT = 56 * 1024 * 1024

TOKEN_TILE = 512
ATTN_TILE = 512
CONV_CHUNK = 64
CONV_HALO = 32
MOE_BLOCK = 256
PAGES_PER_CHUNK = 16
SEQS_PER_STEP = 4
DMA_UNROLL = 8


def _cparams(sem, vmem=VMEM_LIMIT):
    return pltpu.CompilerParams(dimension_semantics=sem, vmem_limit_bytes=vmem)


def _full(shape):
    n = len(shape)
    return pl.BlockSpec(shape, lambda *_: (0,) * n)


def _layer_norm(x, g, b):
    mu = jnp.mean(x, axis=-1, keepdims=True)
    xc = x - mu
    var = jnp.mean(xc * xc, axis=-1, keepdims=True)
    return xc * lax.rsqrt(var + LN_EPS) * g + b


def _rms_norm(x, g):
    return x * lax.rsqrt(jnp.mean(x * x, axis=-1, keepdims=True) + RMS_EPS) * g


def _rope_table_kernel(inv_ref, sign_ref, valid_ref, cos_ref, sin_ref, *, offset, step):
    rows = cos_ref.shape[0]
    r = lax.broadcasted_iota(jnp.int32, (rows, LANES), 0) + pl.program_id(0) * rows
    pos = (offset + step * r).astype(F32)
    ang = pos * inv_ref[...]
    cos_ref[...] = jnp.cos(ang) * valid_ref[...]
    sin_ref[...] = jnp.sin(ang) * sign_ref[...]


def _rope_tables(n_rows, offset, step, inv_l, sign_l, valid_l):
    tr = min(n_rows, 512)
    kern = functools.partial(_rope_table_kernel, offset=offset, step=step)
    return pl.pallas_call(
        kern,
        grid=(n_rows // tr,),
        in_specs=[_full((1, LANES))] * 3,
        out_specs=[pl.BlockSpec((tr, LANES), lambda i: (i, 0))] * 2,
        out_shape=[jax.ShapeDtypeStruct((n_rows, LANES), F32)] * 2,
        compiler_params=_cparams(("arbitrary",)),
        name="rope_tables",
    )(inv_l, sign_l, valid_l)


C_Q, C_KV, C_CA, C_CB, C_GA, C_GB, C_KA, C_KB, C_END = 0, 384, 640, 1152, 1664, 2688, 3712, 3840, 3968


def _inproj_kernel(x_ref, win_ref, qg_ref, kvg_ref, wuq_ref, *rest, n_rep, make_kv, bcast, q_scale):
    if make_kv:
        (wukv_ref, cos_ref, sin_ref, qn_o, qr_o, kn_o, v_o, krd_o, ckv_o, kr_o, u_o, sa_o, sb_o) = rest
    else:
        (cos_ref, sin_ref, qn_o, qr_o, ckv_o, kr_o, u_o, sa_o, sb_o) = rest
    xb = x_ref[...].astype(BF16)

    def proj(a, b):
        return jnp.dot(xb, win_ref[:, a:b], preferred_element_type=F32)

    if bcast:
        c1, s1 = cos_ref[0:1, :], sin_ref[0:1, :]
    else:
        c1, s1 = cos_ref[...], sin_ref[...]
    cn = jnp.concatenate([c1] * n_rep, axis=1)
    sn = jnp.concatenate([s1] * n_rep, axis=1)
    d_nope = qn_o.shape[1]
    nr = LANES * n_rep

    cqn = _rms_norm(proj(C_Q, C_KV), qg_ref[...])
    qall = jnp.dot(cqn.astype(BF16), wuq_ref[...], preferred_element_type=F32)
    qn_o[...] = (qall[:, :d_nope] * q_scale).astype(qn_o.dtype)
    qa = qall[:, d_nope:d_nope + nr]
    qb = qall[:, d_nope + nr:d_nope + 2 * nr]
    qr_o[...] = ((qa * cn + qb * sn) * q_scale).astype(qr_o.dtype)

    ckvn = _rms_norm(proj(C_KV, C_CA), kvg_ref[...])
    ckv_o[...] = ckvn
    krot = proj(C_KA, C_KB) * c1 + proj(C_KB, C_END) * s1
    kr_o[...] = krot[:, :QK_ROPE]
    if make_kv:
        kv = jnp.dot(ckvn.astype(BF16), wukv_ref[...], preferred_element_type=F32)
        half = kv.shape[1] // 2
        kn_o[...] = kv[:, :half].astype(BF16)
        v_o[...] = kv[:, half:].astype(BF16)
        krd_o[...] = krot.astype(BF16)

    u_o[...] = proj(C_CA, C_CB) * jax.nn.sigmoid(proj(C_CB, C_GA))
    sa_o[...] = jax.nn.sigmoid(proj(C_GA, C_GB)).astype(BF16)
    sb_o[...] = jax.nn.sigmoid(proj(C_GB, C_KA)).astype(BF16)


def _inproj(x, win_r, qg, kvg, wuq_r, wukv, cos_t, sin_t, *, n_rep, seq, make_kv):
    t, d = x.shape
    tm = min(TOKEN_TILE, t)
    d_nope = wuq_r.shape[1] - 2 * LANES * n_rep
    nr = LANES * n_rep
    c_conv = C_CB - C_CA
    kvl = C_CA - C_KV
    bcast = seq < tm
    if bcast:
        tab_spec = _full(cos_t.shape)
    else:
        nst = seq // tm
        tab_spec = pl.BlockSpec((tm, LANES), lambda i: (i % nst, 0))
    row = lambda w: pl.BlockSpec((tm, w), lambda i: (i, 0))
    in_specs = [row(d), _full(win_r.shape), _full(qg.shape), _full(kvg.shape), _full(wuq_r.shape)]
    args = [x, win_r, qg, kvg, wuq_r]
    if make_kv:
        in_specs.append(_full(wukv.shape))
        args.append(wukv)
    in_specs += [tab_spec, tab_spec]
    args += [cos_t, sin_t]
    q_dt = BF16 if make_kv else F32
    outs = [(d_nope, BF16), (nr, q_dt)]
    if make_kv:
        outs += [(d_nope, BF16), (d_nope, BF16), (LANES, BF16)]
    outs += [(kvl, F32), (QK_ROPE, F32), (c_conv, F32), (d, BF16), (d, BF16)]
    q_scale = ATTN_SCALE * LOG2E if make_kv else ATTN_SCALE
    kern = functools.partial(_inproj_kernel, n_rep=n_rep, make_kv=make_kv, bcast=bcast, q_scale=q_scale)
    return pl.pallas_call(
        kern,
        grid=(t // tm,),
        in_specs=in_specs,
        out_specs=[row(w) for w, _ in outs],
        out_shape=[jax.ShapeDtypeStruct((t, w), dt) for w, dt in outs],
        compiler_params=_cparams(("arbitrary",)),
        name="inproj_kv" if make_kv else "inproj_q",
    )(*args)


def _pattn_kernel(qn_ref, qr_ref, kn_ref, krd_ref, v_ref, o_ref, *, blk, nq):
    lane = lax.broadcasted_iota(jnp.int32, (blk, LANES), 1)
    row = lax.broadcasted_iota(jnp.int32, (blk, blk), 0)
    col = lax.broadcasted_iota(jnp.int32, (blk, blk), 1)
    n_rep = blk // LANES
    dn_t = (((1,), (1,)), ((), ()))
    for i in range(nq):
        qn = qn_ref[i * blk:(i + 1) * blk, :]
        qr = qr_ref[i * blk:(i + 1) * blk, :]
        zero = jnp.zeros_like(qn)
        q_heads = (
            jnp.concatenate([jnp.where(lane < QK_NOPE, qn, zero), jnp.where(lane < QK_ROPE, qr, zero)], axis=1),
            jnp.concatenate([jnp.where(lane >= QK_NOPE, qn, zero),
                             jnp.where((lane >= QK_ROPE) & (lane < 2 * QK_ROPE), qr, zero)], axis=1),
        )
        m = [jnp.full((blk, LANES), NEG_INF, F32)] * 2
        l = [jnp.zeros((blk, LANES), F32)] * 2
        acc = [jnp.zeros((blk, LANES), F32)] * 2
        for j in range(i + 1):
            k = jnp.concatenate([kn_ref[j * blk:(j + 1) * blk, :], krd_ref[j * blk:(j + 1) * blk, :]], axis=1)
            v = v_ref[j * blk:(j + 1) * blk, :]
            for h in range(2):
                s = lax.dot_general(q_heads[h], k, dn_t, preferred_element_type=F32)
                if j == i:
                    s = jnp.where(col <= row, s, NEG_INF)
                m_new = jnp.maximum(m[h], jnp.max(s, axis=1, keepdims=True))
                alpha = jnp.exp2(m[h] - m_new)
                p = jnp.exp2(s - jnp.concatenate([m_new] * n_rep, axis=1))
                l[h] = alpha * l[h] + jnp.sum(p, axis=1, keepdims=True)
                acc[h] = alpha * acc[h] + jnp.dot(p.astype(BF16), v, preferred_element_type=F32)
                m[h] = m_new
        o = jnp.where(lane < V_HEAD, acc[0] / l[0], acc[1] / l[1])
        o_ref[i * blk:(i + 1) * blk, :] = o.astype(o_ref.dtype)


def _prompt_attention(qn, qr, kn, krd, v, *, batch, seq):
    t, d = qn.shape
    blk = min(ATTN_TILE, seq)
    nq = seq // blk
    n_pairs = d // LANES
    kspec = pl.BlockSpec((seq, LANES), lambda b, j: (b, j))
    kern = functools.partial(_pattn_kernel, blk=blk, nq=nq)
    return pl.pallas_call(
        kern,
        grid=(batch, n_pairs),
        in_specs=[kspec, kspec, kspec, pl.BlockSpec((seq, LANES), lambda b, j: (b, 0)), kspec],
        out_specs=kspec,
        out_shape=jax.ShapeDtypeStruct((t, d), BF16),
        compiler_params=_cparams(("arbitrary",) * 2),
        name="prompt_attention",
    )(qn, qr, kn, krd, v)


def _dense_kernel(a_ref, w_ref, o_ref):
    o_ref[...] = jnp.dot(a_ref[...].astype(BF16), w_ref[...], preferred_element_type=F32).astype(o_ref.dtype)


def _dense(a, w, out_dtype):
    m, n = a.shape[0], w.shape[1]
    return pl.pallas_call(
        _dense_kernel,
        grid=(1,),
        in_specs=[_full(a.shape), _full(w.shape)],
        out_specs=_full((m, n)),
        out_shape=jax.ShapeDtypeStruct((m, n), out_dtype),
        compiler_params=_cparams(("arbitrary",)),
        name="dense",
    )(a, w)


def _sattn_kernel(pt_ref, ptn_ref, ql_ref, qr_ref, cnew_ref, knew_ref, ckv_hbm, krt_hbm, o_ref,
                  cbuf, kbuf, sems, *, n_chunks, ppc, nb):
    step = pl.program_id(0)
    n_steps = pl.num_programs(0)
    kvl = cbuf.shape[4]
    kc = ppc * cbuf.shape[3]
    n_heads = ql_ref.shape[1]
    dn_t = (((1,), (1,)), ((), ()))

    def chunk_copies(tab_ref, chunk, slot):
        out = []
        for b in range(nb):
            for pg in range(ppc):
                page = tab_ref[b, 0, chunk * ppc + pg]
                out.append(pltpu.make_async_copy(ckv_hbm.at[page], cbuf.at[slot, b, pg], sems.at[0, slot]))
                out.append(pltpu.make_async_copy(krt_hbm.at[page], kbuf.at[slot, b, pg], sems.at[1, slot]))
        return out

    def start_chunk(tab_ref, chunk, slot):
        for cp in chunk_copies(tab_ref, chunk, slot):
            cp.start()

    @pl.when(step == 0)
    def _():
        start_chunk(pt_ref, 0, 0)

    first = step * n_chunks
    qls = [ql_ref[b].astype(BF16) for b in range(nb)]
    qrs = [qr_ref[b].astype(BF16) for b in range(nb)]

    def body(c, carry):
        slot = (first + c) % 2

        @pl.when(c + 1 < n_chunks)
        def _():
            start_chunk(pt_ref, c + 1, 1 - slot)

        @pl.when((c + 1 == n_chunks) & (step + 1 < n_steps))
        def _():
            start_chunk(ptn_ref, 0, 1 - slot)

        for cp in chunk_copies(pt_ref, c, slot):
            cp.wait()
        new = []
        for b in range(nb):
            m, l, acc = carry[3 * b:3 * b + 3]
            ck = cbuf[slot, b].reshape(kc, kvl).astype(BF16)
            krt = jnp.concatenate([kbuf[slot, b, pg] for pg in range(ppc)], axis=1).astype(BF16)
            s = lax.dot_general(qls[b], ck, dn_t, preferred_element_type=F32)
            s = s + jnp.dot(qrs[b], krt, preferred_element_type=F32)
            m_new = jnp.maximum(m, jnp.max(s, axis=1, keepdims=True))
            alpha = jnp.exp(m - m_new)
            p = jnp.exp(s - m_new)
            l = alpha * l + jnp.sum(p, axis=1, keepdims=True)
            acc = alpha * acc + jnp.dot(p.astype(BF16), ck, preferred_element_type=F32)
            new += [m_new, l, acc]
        return tuple(new)

    init = (jnp.full((n_heads, 1), NEG_INF, F32), jnp.zeros((n_heads, 1), F32),
            jnp.zeros((n_heads, kvl), F32)) * nb
    carry = lax.fori_loop(0, n_chunks, body, init)

    for b in range(nb):
        m, l, acc = carry[3 * b:3 * b + 3]
        cnew = cnew_ref[b]
        knew = knew_ref[b]
        s_new = (jnp.sum(ql_ref[b] * cnew, axis=1, keepdims=True)
                 + jnp.sum(qr_ref[b] * knew, axis=1, keepdims=True))
        m_new = jnp.maximum(m, s_new)
        alpha = jnp.exp(m - m_new)
        p_new = jnp.exp(s_new - m_new)
        l = alpha * l + p_new
        acc = alpha * acc + p_new * cnew
        o_ref[b] = acc / l


def _sample_attention(page_table, q_lat, q_rope, c_new, k_new, cache_ckv, cache_krope_t):
    bd, n_heads, kvl = q_lat.shape
    n_pages = page_table.shape[1]
    page_rows = cache_ckv.shape[1]
    ppc = min(PAGES_PER_CHUNK, n_pages)
    n_chunks = n_pages // ppc
    nb = min(SEQS_PER_STEP, bd)
    n_steps = bd // nb
    pt3 = page_table.reshape(bd, 1, n_pages)
    per_b = lambda s: pl.BlockSpec((nb,) + s, lambda i: (i, 0, 0))
    kern = functools.partial(_sattn_kernel, n_chunks=n_chunks, ppc=ppc, nb=nb)
    return pl.pallas_call(
        kern,
        grid=(n_steps,),
        in_specs=[
            pl.BlockSpec((nb, 1, n_pages), lambda i: (i, 0, 0), memory_space=pltpu.SMEM),
            pl.BlockSpec((nb, 1, n_pages), lambda i: (jnp.minimum(i + 1, n_steps - 1), 0, 0),
                         memory_space=pltpu.SMEM),
            per_b((n_heads, kvl)), per_b((n_heads, QK_ROPE)), per_b((1, kvl)), per_b((1, QK_ROPE)),
            pl.BlockSpec(memory_space=pl.ANY), pl.BlockSpec(memory_space=pl.ANY),
        ],
        out_specs=per_b((n_heads, kvl)),
        out_shape=jax.ShapeDtypeStruct((bd, n_heads, kvl), F32),
        scratch_shapes=[
            pltpu.VMEM((2, nb, ppc, page_rows, kvl), F32),
            pltpu.VMEM((2, nb, ppc, QK_ROPE, page_rows), F32),
            pltpu.SemaphoreType.DMA((2, 2)),
        ],
        compiler_params=_cparams(("arbitrary",)),
        name="sample_attention",
    )(pt3, pt3, q_lat, q_rope, c_new, k_new, cache_ckv, cache_krope_t)


def _conv_prompt_kernel(u_ref, halo_ref, w_ref, b_ref, y_ref, ext_s, *, ts, width):
    i = pl.program_id(1)
    left = jnp.where(i > 0, halo_ref[...], jnp.zeros_like(halo_ref))
    ext_s[0:CONV_HALO, :] = left
    ext_s[CONV_HALO:CONV_HALO + ts, :] = u_ref[...]
    base = CONV_HALO - (width - 1)
    for c in range(ts // CONV_CHUNK):
        r0 = c * CONV_CHUNK
        acc = jnp.broadcast_to(b_ref[...], (CONV_CHUNK, u_ref.shape[1]))
        for k in range(width):
            acc = acc + w_ref[k:k + 1, :] * ext_s[r0 + base + k:r0 + base + k + CONV_CHUNK, :]
        y_ref[r0:r0 + CONV_CHUNK, :] = acc


def _conv_prompt(u, w_pad, b, *, batch, seq, width):
    t, c = u.shape
    ts = min(TOKEN_TILE, seq)
    ns = seq // ts
    hb = ts // CONV_HALO
    kern = functools.partial(_conv_prompt_kernel, ts=ts, width=width)
    return pl.pallas_call(
        kern,
        grid=(batch, ns),
        in_specs=[
            pl.BlockSpec((ts, c), lambda b_, i: (b_ * ns + i, 0)),
            pl.BlockSpec((CONV_HALO, c), lambda b_, i: (jnp.maximum((b_ * ns + i) * hb - 1, 0), 0)),
            _full(w_pad.shape), _full(b.shape),
        ],
        out_specs=pl.BlockSpec((ts, c), lambda b_, i: (b_ * ns + i, 0)),
        out_shape=jax.ShapeDtypeStruct((t, c), F32),
        scratch_shapes=[pltpu.VMEM((CONV_HALO + ts, c), F32)],
        compiler_params=_cparams(("arbitrary", "arbitrary")),
        name="conv_prompt",
    )(u, u, w_pad, b)


def _conv_sample_kernel(ext_ref, w_ref, b_ref, y_ref, *, width):
    acc = jnp.broadcast_to(b_ref[...], y_ref.shape)
    for k in range(width):
        acc = acc + w_ref[k:k + 1, :] * ext_ref[k]
    y_ref[...] = acc


def _conv_sample(ext_t, w_pad, b, *, width):
    _, bd, c = ext_t.shape
    kern = functools.partial(_conv_sample_kernel, width=width)
    return pl.pallas_call(
        kern,
        grid=(1,),
        in_specs=[_full(ext_t.shape), _full(w_pad.shape), _full(b.shape)],
        out_specs=_full((bd, c)),
        out_shape=jax.ShapeDtypeStruct((bd, c), F32),
        compiler_params=_cparams(("arbitrary",)),
        name="conv_sample",
    )(ext_t, w_pad, b)


def _store_token_tiles(ref, x):
    rows, d = x.shape
    tiles = d // LANES
    for c in range(tiles):
        ref[pl.ds(c, rows, stride=tiles), :] = x[:, c * LANES:(c + 1) * LANES]


def _load_token_tiles(ref, rows, tiles):
    return jnp.concatenate([ref[pl.ds(c, rows, stride=tiles), :] for c in range(tiles)], axis=1)


def _mix_kernel(x_ref, o_ref, y_ref, sa_ref, sb_ref, wo_ref, wpw_ref, wout_ref, cg_ref, cb_ref,
                g1_ref, b1_ref, wgh_ref, wgl_ref, x1_ref, x1t_ref, ri_ref):
    branch_b = jnp.dot(o_ref[...], wo_ref[...], preferred_element_type=F32)
    z = _layer_norm(y_ref[...], cg_ref[...], cb_ref[...])
    z = z * jax.nn.sigmoid(z)
    branch_a = jnp.dot(z.astype(BF16), wpw_ref[...], preferred_element_type=F32)
    mixin = sa_ref[...].astype(F32) * branch_a + sb_ref[...].astype(F32) * branch_b
    mix = jnp.dot(mixin.astype(BF16), wout_ref[...], preferred_element_type=F32)
    x1 = _layer_norm(DN_ALPHA * x_ref[...] + mix, g1_ref[...], b1_ref[...])
    x1_ref[...] = x1
    _store_token_tiles(x1t_ref, x1)

    x_hi = x1.astype(BF16)
    x_lo = (x1 - x_hi.astype(F32)).astype(BF16)
    lg = (jnp.dot(x_hi, wgh_ref[...], preferred_element_type=F32)
          + jnp.dot(x_lo, wgh_ref[...], preferred_element_type=F32)
          + jnp.dot(x_hi, wgl_ref[...], preferred_element_type=F32))
    n_exp = N_GROUPS * EXPERTS_PER_GROUP
    lane = lax.broadcasted_iota(jnp.int32, lg.shape, 1)
    lane_f = lane.astype(F32)
    big = float(LANES)
    gmask = lane < N_GROUPS
    lgm = jnp.where(gmask, lg, NEG_INF)
    gmax = jnp.max(lgm, axis=1, keepdims=True)
    gidx = jnp.min(jnp.where(lgm == gmax, lane_f, big), axis=1, keepdims=True)
    pg_sel = 1.0 / jnp.sum(jnp.where(gmask, jnp.exp(lgm - gmax), 0.0), axis=1, keepdims=True)
    egroup = jnp.floor((lane_f - N_GROUPS) * (1.0 / EXPERTS_PER_GROUP))
    emask = (lane >= N_GROUPS) & (lane < N_GROUPS + n_exp) & (egroup == gidx)
    le = jnp.where(emask, lg, NEG_INF)
    v1 = jnp.max(le, axis=1, keepdims=True)
    i1 = jnp.min(jnp.where(le == v1, lane_f, big), axis=1, keepdims=True)
    le2 = jnp.where(lane_f == i1, NEG_INF, le)
    v2 = jnp.max(le2, axis=1, keepdims=True)
    i2 = jnp.min(jnp.where(le2 == v2, lane_f, big), axis=1, keepdims=True)
    e = jnp.exp(v2 - v1)
    gate1 = pg_sel / (1.0 + e)
    gate2 = pg_sel * e / (1.0 + e)
    ri = jnp.where(lane == 0, i1 - N_GROUPS,
                   jnp.where(lane == 1, i2 - N_GROUPS,
                             jnp.where(lane == 2, gate1, jnp.where(lane == 3, gate2, 0.0))))
    ri_ref[...] = ri


def _mix(x, o, y, sa, sb, wo, wpw, wout, cg, cb, g1, b1, wgh, wgl):
    t, d = x.shape
    tm = min(TOKEN_TILE, t)
    row = lambda w: pl.BlockSpec((tm, w), lambda i: (i, 0))
    consts = [wo, wpw, wout, cg, cb, g1, b1, wgh, wgl]
    tiles = d // LANES
    return pl.pallas_call(
        _mix_kernel,
        grid=(t // tm,),
        in_specs=[row(d), row(d), row(y.shape[1]), row(d), row(d)] + [_full(a.shape) for a in consts],
        out_specs=[row(d), pl.BlockSpec((tm * tiles, LANES), lambda i: (i, 0)), row(LANES)],
        out_shape=[jax.ShapeDtypeStruct((t, d), F32), jax.ShapeDtypeStruct((t * tiles, LANES), F32),
                   jax.ShapeDtypeStruct((t, LANES), F32)],
        compiler_params=_cparams(("arbitrary",)),
        name="mix_ln1_router",
    )(x, o, y, sa, sb, *consts)


def _route_kernel(ri_ref, pos_ref, be_ref, cnt_s, base_s, *, tm, bm, n_exp):
    ph = pl.program_id(0)
    i = pl.program_id(1)
    lane_f = lax.broadcasted_iota(jnp.int32, (tm, LANES), 1).astype(F32)
    ri = ri_ref[...]
    oh0 = lane_f == ri[:, 0:1]
    oh1 = lane_f == ri[:, 1:2]
    c = jnp.where(oh0 | oh1, 1.0, 0.0)
    csum = jnp.sum(c, axis=0, keepdims=True)

    @pl.when((ph == 0) & (i == 0))
    def _():
        cnt_s[...] = jnp.zeros(cnt_s.shape, F32)

    @pl.when(ph == 0)
    def _():
        cnt_s[0:1, :] = cnt_s[0:1, :] + csum

    @pl.when((ph == 1) & (i == 0))
    def _():
        cnt = cnt_s[...]
        pc = jnp.floor((cnt + (bm - 1)) * (1.0 / bm)) * bm
        r = lax.broadcasted_iota(jnp.int32, (LANES, LANES), 0)
        cc = lax.broadcasted_iota(jnp.int32, (LANES, LANES), 1)
        upper = jnp.where(r <= cc, 1.0, 0.0)
        pend = jnp.dot(pc, upper, precision=lax.Precision.HIGHEST, preferred_element_type=F32)
        base_s[...] = pend - pc
        nbp = be_ref.shape[0]
        bstart = (lax.broadcasted_iota(jnp.int32, (nbp, LANES), 0) * bm).astype(F32)
        lane_b = lax.broadcasted_iota(jnp.int32, (nbp, LANES), 1)
        hit = jnp.where((lane_b < n_exp) & (pend[0:1, :] <= bstart), 1.0, 0.0)
        be = jnp.minimum(jnp.sum(hit, axis=1, keepdims=True), float(n_exp - 1))
        pstart = pend[0:1, :] - pc[0:1, :]
        span = jnp.minimum(pstart + cnt[0:1, :], bstart + bm) - jnp.maximum(pstart, bstart)
        nvalid = jnp.sum(jnp.where(lane_b < n_exp, jnp.maximum(span, 0.0), 0.0), axis=1, keepdims=True)
        be_ref[...] = jnp.where(lane_b == 0, be, jnp.where(lane_b == 1, nvalid, 0.0)).astype(jnp.int32)

    @pl.when(ph == 1)
    def _():
        r = lax.broadcasted_iota(jnp.int32, (tm, tm), 0)
        cc = lax.broadcasted_iota(jnp.int32, (tm, tm), 1)
        lower = jnp.where(cc < r, 1.0, 0.0).astype(BF16)
        cum = jnp.dot(lower, c.astype(BF16), preferred_element_type=F32)
        tot = cum + base_s[0:1, :]
        p0 = jnp.sum(jnp.where(oh0, tot, 0.0), axis=1, keepdims=True)
        p1 = jnp.sum(jnp.where(oh1, tot, 0.0), axis=1, keepdims=True)
        pos = jnp.where(lane_f == 0.0, p0, jnp.where(lane_f == 1.0, p1, 0.0))
        pos_ref[...] = pos.astype(jnp.int32)
        base_s[0:1, :] = base_s[0:1, :] + csum


def _route(ri, *, n_blocks, n_exp):
    t = ri.shape[0]
    tm = min(TOKEN_TILE, t)
    nbp = -(-n_blocks // SUBLANES) * SUBLANES
    kern = functools.partial(_route_kernel, tm=tm, bm=MOE_BLOCK, n_exp=n_exp)
    return pl.pallas_call(
        kern,
        grid=(2, t // tm),
        in_specs=[pl.BlockSpec((tm, LANES), lambda p, i: (i, 0))],
        out_specs=[pl.BlockSpec((tm, LANES), lambda p, i: (i * p, 0)), _full((nbp, LANES))],
        out_shape=[jax.ShapeDtypeStruct((t, LANES), jnp.int32), jax.ShapeDtypeStruct((nbp, LANES), jnp.int32)],
        scratch_shapes=[pltpu.VMEM((SUBLANES, LANES), F32)] * 2,
        compiler_params=_cparams(("arbitrary", "arbitrary")),
        name="moe_route",
    )(ri)


def _scatter_kernel(pos_ref, nv_ref, x_ref, xs_out, zbuf, sem, zsem, *, tm, tiles, n_blocks):
    step = pl.program_id(0)
    blk_rows = MOE_BLOCK * tiles

    def zero_copy(i):
        dst = pl.multiple_of(i * blk_rows, blk_rows)
        return pltpu.make_async_copy(zbuf, xs_out.at[pl.ds(dst, blk_rows), :], zsem)

    @pl.when(step == 0)
    def _():
        zbuf[...] = jnp.zeros(zbuf.shape, F32)

        def zstart(i, carry):
            @pl.when(nv_ref[i] < MOE_BLOCK)
            def _():
                zero_copy(i).start()
            return carry

        def zwait(i, carry):
            @pl.when(nv_ref[i] < MOE_BLOCK)
            def _():
                zero_copy(i).wait()
            return carry

        lax.fori_loop(0, n_blocks, zstart, 0)
        lax.fori_loop(0, n_blocks, zwait, 0)

    def row_copy(t, k):
        src = pl.multiple_of(t * tiles, tiles)
        dst = pl.multiple_of(pos_ref[2 * t + k] * tiles, tiles)
        return pltpu.make_async_copy(x_ref.at[pl.ds(src, tiles), :], xs_out.at[pl.ds(dst, tiles), :], sem)

    def start(t, carry):
        row_copy(t, 0).start()
        row_copy(t, 1).start()
        return carry

    def wait(t, carry):
        row_copy(t, 0).wait()
        row_copy(t, 1).wait()
        return carry

    lax.fori_loop(0, tm, start, 0, unroll=DMA_UNROLL)
    lax.fori_loop(0, tm, wait, 0, unroll=DMA_UNROLL)


def _scatter(pos_flat, nvalid, x1t, n_blocks, tiles):
    t = x1t.shape[0] // tiles
    tm = min(TOKEN_TILE, t)
    kern = functools.partial(_scatter_kernel, tm=tm, tiles=tiles, n_blocks=n_blocks)
    return pl.pallas_call(
        kern,
        grid=(t // tm,),
        in_specs=[
            pl.BlockSpec((2 * tm,), lambda i: (i,), memory_space=pltpu.SMEM),
            pl.BlockSpec(memory_space=pltpu.SMEM),
            pl.BlockSpec((tm * tiles, LANES), lambda i: (i, 0)),
        ],
        out_specs=pl.BlockSpec(memory_space=pl.ANY),
        out_shape=jax.ShapeDtypeStruct((n_blocks * MOE_BLOCK * tiles, LANES), F32),
        scratch_shapes=[pltpu.VMEM((MOE_BLOCK * tiles, LANES), F32),
                        pltpu.SemaphoreType.DMA(()), pltpu.SemaphoreType.DMA(())],
        compiler_params=_cparams(("arbitrary",)),
        name="moe_scatter",
    )(pos_flat, nvalid, x1t)


def _expert_kernel(be_ref, nv_ref, x_ref, wg_ref, wu_ref, wd_ref, y_ref, *, tiles):
    del be_ref
    n_valid = nv_ref[pl.program_id(0)]

    @pl.when(n_valid == 0)
    def _():
        y_ref[...] = jnp.zeros(y_ref.shape, F32)

    @pl.when(n_valid > 0)
    def _():
        xb = _load_token_tiles(x_ref, MOE_BLOCK, tiles).astype(BF16)
        g = jnp.dot(xb, wg_ref[0], preferred_element_type=F32)
        u = jnp.dot(xb, wu_ref[0], preferred_element_type=F32)
        h = g * jax.nn.sigmoid(g) * u
        _store_token_tiles(y_ref, jnp.dot(h.astype(BF16), wd_ref[0], preferred_element_type=F32))


def _experts(block_expert, nvalid, xs, wg, wu, wd):
    d, de = wg.shape[1], wg.shape[2]
    tiles = d // LANES
    blk_rows = MOE_BLOCK * tiles
    nb = xs.shape[0] // blk_rows
    grid_spec = pltpu.PrefetchScalarGridSpec(
        num_scalar_prefetch=2,
        grid=(nb,),
        in_specs=[
            pl.BlockSpec((blk_rows, LANES), lambda i, be, nv: (i, 0)),
            pl.BlockSpec((1, d, de), lambda i, be, nv: (be[i], 0, 0)),
            pl.BlockSpec((1, d, de), lambda i, be, nv: (be[i], 0, 0)),
            pl.BlockSpec((1, de, d), lambda i, be, nv: (be[i], 0, 0)),
        ],
        out_specs=pl.BlockSpec((blk_rows, LANES), lambda i, be, nv: (i, 0)),
    )
    return pl.pallas_call(
        functools.partial(_expert_kernel, tiles=tiles),
        grid_spec=grid_spec,
        out_shape=jax.ShapeDtypeStruct(xs.shape, F32),
        compiler_params=_cparams(("arbitrary",)),
        name="moe_experts",
    )(block_expert, nvalid, xs, wg, wu, wd)


def _final_kernel(pos_ref, posn_ref, x1_ref, ri_ref, p_ref, ys_hbm, wpg_ref, wpp_ref, g2_ref, b2_ref,
                  pg_ref, out_ref, ybuf, sems, *, tm, tiles):
    step = pl.program_id(0)
    n_steps = pl.num_programs(0)
    slot = step % 2

    def row_copy(tab_ref, t, k, sl):
        src = pl.multiple_of(tab_ref[2 * t + k] * tiles, tiles)
        dst = pl.multiple_of(t * tiles, tiles)
        return pltpu.make_async_copy(ys_hbm.at[pl.ds(src, tiles), :], ybuf.at[sl, k, pl.ds(dst, tiles), :],
                                     sems.at[sl])

    def start_tile(tab_ref, sl):
        def body(t, carry):
            row_copy(tab_ref, t, 0, sl).start()
            row_copy(tab_ref, t, 1, sl).start()
            return carry

        lax.fori_loop(0, tm, body, 0, unroll=DMA_UNROLL)

    @pl.when(step == 0)
    def _():
        start_tile(pos_ref, 0)

    @pl.when(step + 1 < n_steps)
    def _():
        start_tile(posn_ref, 1 - slot)

    def wait(t, carry):
        row_copy(pos_ref, t, 0, slot).wait()
        row_copy(pos_ref, t, 1, slot).wait()
        return carry

    lax.fori_loop(0, tm, wait, 0, unroll=DMA_UNROLL)

    ri = ri_ref[...]
    y0 = _load_token_tiles(ybuf.at[slot, 0], tm, tiles)
    y1 = _load_token_tiles(ybuf.at[slot, 1], tm, tiles)
    ffn = ri[:, 2:3] * y0 + ri[:, 3:4] * y1
    x2 = _layer_norm(DN_ALPHA * x1_ref[...] + ffn, g2_ref[...], b2_ref[...])
    gate = jax.nn.sigmoid(jnp.dot(x2.astype(BF16), wpg_ref[...], preferred_element_type=F32))
    proj = jnp.dot(p_ref[...].astype(BF16), wpp_ref[...], preferred_element_type=F32)
    out_ref[...] = x2 + _rms_norm(gate * proj, pg_ref[...])


def _final(pos_flat, x1, ri, p, ys, wpg, wpp, g2, b2, pg):
    t, d = x1.shape
    tm = min(TOKEN_TILE, t)
    n_steps = t // tm
    tiles = d // LANES
    row = lambda w: pl.BlockSpec((tm, w), lambda i: (i, 0))
    consts = [wpg, wpp, g2, b2, pg]
    kern = functools.partial(_final_kernel, tm=tm, tiles=tiles)
    return pl.pallas_call(
        kern,
        grid=(n_steps,),
        in_specs=[pl.BlockSpec((2 * tm,), lambda i: (i,), memory_space=pltpu.SMEM),
                  pl.BlockSpec((2 * tm,), lambda i: (jnp.minimum(i + 1, n_steps - 1),), memory_space=pltpu.SMEM),
                  row(d), row(LANES), row(p.shape[1]), pl.BlockSpec(memory_space=pl.ANY)]
                 + [_full(a.shape) for a in consts],
        out_specs=row(d),
        out_shape=jax.ShapeDtypeStruct((t, d), F32),
        scratch_shapes=[pltpu.VMEM((2, 2, tm * tiles, LANES), F32), pltpu.SemaphoreType.DMA((2,))],
        compiler_params=_cparams(("arbitrary",)),
        name="combine_ln2_ple",
    )(pos_flat, pos_flat, x1, ri, p, ys, *consts)


def _lane_patterns(pair_layout):
    lane = np.arange(LANES)
    if pair_layout:
        valid = (lane < 2 * QK_ROPE).astype(np.float32)
    else:
        valid = np.ones(LANES, np.float32)
    sign = np.where((lane % QK_ROPE) < HALF_ROPE, -1.0, 1.0).astype(np.float32) * valid
    return lane % HALF_ROPE, sign, valid


def _prep_weights(w_in, w_uq, w_uk, w_uv, w_group, w_router):
    d = w_in.shape[0]
    ql = w_uq.shape[0]
    kvl, n_heads, _ = w_uk.shape
    o_kv, o_kr = ql, ql + kvl
    o_conv = o_kr + QK_ROPE
    c_conv = (w_in.shape[1] - o_conv - 2 * d) // 2
    o_ga = o_conv + 2 * c_conv
    o_gb = o_ga + d
    kr = w_in[:, o_kr:o_conv]
    kr_sw = jnp.concatenate([kr[:, HALF_ROPE:], kr[:, :HALF_ROPE]], axis=1)
    zpad = jnp.zeros((d, LANES - 2 * QK_ROPE), w_in.dtype)
    win_r = jnp.concatenate([
        w_in[:, :o_kv], w_in[:, o_kv:o_kr], w_in[:, o_conv:o_ga], w_in[:, o_ga:o_gb], w_in[:, o_gb:],
        kr, kr, zpad, kr_sw, kr_sw, zpad], axis=1).astype(BF16)
    assert win_r.shape[1] == C_END and c_conv == C_CB - C_CA and kvl == C_CA - C_KV and ql == C_KV

    hd = QK_NOPE + QK_ROPE
    wq = w_uq.reshape(ql, n_heads, hd)
    nope = wq[:, :, :QK_NOPE].reshape(ql, n_heads * QK_NOPE)
    x1 = wq[:, :, QK_NOPE:QK_NOPE + HALF_ROPE]
    x2 = wq[:, :, QK_NOPE + HALF_ROPE:]
    rope_a = jnp.concatenate([x1, x2], axis=2)
    rope_b = jnp.concatenate([x2, x1], axis=2)
    wuq_s = jnp.concatenate([nope, rope_a.reshape(ql, -1), rope_b.reshape(ql, -1)], axis=1).astype(BF16)

    def pair_layout(r):
        r = r.reshape(ql, n_heads // 2, 2 * QK_ROPE)
        z = jnp.zeros((ql, n_heads // 2, LANES - 2 * QK_ROPE), r.dtype)
        return jnp.concatenate([r, z], axis=2).reshape(ql, -1)

    wuq_p = jnp.concatenate([nope, pair_layout(rope_a), pair_layout(rope_b)], axis=1).astype(BF16)

    wuk_flat = w_uk.reshape(kvl, n_heads * QK_NOPE)
    wuv_flat = w_uv.reshape(kvl, n_heads * V_HEAD)
    wukv = jnp.concatenate([wuk_flat, wuv_flat], axis=1).astype(BF16)
    eye = jnp.eye(n_heads, dtype=w_uk.dtype)
    wk_bd = jnp.einsum('lhn,hg->hngl', w_uk, eye).reshape(n_heads * QK_NOPE, n_heads * kvl).astype(BF16)
    wv_bd = jnp.einsum('lhv,hg->hlgv', w_uv, eye).reshape(n_heads * kvl, n_heads * V_HEAD).astype(BF16)

    n_exp = w_router.shape[1]
    wgr = jnp.concatenate([w_group, w_router,
                           jnp.zeros((d, LANES - N_GROUPS - n_exp), w_group.dtype)], axis=1)
    wgh = wgr.astype(BF16)
    wgl = (wgr - wgh.astype(F32)).astype(BF16)
    return win_r, wuq_p, wuq_s, wukv, wk_bd, wv_bd, wgh, wgl


def _moe(x1t, ri, wg, wu, wd, n_exp):
    t = ri.shape[0]
    tiles = x1t.shape[0] // t
    n_assign = 2 * t
    n_blocks = (n_assign + n_exp * (MOE_BLOCK - 1) + MOE_BLOCK - 1) // MOE_BLOCK
    pos, be = _route(ri, n_blocks=n_blocks, n_exp=n_exp)
    pos_flat = pos[:, :2].reshape(-1)
    block_expert = be[:n_blocks, 0]
    nvalid = be[:n_blocks, 1]
    xs = _scatter(pos_flat, nvalid, x1t, n_blocks, tiles)
    ys = _experts(block_expert, nvalid, xs, wg, wu, wd)
    return pos_flat, ys


def kernel(x_prompt, x_sample, cache_ckv, cache_krope, state_conv, page_table, p_prompt, p_sample, w_in, q_norm_g, w_uq, kv_norm_g, w_uk, w_uv, w_o_attn, w_dw, b_dw, conv_ln_g, conv_ln_b, w_pw2, w_out, ln1_g, ln1_b, w_group, w_router, w_gate, w_up, w_down, ln2_g, ln2_b, w_ple_gate, w_ple_proj, ple_norm_g):
    assert w_in.shape[0] == DEPTH
    b, s, d = x_prompt.shape
    bd, sd, _ = x_sample.shape
    assert sd == 1
    n_pages = page_table.shape[1]
    page_rows = cache_ckv.shape[2]
    past = n_pages * page_rows
    n_heads = w_uk.shape[2]
    kvl = w_uk.shape[1]
    width = w_dw.shape[1]
    n_exp = w_router.shape[2]
    c_conv = w_dw.shape[2]

    win_r, wuq_p, wuq_s, wukv, wk_bd, wv_bd, wgh, wgl = _prep_weights(
        w_in[0], w_uq[0], w_uk[0], w_uv[0], w_group[0], w_router[0])
    qg, kvg = q_norm_g, kv_norm_g
    wo, wpw, wout = w_o_attn[0].astype(BF16), w_pw2[0].astype(BF16), w_out[0].astype(BF16)
    wgate, wup, wdown = w_gate[0].astype(BF16), w_up[0].astype(BF16), w_down[0].astype(BF16)
    wpg, wpp = w_ple_gate[0].astype(BF16), w_ple_proj[0].astype(BF16)
    w_dw_pad = jnp.concatenate([w_dw[0], jnp.zeros((CONV_HALO - width, c_conv), F32)], axis=0)

    inv16 = 1.0 / (ROPE_THETA ** (jnp.arange(HALF_ROPE, dtype=F32) / HALF_ROPE))

    def tables(pair_layout, n_rows, offset, step):
        idx, sign, valid = _lane_patterns(pair_layout)
        inv_l = (inv16[idx] * valid)[None, :]
        return _rope_tables(n_rows, offset, step, inv_l, jnp.asarray(sign)[None, :], jnp.asarray(valid)[None, :])

    def trunk_tail(x2d, o, y, sa, sb, p2d):
        x1, x1t, ri = _mix(x2d, o, y, sa, sb, wo, wpw, wout, conv_ln_g, conv_ln_b, ln1_g, ln1_b, wgh, wgl)
        pos_flat, ys = _moe(x1t, ri, wgate, wup, wdown, n_exp)
        return _final(pos_flat, x1, ri, p2d, ys, wpg, wpp, ln2_g, ln2_b, ple_norm_g)

    xs_ = x_sample.reshape(bd, d)
    cos_s, sin_s = tables(False, SUBLANES, past, 0)
    qn_s, qr_s, ckv_s, kr_s, u_s, sa_s, sb_s = _inproj(
        xs_, win_r, qg, kvg, wuq_s, None, cos_s, sin_s, n_rep=n_heads * QK_ROPE // LANES, seq=1, make_kv=False)
    q_lat = _dense(qn_s, wk_bd, F32).reshape(bd, n_heads, kvl)
    o_lat = _sample_attention(page_table, q_lat, qr_s.reshape(bd, n_heads, QK_ROPE),
                              ckv_s.reshape(bd, 1, kvl), kr_s.reshape(bd, 1, QK_ROPE),
                              cache_ckv[0], jnp.swapaxes(cache_krope[0], 1, 2))
    o_s = _dense(o_lat.reshape(bd, n_heads * kvl), wv_bd, BF16)
    ext_s = jnp.concatenate([state_conv[0], u_s[:, None, :]], axis=1)
    y_s = _conv_sample(jnp.transpose(ext_s, (1, 0, 2)), w_dw_pad, b_dw, width=width)
    out_s = trunk_tail(xs_, o_s, y_s, sa_s, sb_s, p_sample[0].reshape(bd, -1))
    new_conv_s = ext_s[:, 1:, :]

    xp = x_prompt.reshape(b * s, d)
    cos_p, sin_p = tables(True, s, 0, 1)
    qn, qr, kn, v, krd, ckv_p, kr_p, u_p, sa_p, sb_p = _inproj(
        xp, win_r, qg, kvg, wuq_p, wukv, cos_p, sin_p, n_rep=n_heads // 2, seq=s, make_kv=True)
    o_p = _prompt_attention(qn, qr, kn, krd, v, batch=b, seq=s)
    y_p = _conv_prompt(u_p, w_dw_pad, b_dw, batch=b, seq=s, width=width)
    out_p = trunk_tail(xp, o_p, y_p, sa_p, sb_p, p_prompt[0].reshape(b * s, -1))
    u_p3 = u_p.reshape(b, s, c_conv)
    new_conv_p = u_p3[:, s - (width - 1):, :]

    return (out_p.reshape(b, s, d), out_s.reshape(bd, 1, d),
            ckv_p.reshape(1, b, s, kvl), kr_p.reshape(1, b, s, QK_ROPE), new_conv_p[None],
            ckv_s.reshape(1, bd, 1, kvl), kr_s.reshape(1, bd, 1, QK_ROPE), new_conv_s[None])
```

```python
import functools

import numpy as np
import jax
import jax.numpy as jnp
from jax import lax
from jax.experimental import pallas as pl
from jax.experimental.pallas import tpu as pltpu

F32 = jnp.float32
BF16 = jnp.bfloat16

QK_NOPE = 64
QK_ROPE = 32
HALF_ROPE = QK_ROPE // 2
V_HEAD = 64
ROPE_THETA = 10000.0
ATTN_SCALE = (QK_NOPE + QK_ROPE) ** -0.5
LOG2E = 1.4426950408889634
N_GROUPS = 4
EXPERTS_PER_GROUP = 8
DEPTH = 1
DN_ALPHA = (2 * DEPTH) ** 0.25
LN_EPS = 1e-5
RMS_EPS = 1e-6
NEG_INF = -1e30

LANES = 128
SUBLANES = 8
VMEM_LIMIT = 56 * 1024 * 1024

TOKEN_TILE = 512
ATTN_TILE = 512
CONV_CHUNK = 64
CONV_HALO = 32
MOE_BLOCK = 512
MOE_BLOCK_MIN = 128
PAGES_PER_CHUNK = 16
SEQS_PER_STEP = 4
DMA_UNROLL = 8


def _cparams(sem, vmem=VMEM_LIMIT):
    return pltpu.CompilerParams(dimension_semantics=sem, vmem_limit_bytes=vmem)


def _full(shape):
    n = len(shape)
    return pl.BlockSpec(shape, lambda *_: (0,) * n)


def _layer_norm(x, g, b):
    mu = jnp.mean(x, axis=-1, keepdims=True)
    xc = x - mu
    var = jnp.mean(xc * xc, axis=-1, keepdims=True)
    return xc * lax.rsqrt(var + LN_EPS) * g + b


def _rms_norm(x, g):
    return x * lax.rsqrt(jnp.mean(x * x, axis=-1, keepdims=True) + RMS_EPS) * g


def _rope_table_kernel(inv_ref, sign_ref, valid_ref, cos_ref, sin_ref, *, offset, step):
    rows = cos_ref.shape[0]
    r = lax.broadcasted_iota(jnp.int32, (rows, LANES), 0) + pl.program_id(0) * rows
    pos = (offset + step * r).astype(F32)
    ang = pos * inv_ref[...]
    cos_ref[...] = jnp.cos(ang) * valid_ref[...]
    sin_ref[...] = jnp.sin(ang) * sign_ref[...]


def _rope_tables(n_rows, offset, step, inv_l, sign_l, valid_l):
    tr = min(n_rows, 512)
    kern = functools.partial(_rope_table_kernel, offset=offset, step=step)
    return pl.pallas_call(
        kern,
        grid=(n_rows // tr,),
        in_specs=[_full((1, LANES))] * 3,
        out_specs=[pl.BlockSpec((tr, LANES), lambda i: (i, 0))] * 2,
        out_shape=[jax.ShapeDtypeStruct((n_rows, LANES), F32)] * 2,
        compiler_params=_cparams(("arbitrary",)),
        name="rope_tables",
    )(inv_l, sign_l, valid_l)


C_Q, C_KV, C_CA, C_CB, C_GA, C_GB, C_KA, C_KB, C_END = 0, 384, 640, 1152, 1664, 2688, 3712, 3840, 3968


def _inproj_kernel(x_ref, win_ref, qg_ref, kvg_ref, wuq_ref, *rest, n_rep, make_kv, bcast, q_scale):
    if make_kv:
        (wukv_ref, cos_ref, sin_ref, qn_o, qr_o, kn_o, v_o, krd_o, ckv_o, kr_o, u_o, sa_o, sb_o) = rest
    else:
        (cos_ref, sin_ref, qn_o, qr_o, ckv_o, kr_o, u_o, sa_o, sb_o) = rest
    xb = x_ref[...].astype(BF16)

    def proj(a, b):
        return jnp.dot(xb, win_ref[:, a:b], preferred_element_type=F32)

    if bcast:
        c1, s1 = cos_ref[0:1, :], sin_ref[0:1, :]
    else:
        c1, s1 = cos_ref[...], sin_ref[...]
    cn = jnp.concatenate([c1] * n_rep, axis=1)
    sn = jnp.concatenate([s1] * n_rep, axis=1)
    d_nope = qn_o.shape[1]
    nr = LANES * n_rep

    cqn = _rms_norm(proj(C_Q, C_KV), qg_ref[...])
    qall = jnp.dot(cqn.astype(BF16), wuq_ref[...], preferred_element_type=F32)
    qn_o[...] = (qall[:, :d_nope] * q_scale).astype(qn_o.dtype)
    qa = qall[:, d_nope:d_nope + nr]
    qb = qall[:, d_nope + nr:d_nope + 2 * nr]
    qr_o[...] = ((qa * cn + qb * sn) * q_scale).astype(qr_o.dtype)

    ckvn = _rms_norm(proj(C_KV, C_CA), kvg_ref[...])
    ckv_o[...] = ckvn
    krot = proj(C_KA, C_KB) * c1 + proj(C_KB, C_END) * s1
    kr_o[...] = krot[:, :QK_ROPE]
    if make_kv:
        kv = jnp.dot(ckvn.astype(BF16), wukv_ref[...], preferred_element_type=F32)
        half = kv.shape[1] // 2
        kn_o[...] = kv[:, :half].astype(BF16)
        v_o[...] = kv[:, half:].astype(BF16)
        krd_o[...] = krot.astype(BF16)

    u_o[...] = proj(C_CA, C_CB) * jax.nn.sigmoid(proj(C_CB, C_GA))
    sa_o[...] = jax.nn.sigmoid(proj(C_GA, C_GB)).astype(BF16)
    sb_o[...] = jax.nn.sigmoid(proj(C_GB, C_KA)).astype(BF16)


def _inproj(x, win_r, qg, kvg, wuq_r, wukv, cos_t, sin_t, *, n_rep, seq, make_kv):
    t, d = x.shape
    tm = min(TOKEN_TILE, t)
    d_nope = wuq_r.shape[1] - 2 * LANES * n_rep
    nr = LANES * n_rep
    c_conv = C_CB - C_CA
    kvl = C_CA - C_KV
    bcast = seq < tm
    if bcast:
        tab_spec = _full(cos_t.shape)
    else:
        nst = seq // tm
        tab_spec = pl.BlockSpec((tm, LANES), lambda i: (i % nst, 0))
    row = lambda w: pl.BlockSpec((tm, w), lambda i: (i, 0))
    in_specs = [row(d), _full(win_r.shape), _full(qg.shape), _full(kvg.shape), _full(wuq_r.shape)]
    args = [x, win_r, qg, kvg, wuq_r]
    if make_kv:
        in_specs.append(_full(wukv.shape))
        args.append(wukv)
    in_specs += [tab_spec, tab_spec]
    args += [cos_t, sin_t]
    q_dt = BF16 if make_kv else F32
    outs = [(d_nope, BF16), (nr, q_dt)]
    if make_kv:
        outs += [(d_nope, BF16), (d_nope, BF16), (LANES, BF16)]
    outs += [(kvl, F32), (QK_ROPE, F32), (c_conv, F32), (d, BF16), (d, BF16)]
    q_scale = ATTN_SCALE * LOG2E if make_kv else ATTN_SCALE
    kern = functools.partial(_inproj_kernel, n_rep=n_rep, make_kv=make_kv, bcast=bcast, q_scale=q_scale)
    return pl.pallas_call(
        kern,
        grid=(t // tm,),
        in_specs=in_specs,
        out_specs=[row(w) for w, _ in outs],
        out_shape=[jax.ShapeDtypeStruct((t, w), dt) for w, dt in outs],
        compiler_params=_cparams(("arbitrary",)),
        name="inproj_kv" if make_kv else "inproj_q",
    )(*args)


def _online_softmax(state, s, v):
    m, l, acc = state
    m_new = jnp.maximum(m, jnp.max(s, axis=1, keepdims=True))
    alpha = jnp.exp2(m - m_new)
    p = jnp.exp2(s - jnp.concatenate([m_new] * (s.shape[1] // LANES), axis=1))
    l = alpha * l + jnp.sum(p, axis=1, keepdims=True)
    acc = alpha * acc + jnp.dot(p.astype(BF16), v, preferred_element_type=F32)
    return m_new, l, acc


def _pattn_kernel(qn_ref, qr_ref, kn_ref, krd_ref, v_ref, o_ref, m_s, l_s, acc_s, *, blk, nq):
    half = blk // 2
    lane = lax.broadcasted_iota(jnp.int32, (blk, LANES), 1)
    row_a = lax.broadcasted_iota(jnp.int32, (blk, half), 0)
    col_a = lax.broadcasted_iota(jnp.int32, (blk, half), 1)
    row_b = lax.broadcasted_iota(jnp.int32, (half, half), 0)
    col_b = lax.broadcasted_iota(jnp.int32, (half, half), 1)
    dn_t = (((1,), (1,)), ((), ()))
    for i in range(nq):
        qn = qn_ref[i * blk:(i + 1) * blk, :]
        qr = qr_ref[i * blk:(i + 1) * blk, :]
        zero = jnp.zeros_like(qn)
        q_heads = (
            jnp.concatenate([jnp.where(lane < QK_NOPE, qn, zero), jnp.where(lane < QK_ROPE, qr, zero)], axis=1),
            jnp.concatenate([jnp.where(lane >= QK_NOPE, qn, zero),
                             jnp.where((lane >= QK_ROPE) & (lane < 2 * QK_ROPE), qr, zero)], axis=1),
        )
        state = [(jnp.full((blk, LANES), NEG_INF, F32), jnp.zeros((blk, LANES), F32),
                  jnp.zeros((blk, LANES), F32))] * 2
        for j in range(i):
            k = jnp.concatenate([kn_ref[j * blk:(j + 1) * blk, :], krd_ref[j * blk:(j + 1) * blk, :]], axis=1)
            v = v_ref[j * blk:(j + 1) * blk, :]
            for h in range(2):
                s = lax.dot_general(q_heads[h], k, dn_t, preferred_element_type=F32)
                state[h] = _online_softmax(state[h], s, v)
        k0 = i * blk
        for h in range(2):
            k_a = jnp.concatenate([kn_ref[k0:k0 + half, :], krd_ref[k0:k0 + half, :]], axis=1)
            s_a = lax.dot_general(q_heads[h], k_a, dn_t, preferred_element_type=F32)
            s_a = jnp.where(col_a <= row_a, s_a, NEG_INF)
            m, l, acc = _online_softmax(state[h], s_a, v_ref[k0:k0 + half, :])
            m_s[i, h], l_s[i, h], acc_s[i, h] = m, l, acc
            k_b = jnp.concatenate([kn_ref[k0 + half:k0 + blk, :], krd_ref[k0 + half:k0 + blk, :]], axis=1)
            s_b = lax.dot_general(q_heads[h][half:], k_b, dn_t, preferred_element_type=F32)
            s_b = jnp.where(col_b <= row_b, s_b, NEG_INF)
            _, l_b, acc_b = _online_softmax(
                (m_s[i, h, half:, :], l_s[i, h, half:, :], acc_s[i, h, half:, :]), s_b,
                v_ref[k0 + half:k0 + blk, :])
            l_s[i, h, half:, :] = l_b
            acc_s[i, h, half:, :] = acc_b
        o = jnp.where(lane < V_HEAD, acc_s[i, 0] / l_s[i, 0], acc_s[i, 1] / l_s[i, 1])
        o_ref[i * blk:(i + 1) * blk, :] = o.astype(o_ref.dtype)


def _prompt_attention(qn, qr, kn, krd, v, *, batch, seq):
    t, d = qn.shape
    blk = min(ATTN_TILE, seq)
    nq = seq // blk
    n_pairs = d // LANES
    kspec = pl.BlockSpec((seq, LANES), lambda b, j: (b, j))
    kern = functools.partial(_pattn_kernel, blk=blk, nq=nq)
    return pl.pallas_call(
        kern,
        grid=(batch, n_pairs),
        in_specs=[kspec, kspec, kspec, pl.BlockSpec((seq, LANES), lambda b, j: (b, 0)), kspec],
        out_specs=kspec,
        out_shape=jax.ShapeDtypeStruct((t, d), BF16),
        scratch_shapes=[pltpu.VMEM((nq, 2, blk, LANES), F32)] * 3,
        compiler_params=_cparams(("arbitrary",) * 2),
        name="prompt_attention",
    )(qn, qr, kn, krd, v)


def _dense_kernel(a_ref, w_ref, o_ref):
    o_ref[...] = jnp.dot(a_ref[...].astype(BF16), w_ref[...], preferred_element_type=F32).astype(o_ref.dtype)


def _dense(a, w, out_dtype):
    m, n = a.shape[0], w.shape[1]
    return pl.pallas_call(
        _dense_kernel,
        grid=(1,),
        in_specs=[_full(a.shape), _full(w.shape)],
        out_specs=_full((m, n)),
        out_shape=jax.ShapeDtypeStruct((m, n), out_dtype),
        compiler_params=_cparams(("arbitrary",)),
        name="dense",
    )(a, w)


def _sattn_kernel(pt_ref, ptn_ref, ql_ref, qr_ref, cnew_ref, knew_ref, ckv_hbm, krt_hbm, o_ref,
                  cbuf, kbuf, sems, *, n_chunks, ppc, nb):
    step = pl.program_id(0)
    n_steps = pl.num_programs(0)
    kvl = cbuf.shape[4]
    kc = ppc * cbuf.shape[3]
    n_heads = ql_ref.shape[1]
    dn_t = (((1,), (1,)), ((), ()))

    def chunk_copies(tab_ref, chunk, slot):
        out = []
        for b in range(nb):
            for pg in range(ppc):
                page = tab_ref[b, 0, chunk * ppc + pg]
                out.append(pltpu.make_async_copy(ckv_hbm.at[page], cbuf.at[slot, b, pg], sems.at[0, slot]))
                out.append(pltpu.make_async_copy(krt_hbm.at[page], kbuf.at[slot, b, pg], sems.at[1, slot]))
        return out

    def start_chunk(tab_ref, chunk, slot):
        for cp in chunk_copies(tab_ref, chunk, slot):
            cp.start()

    @pl.when(step == 0)
    def _():
        start_chunk(pt_ref, 0, 0)

    first = step * n_chunks
    qls = [ql_ref[b].astype(BF16) for b in range(nb)]
    qrs = [qr_ref[b].astype(BF16) for b in range(nb)]

    def body(c, carry):
        slot = (first + c) % 2

        @pl.when(c + 1 < n_chunks)
        def _():
            start_chunk(pt_ref, c + 1, 1 - slot)

        @pl.when((c + 1 == n_chunks) & (step + 1 < n_steps))
        def _():
            start_chunk(ptn_ref, 0, 1 - slot)

        for cp in chunk_copies(pt_ref, c, slot):
            cp.wait()
        new = []
        for b in range(nb):
            m, l, acc = carry[3 * b:3 * b + 3]
            ck = cbuf[slot, b].reshape(kc, kvl).astype(BF16)
            krt = jnp.concatenate([kbuf[slot, b, pg] for pg in range(ppc)], axis=1).astype(BF16)
            s = lax.dot_general(qls[b], ck, dn_t, preferred_element_type=F32)
            s = s + jnp.dot(qrs[b], krt, preferred_element_type=F32)
            m_new = jnp.maximum(m, jnp.max(s, axis=1, keepdims=True))
            alpha = jnp.exp(m - m_new)
            p = jnp.exp(s - m_new)
            l = alpha * l + jnp.sum(p, axis=1, keepdims=True)
            acc = alpha * acc + jnp.dot(p.astype(BF16), ck, preferred_element_type=F32)
            new += [m_new, l, acc]
        return tuple(new)

    init = (jnp.full((n_heads, 1), NEG_INF, F32), jnp.zeros((n_heads, 1), F32),
            jnp.zeros((n_heads, kvl), F32)) * nb
    carry = lax.fori_loop(0, n_chunks, body, init)

    for b in range(nb):
        m, l, acc = carry[3 * b:3 * b + 3]
        cnew = cnew_ref[b]
        knew = knew_ref[b]
        s_new = (jnp.sum(ql_ref[b] * cnew, axis=1, keepdims=True)
                 + jnp.sum(qr_ref[b] * knew, axis=1, keepdims=True))
        m_new = jnp.maximum(m, s_new)
        alpha = jnp.exp(m - m_new)
        p_new = jnp.exp(s_new - m_new)
        l = alpha * l + p_new
        acc = alpha * acc + p_new * cnew
        o_ref[b] = acc / l


def _sample_attention(page_table, q_lat, q_rope, c_new, k_new, cache_ckv, cache_krope_t):
    bd, n_heads, kvl = q_lat.shape
    n_pages = page_table.shape[1]
    page_rows = cache_ckv.shape[1]
    ppc = min(PAGES_PER_CHUNK, n_pages)
    n_chunks = n_pages // ppc
    nb = min(SEQS_PER_STEP, bd)
    n_steps = bd // nb
    pt3 = page_table.reshape(bd, 1, n_pages)
    per_b = lambda s: pl.BlockSpec((nb,) + s, lambda i: (i, 0, 0))
    kern = functools.partial(_sattn_kernel, n_chunks=n_chunks, ppc=ppc, nb=nb)
    return pl.pallas_call(
        kern,
        grid=(n_steps,),
        in_specs=[
            pl.BlockSpec((nb, 1, n_pages), lambda i: (i, 0, 0), memory_space=pltpu.SMEM),
            pl.BlockSpec((nb, 1, n_pages), lambda i: (jnp.minimum(i + 1, n_steps - 1), 0, 0),
                         memory_space=pltpu.SMEM),
            per_b((n_heads, kvl)), per_b((n_heads, QK_ROPE)), per_b((1, kvl)), per_b((1, QK_ROPE)),
            pl.BlockSpec(memory_space=pl.ANY), pl.BlockSpec(memory_space=pl.ANY),
        ],
        out_specs=per_b((n_heads, kvl)),
        out_shape=jax.ShapeDtypeStruct((bd, n_heads, kvl), F32),
        scratch_shapes=[
            pltpu.VMEM((2, nb, ppc, page_rows, kvl), F32),
            pltpu.VMEM((2, nb, ppc, QK_ROPE, page_rows), F32),
            pltpu.SemaphoreType.DMA((2, 2)),
        ],
        compiler_params=_cparams(("arbitrary",)),
        name="sample_attention",
    )(pt3, pt3, q_lat, q_rope, c_new, k_new, cache_ckv, cache_krope_t)


def _conv_rows(u_ref, halo_ref, w_ref, b_ref, y_ref, ext_s, first_tile, width):
    ts = u_ref.shape[0]
    left = jnp.where(first_tile, jnp.zeros_like(halo_ref), halo_ref[...])
    ext_s[0:CONV_HALO, :] = left
    ext_s[CONV_HALO:CONV_HALO + ts, :] = u_ref[...]
    base = CONV_HALO - (width - 1)
    for c in range(ts // CONV_CHUNK):
        r0 = c * CONV_CHUNK
        acc = jnp.broadcast_to(b_ref[...], (CONV_CHUNK, u_ref.shape[1]))
        for k in range(width):
            acc = acc + w_ref[k:k + 1, :] * ext_s[r0 + base + k:r0 + base + k + CONV_CHUNK, :]
        y_ref[r0:r0 + CONV_CHUNK, :] = acc


def _conv_sample_kernel(ext_ref, w_ref, b_ref, y_ref, *, width):
    acc = jnp.broadcast_to(b_ref[...], y_ref.shape)
    for k in range(width):
        acc = acc + w_ref[k:k + 1, :] * ext_ref[k]
    y_ref[...] = acc


def _conv_sample(ext_t, w_pad, b, *, width):
    _, bd, c = ext_t.shape
    kern = functools.partial(_conv_sample_kernel, width=width)
    return pl.pallas_call(
        kern,
        grid=(1,),
        in_specs=[_full(ext_t.shape), _full(w_pad.shape), _full(b.shape)],
        out_specs=_full((bd, c)),
        out_shape=jax.ShapeDtypeStruct((bd, c), F32),
        compiler_params=_cparams(("arbitrary",)),
        name="conv_sample",
    )(ext_t, w_pad, b)


def _store_token_tiles(ref, x):
    rows, d = x.shape
    tiles = d // LANES
    for c in range(tiles):
        ref[pl.ds(c, rows, stride=tiles), :] = x[:, c * LANES:(c + 1) * LANES]


def _load_token_tiles(ref, rows, tiles):
    return jnp.concatenate([ref[pl.ds(c, rows, stride=tiles), :] for c in range(tiles)], axis=1)


def _mix_kernel(x_ref, o_ref, *rest, conv_tiles, width):
    if conv_tiles:
        (u_ref, halo_ref, wdw_ref, bdw_ref, sa_ref, sb_ref, wo_ref, wpw_ref, wout_ref, cg_ref, cb_ref,
         g1_ref, b1_ref, wgh_ref, wgl_ref, x1_ref, x1t_ref, ri_ref, ext_s, y_ref) = rest
        first_tile = pl.program_id(0) % conv_tiles == 0
        _conv_rows(u_ref, halo_ref, wdw_ref, bdw_ref, y_ref, ext_s, first_tile, width)
    else:
        (y_ref, sa_ref, sb_ref, wo_ref, wpw_ref, wout_ref, cg_ref, cb_ref,
         g1_ref, b1_ref, wgh_ref, wgl_ref, x1_ref, x1t_ref, ri_ref) = rest
    branch_b = jnp.dot(o_ref[...], wo_ref[...], preferred_element_type=F32)
    z = _layer_norm(y_ref[...], cg_ref[...], cb_ref[...])
    z = z * jax.nn.sigmoid(z)
    branch_a = jnp.dot(z.astype(BF16), wpw_ref[...], preferred_element_type=F32)
    mixin = sa_ref[...].astype(F32) * branch_a + sb_ref[...].astype(F32) * branch_b
    mix = jnp.dot(mixin.astype(BF16), wout_ref[...], preferred_element_type=F32)
    x1 = _layer_norm(DN_ALPHA * x_ref[...] + mix, g1_ref[...], b1_ref[...])
    x1_ref[...] = x1
    _store_token_tiles(x1t_ref, x1)

    x_hi = x1.astype(BF16)
    x_lo = (x1 - x_hi.astype(F32)).astype(BF16)
    lg = (jnp.dot(x_hi, wgh_ref[...], preferred_element_type=F32)
          + jnp.dot(x_lo, wgh_ref[...], preferred_element_type=F32)
          + jnp.dot(x_hi, wgl_ref[...], preferred_element_type=F32))
    n_exp = N_GROUPS * EXPERTS_PER_GROUP
    lane = lax.broadcasted_iota(jnp.int32, lg.shape, 1)
    lane_f = lane.astype(F32)
    big = float(LANES)
    gmask = lane < N_GROUPS
    lgm = jnp.where(gmask, lg, NEG_INF)
    gmax = jnp.max(lgm, axis=1, keepdims=True)
    gidx = jnp.min(jnp.where(lgm == gmax, lane_f, big), axis=1, keepdims=True)
    pg_sel = 1.0 / jnp.sum(jnp.where(gmask, jnp.exp(lgm - gmax), 0.0), axis=1, keepdims=True)
    egroup = jnp.floor((lane_f - N_GROUPS) * (1.0 / EXPERTS_PER_GROUP))
    emask = (lane >= N_GROUPS) & (lane < N_GROUPS + n_exp) & (egroup == gidx)
    le = jnp.where(emask, lg, NEG_INF)
    v1 = jnp.max(le, axis=1, keepdims=True)
    i1 = jnp.min(jnp.where(le == v1, lane_f, big), axis=1, keepdims=True)
    le2 = jnp.where(lane_f == i1, NEG_INF, le)
    v2 = jnp.max(le2, axis=1, keepdims=True)
    i2 = jnp.min(jnp.where(le2 == v2, lane_f, big), axis=1, keepdims=True)
    e = jnp.exp(v2 - v1)
    gate1 = pg_sel / (1.0 + e)
    gate2 = pg_sel * e / (1.0 + e)
    ri = jnp.where(lane == 0, i1 - N_GROUPS,
                   jnp.where(lane == 1, i2 - N_GROUPS,
                             jnp.where(lane == 2, gate1, jnp.where(lane == 3, gate2, 0.0))))
    ri_ref[...] = ri


def _mix(x, o, y_or_u, sa, sb, wo, wpw, wout, cg, cb, g1, b1, wgh, wgl, conv=None):
    t, d = x.shape
    tm = min(TOKEN_TILE, t)
    c = y_or_u.shape[1]
    row = lambda w: pl.BlockSpec((tm, w), lambda i: (i, 0))
    consts = [wo, wpw, wout, cg, cb, g1, b1, wgh, wgl]
    tiles = d // LANES
    if conv is None:
        kern = functools.partial(_mix_kernel, conv_tiles=0, width=0)
        lead_specs, lead, scratch = [row(c)], [y_or_u], []
    else:
        w_pad, b_dw, seq, width = conv
        assert seq % tm == 0
        hb = tm // CONV_HALO
        kern = functools.partial(_mix_kernel, conv_tiles=seq // tm, width=width)
        lead_specs = [row(c), pl.BlockSpec((CONV_HALO, c), lambda i: (jnp.maximum(i * hb - 1, 0), 0)),
                      _full(w_pad.shape), _full(b_dw.shape)]
        lead = [y_or_u, y_or_u, w_pad, b_dw]
        scratch = [pltpu.VMEM((CONV_HALO + tm, c), F32), pltpu.VMEM((tm, c), F32)]
    return pl.pallas_call(
        kern,
        grid=(t // tm,),
        in_specs=[row(d), row(d)] + lead_specs + [row(d), row(d)] + [_full(a.shape) for a in consts],
        out_specs=[row(d), pl.BlockSpec((tm * tiles, LANES), lambda i: (i, 0)), row(LANES)],
        out_shape=[jax.ShapeDtypeStruct((t, d), F32), jax.ShapeDtypeStruct((t * tiles, LANES), F32),
                   jax.ShapeDtypeStruct((t, LANES), F32)],
        scratch_shapes=scratch,
        compiler_params=_cparams(("arbitrary",)),
        name="mix_ln1_router",
    )(x, o, *lead, sa, sb, *consts)


def _route_kernel(ri_ref, pos_ref, be_ref, cnt_s, base_s, *, tm, bm, n_exp):
    ph = pl.program_id(0)
    i = pl.program_id(1)
    lane_f = lax.broadcasted_iota(jnp.int32, (tm, LANES), 1).astype(F32)
    ri = ri_ref[...]
    oh0 = lane_f == ri[:, 0:1]
    oh1 = lane_f == ri[:, 1:2]
    c = jnp.where(oh0 | oh1, 1.0, 0.0)
    csum = jnp.sum(c, axis=0, keepdims=True)

    @pl.when((ph == 0) & (i == 0))
    def _():
        cnt_s[...] = jnp.zeros(cnt_s.shape, F32)

    @pl.when(ph == 0)
    def _():
        cnt_s[0:1, :] = cnt_s[0:1, :] + csum

    @pl.when((ph == 1) & (i == 0))
    def _():
        cnt = cnt_s[...]
        pc = jnp.floor((cnt + (bm - 1)) * (1.0 / bm)) * bm
        r = lax.broadcasted_iota(jnp.int32, (LANES, LANES), 0)
        cc = lax.broadcasted_iota(jnp.int32, (LANES, LANES), 1)
        upper = jnp.where(r <= cc, 1.0, 0.0)
        pend = jnp.dot(pc, upper, precision=lax.Precision.HIGHEST, preferred_element_type=F32)
        base_s[...] = pend - pc
        nbp = be_ref.shape[0]
        bstart = (lax.broadcasted_iota(jnp.int32, (nbp, LANES), 0) * bm).astype(F32)
        lane_b = lax.broadcasted_iota(jnp.int32, (nbp, LANES), 1)
        hit = jnp.where((lane_b < n_exp) & (pend[0:1, :] <= bstart), 1.0, 0.0)
        be = jnp.minimum(jnp.sum(hit, axis=1, keepdims=True), float(n_exp - 1))
        pstart = pend[0:1, :] - pc[0:1, :]
        span = jnp.minimum(pstart + cnt[0:1, :], bstart + bm) - jnp.maximum(pstart, bstart)
        nvalid = jnp.sum(jnp.where(lane_b < n_exp, jnp.maximum(span, 0.0), 0.0), axis=1, keepdims=True)
        be_ref[...] = jnp.where(lane_b == 0, be, jnp.where(lane_b == 1, nvalid, 0.0)).astype(jnp.int32)

    @pl.when(ph == 1)
    def _():
        r = lax.broadcasted_iota(jnp.int32, (tm, tm), 0)
        cc = lax.broadcasted_iota(jnp.int32, (tm, tm), 1)
        lower = jnp.where(cc < r, 1.0, 0.0).astype(BF16)
        cum = jnp.dot(lower, c.astype(BF16), preferred_element_type=F32)
        tot = cum + base_s[0:1, :]
        p0 = jnp.sum(jnp.where(oh0, tot, 0.0), axis=1, keepdims=True)
        p1 = jnp.sum(jnp.where(oh1, tot, 0.0), axis=1, keepdims=True)
        pos = jnp.where(lane_f == 0.0, p0, jnp.where(lane_f == 1.0, p1, 0.0))
        pos_ref[...] = pos.astype(jnp.int32)
        base_s[0:1, :] = base_s[0:1, :] + csum


def _route(ri, *, n_blocks, n_exp, bm):
    t = ri.shape[0]
    tm = min(TOKEN_TILE, t)
    nbp = -(-n_blocks // SUBLANES) * SUBLANES
    kern = functools.partial(_route_kernel, tm=tm, bm=bm, n_exp=n_exp)
    return pl.pallas_call(
        kern,
        grid=(2, t // tm),
        in_specs=[pl.BlockSpec((tm, LANES), lambda p, i: (i, 0))],
        out_specs=[pl.BlockSpec((tm, LANES), lambda p, i: (i * p, 0)), _full((nbp, LANES))],
        out_shape=[jax.ShapeDtypeStruct((t, LANES), jnp.int32), jax.ShapeDtypeStruct((nbp, LANES), jnp.int32)],
        scratch_shapes=[pltpu.VMEM((SUBLANES, LANES), F32)] * 2,
        compiler_params=_cparams(("arbitrary", "arbitrary")),
        name="moe_route",
    )(ri)


def _scatter_kernel(pos_ref, nv_ref, x_ref, xs_out, zbuf, sem, zsem, *, tm, tiles, n_blocks, bm):
    step = pl.program_id(0)
    blk_rows = bm * tiles

    def zero_copy(i):
        dst = pl.multiple_of(i * blk_rows, blk_rows)
        return pltpu.make_async_copy(zbuf, xs_out.at[pl.ds(dst, blk_rows), :], zsem)

    @pl.when(step == 0)
    def _():
        zbuf[...] = jnp.zeros(zbuf.shape, F32)

        def zstart(i, carry):
            @pl.when(nv_ref[i] < bm)
            def _():
                zero_copy(i).start()
            return carry

        def zwait(i, carry):
            @pl.when(nv_ref[i] < bm)
            def _():
                zero_copy(i).wait()
            return carry

        lax.fori_loop(0, n_blocks, zstart, 0)
        lax.fori_loop(0, n_blocks, zwait, 0)

    def row_copy(t, k):
        src = pl.multiple_of(t * tiles, tiles)
        dst = pl.multiple_of(pos_ref[2 * t + k] * tiles, tiles)
        return pltpu.make_async_copy(x_ref.at[pl.ds(src, tiles), :], xs_out.at[pl.ds(dst, tiles), :], sem)

    def start(t, carry):
        row_copy(t, 0).start()
        row_copy(t, 1).start()
        return carry

    def wait(t, carry):
        row_copy(t, 0).wait()
        row_copy(t, 1).wait()
        return carry

    lax.fori_loop(0, tm, start, 0, unroll=DMA_UNROLL)
    lax.fori_loop(0, tm, wait, 0, unroll=DMA_UNROLL)


def _scatter(pos_flat, nvalid, x1t, n_blocks, tiles, bm):
    t = x1t.shape[0] // tiles
    tm = min(TOKEN_TILE, t)
    kern = functools.partial(_scatter_kernel, tm=tm, tiles=tiles, n_blocks=n_blocks, bm=bm)
    return pl.pallas_call(
        kern,
        grid=(t // tm,),
        in_specs=[
            pl.BlockSpec((2 * tm,), lambda i: (i,), memory_space=pltpu.SMEM),
            pl.BlockSpec(memory_space=pltpu.SMEM),
            pl.BlockSpec((tm * tiles, LANES), lambda i: (i, 0)),
        ],
        out_specs=pl.BlockSpec(memory_space=pl.ANY),
        out_shape=jax.ShapeDtypeStruct((n_blocks * bm * tiles, LANES), F32),
        scratch_shapes=[pltpu.VMEM((bm * tiles, LANES), F32),
                        pltpu.SemaphoreType.DMA(()), pltpu.SemaphoreType.DMA(())],
        compiler_params=_cparams(("arbitrary",)),
        name="moe_scatter",
    )(pos_flat, nvalid, x1t)


def _expert_kernel(be_ref, nv_ref, x_ref, wg_ref, wu_ref, wd_ref, y_ref, *, tiles):
    del be_ref
    n_valid = nv_ref[pl.program_id(0)]

    @pl.when(n_valid == 0)
    def _():
        y_ref[...] = jnp.zeros(y_ref.shape, F32)

    @pl.when(n_valid > 0)
    def _():
        xb = _load_token_tiles(x_ref, x_ref.shape[0] // tiles, tiles).astype(BF16)
        g = jnp.dot(xb, wg_ref[0], preferred_element_type=F32)
        u = jnp.dot(xb, wu_ref[0], preferred_element_type=F32)
        h = g * jax.nn.sigmoid(g) * u
        _store_token_tiles(y_ref, jnp.dot(h.astype(BF16), wd_ref[0], preferred_element_type=F32))


def _experts(block_expert, nvalid, xs, wg, wu, wd, bm):
    d, de = wg.shape[1], wg.shape[2]
    tiles = d // LANES
    blk_rows = bm * tiles
    nb = xs.shape[0] // blk_rows
    grid_spec = pltpu.PrefetchScalarGridSpec(
        num_scalar_prefetch=2,
        grid=(nb,),
        in_specs=[
            pl.BlockSpec((blk_rows, LANES), lambda i, be, nv: (i, 0)),
            pl.BlockSpec((1, d, de), lambda i, be, nv: (be[i], 0, 0)),
            pl.BlockSpec((1, d, de), lambda i, be, nv: (be[i], 0, 0)),
            pl.BlockSpec((1, de, d), lambda i, be, nv: (be[i], 0, 0)),
        ],
        out_specs=pl.BlockSpec((blk_rows, LANES), lambda i, be, nv: (i, 0)),
    )
    return pl.pallas_call(
        functools.partial(_expert_kernel, tiles=tiles),
        grid_spec=grid_spec,
        out_shape=jax.ShapeDtypeStruct(xs.shape, F32),
        compiler_params=_cparams(("arbitrary",)),
        name="moe_experts",
    )(block_expert, nvalid, xs, wg, wu, wd)


def _final_kernel(pos_ref, posn_ref, x1_ref, ri_ref, p_ref, ys_hbm, wpg_ref, wpp_ref, g2_ref, b2_ref,
                  pg_ref, out_ref, ybuf, sems, *, tm, tiles):
    step = pl.program_id(0)
    n_steps = pl.num_programs(0)
    slot = step % 2

    def row_copy(tab_ref, t, k, sl):
        src = pl.multiple_of(tab_ref[2 * t + k] * tiles, tiles)
        dst = pl.multiple_of(t * tiles, tiles)
        return pltpu.make_async_copy(ys_hbm.at[pl.ds(src, tiles), :], ybuf.at[sl, k, pl.ds(dst, tiles), :],
                                     sems.at[sl])

    def start_tile(tab_ref, sl):
        def body(t, carry):
            row_copy(tab_ref, t, 0, sl).start()
            row_copy(tab_ref, t, 1, sl).start()
            return carry

        lax.fori_loop(0, tm, body, 0, unroll=DMA_UNROLL)

    @pl.when(step == 0)
    def _():
        start_tile(pos_ref, 0)

    @pl.when(step + 1 < n_steps)
    def _():
        start_tile(posn_ref, 1 - slot)

    def wait(t, carry):
        row_copy(pos_ref, t, 0, slot).wait()
        row_copy(pos_ref, t, 1, slot).wait()
        return carry

    lax.fori_loop(0, tm, wait, 0, unroll=DMA_UNROLL)

    ri = ri_ref[...]
    y0 = _load_token_tiles(ybuf.at[slot, 0], tm, tiles)
    y1 = _load_token_tiles(ybuf.at[slot, 1], tm, tiles)
    ffn = ri[:, 2:3] * y0 + ri[:, 3:4] * y1
    x2 = _layer_norm(DN_ALPHA * x1_ref[...] + ffn, g2_ref[...], b2_ref[...])
    gate = jax.nn.sigmoid(jnp.dot(x2.astype(BF16), wpg_ref[...], preferred_element_type=F32))
    proj = jnp.dot(p_ref[...].astype(BF16), wpp_ref[...], preferred_element_type=F32)
    out_ref[...] = x2 + _rms_norm(gate * proj, pg_ref[...])


def _final(pos_flat, x1, ri, p, ys, wpg, wpp, g2, b2, pg):
    t, d = x1.shape
    tm = min(TOKEN_TILE, t)
    n_steps = t // tm
    tiles = d // LANES
    row = lambda w: pl.BlockSpec((tm, w), lambda i: (i, 0))
    consts = [wpg, wpp, g2, b2, pg]
    kern = functools.partial(_final_kernel, tm=tm, tiles=tiles)
    return pl.pallas_call(
        kern,
        grid=(n_steps,),
        in_specs=[pl.BlockSpec((2 * tm,), lambda i: (i,), memory_space=pltpu.SMEM),
                  pl.BlockSpec((2 * tm,), lambda i: (jnp.minimum(i + 1, n_steps - 1),), memory_space=pltpu.SMEM),
                  row(d), row(LANES), row(p.shape[1]), pl.BlockSpec(memory_space=pl.ANY)]
                 + [_full(a.shape) for a in consts],
        out_specs=row(d),
        out_shape=jax.ShapeDtypeStruct((t, d), F32),
        scratch_shapes=[pltpu.VMEM((2, 2, tm * tiles, LANES), F32), pltpu.SemaphoreType.DMA((2,))],
        compiler_params=_cparams(("arbitrary",)),
        name="combine_ln2_ple",
    )(pos_flat, pos_flat, x1, ri, p, ys, *consts)


def _lane_patterns(pair_layout):
    lane = np.arange(LANES)
    if pair_layout:
        valid = (lane < 2 * QK_ROPE).astype(np.float32)
    else:
        valid = np.ones(LANES, np.float32)
    sign = np.where((lane % QK_ROPE) < HALF_ROPE, -1.0, 1.0).astype(np.float32) * valid
    return lane % HALF_ROPE, sign, valid


def _prep_weights(w_in, w_uq, w_uk, w_uv, w_group, w_router):
    d = w_in.shape[0]
    ql = w_uq.shape[0]
    kvl, n_heads, _ = w_uk.shape
    o_kv, o_kr = ql, ql + kvl
    o_conv = o_kr + QK_ROPE
    c_conv = (w_in.shape[1] - o_conv - 2 * d) // 2
    o_ga = o_conv + 2 * c_conv
    o_gb = o_ga + d
    kr = w_in[:, o_kr:o_conv]
    kr_sw = jnp.concatenate([kr[:, HALF_ROPE:], kr[:, :HALF_ROPE]], axis=1)
    zpad = jnp.zeros((d, LANES - 2 * QK_ROPE), w_in.dtype)
    win_r = jnp.concatenate([
        w_in[:, :o_kv], w_in[:, o_kv:o_kr], w_in[:, o_conv:o_ga], w_in[:, o_ga:o_gb], w_in[:, o_gb:],
        kr, kr, zpad, kr_sw, kr_sw, zpad], axis=1).astype(BF16)
    assert win_r.shape[1] == C_END and c_conv == C_CB - C_CA and kvl == C_CA - C_KV and ql == C_KV

    hd = QK_NOPE + QK_ROPE
    wq = w_uq.reshape(ql, n_heads, hd)
    nope = wq[:, :, :QK_NOPE].reshape(ql, n_heads * QK_NOPE)
    x1 = wq[:, :, QK_NOPE:QK_NOPE + HALF_ROPE]
    x2 = wq[:, :, QK_NOPE + HALF_ROPE:]
    rope_a = jnp.concatenate([x1, x2], axis=2)
    rope_b = jnp.concatenate([x2, x1], axis=2)
    wuq_s = jnp.concatenate([nope, rope_a.reshape(ql, -1), rope_b.reshape(ql, -1)], axis=1).astype(BF16)

    def pair_layout(r):
        r = r.reshape(ql, n_heads // 2, 2 * QK_ROPE)
        z = jnp.zeros((ql, n_heads // 2, LANES - 2 * QK_ROPE), r.dtype)
        return jnp.concatenate([r, z], axis=2).reshape(ql, -1)

    wuq_p = jnp.concatenate([nope, pair_layout(rope_a), pair_layout(rope_b)], axis=1).astype(BF16)

    wuk_flat = w_uk.reshape(kvl, n_heads * QK_NOPE)
    wuv_flat = w_uv.reshape(kvl, n_heads * V_HEAD)
    wukv = jnp.concatenate([wuk_flat, wuv_flat], axis=1).astype(BF16)
    eye = jnp.eye(n_heads, dtype=w_uk.dtype)
    wk_bd = jnp.einsum('lhn,hg->hngl', w_uk, eye).reshape(n_heads * QK_NOPE, n_heads * kvl).astype(BF16)
    wv_bd = jnp.einsum('lhv,hg->hlgv', w_uv, eye).reshape(n_heads * kvl, n_heads * V_HEAD).astype(BF16)

    n_exp = w_router.shape[1]
    wgr = jnp.concatenate([w_group, w_router,
                           jnp.zeros((d, LANES - N_GROUPS - n_exp), w_group.dtype)], axis=1)
    wgh = wgr.astype(BF16)
    wgl = (wgr - wgh.astype(F32)).astype(BF16)
    return win_r, wuq_p, wuq_s, wukv, wk_bd, wv_bd, wgh, wgl


def _moe(x1t, ri, wg, wu, wd, n_exp):
    t = ri.shape[0]
    tiles = x1t.shape[0] // t
    n_assign = 2 * t
    bm = MOE_BLOCK if n_assign >= 2 * MOE_BLOCK * n_exp else MOE_BLOCK_MIN
    n_blocks = (n_assign + n_exp * (bm - 1) + bm - 1) // bm
    pos, be = _route(ri, n_blocks=n_blocks, n_exp=n_exp, bm=bm)
    pos_flat = pos[:, :2].reshape(-1)
    block_expert = be[:n_blocks, 0]
    nvalid = be[:n_blocks, 1]
    xs = _scatter(pos_flat, nvalid, x1t, n_blocks, tiles, bm)
    ys = _experts(block_expert, nvalid, xs, wg, wu, wd, bm)
    return pos_flat, ys


def kernel(x_prompt, x_sample, cache_ckv, cache_krope, state_conv, page_table, p_prompt, p_sample, w_in, q_norm_g, w_uq, kv_norm_g, w_uk, w_uv, w_o_attn, w_dw, b_dw, conv_ln_g, conv_ln_b, w_pw2, w_out, ln1_g, ln1_b, w_group, w_router, w_gate, w_up, w_down, ln2_g, ln2_b, w_ple_gate, w_ple_proj, ple_norm_g):
    assert w_in.shape[0] == DEPTH
    b, s, d = x_prompt.shape
    bd, sd, _ = x_sample.shape
    assert sd == 1
    n_pages = page_table.shape[1]
    page_rows = cache_ckv.shape[2]
    past = n_pages * page_rows
    n_heads = w_uk.shape[2]
    kvl = w_uk.shape[1]
    width = w_dw.shape[1]
    n_exp = w_router.shape[2]
    c_conv = w_dw.shape[2]

    win_r, wuq_p, wuq_s, wukv, wk_bd, wv_bd, wgh, wgl = _prep_weights(
        w_in[0], w_uq[0], w_uk[0], w_uv[0], w_group[0], w_router[0])
    qg, kvg = q_norm_g, kv_norm_g
    wo, wpw, wout = w_o_attn[0].astype(BF16), w_pw2[0].astype(BF16), w_out[0].astype(BF16)
    wgate, wup, wdown = w_gate[0].astype(BF16), w_up[0].astype(BF16), w_down[0].astype(BF16)
    wpg, wpp = w_ple_gate[0].astype(BF16), w_ple_proj[0].astype(BF16)
    w_dw_pad = jnp.concatenate([w_dw[0], jnp.zeros((CONV_HALO - width, c_conv), F32)], axis=0)

    inv16 = 1.0 / (ROPE_THETA ** (jnp.arange(HALF_ROPE, dtype=F32) / HALF_ROPE))

    def tables(pair_layout, n_rows, offset, step):
        idx, sign, valid = _lane_patterns(pair_layout)
        inv_l = (inv16[idx] * valid)[None, :]
        return _rope_tables(n_rows, offset, step, inv_l, jnp.asarray(sign)[None, :], jnp.asarray(valid)[None, :])

    def trunk_tail(x2d, o, y_or_u, sa, sb, p2d, conv=None):
        x1, x1t, ri = _mix(x2d, o, y_or_u, sa, sb, wo, wpw, wout, conv_ln_g, conv_ln_b, ln1_g, ln1_b,
                           wgh, wgl, conv=conv)
        pos_flat, ys = _moe(x1t, ri, wgate, wup, wdown, n_exp)
        return _final(pos_flat, x1, ri, p2d, ys, wpg, wpp, ln2_g, ln2_b, ple_norm_g)

    xs_ = x_sample.reshape(bd, d)
    cos_s, sin_s = tables(False, SUBLANES, past, 0)
    qn_s, qr_s, ckv_s, kr_s, u_s, sa_s, sb_s = _inproj(
        xs_, win_r, qg, kvg, wuq_s, None, cos_s, sin_s, n_rep=n_heads * QK_ROPE // LANES, seq=1, make_kv=False)
    q_lat = _dense(qn_s, wk_bd, F32).reshape(bd, n_heads, kvl)
    o_lat = _sample_attention(page_table, q_lat, qr_s.reshape(bd, n_heads, QK_ROPE),
                              ckv_s.reshape(bd, 1, kvl), kr_s.reshape(bd, 1, QK_ROPE),
                              cache_ckv[0], jnp.swapaxes(cache_krope[0], 1, 2))
    o_s = _dense(o_lat.reshape(bd, n_heads * kvl), wv_bd, BF16)
    ext_s = jnp.concatenate([state_conv[0], u_s[:, None, :]], axis=1)
    y_s = _conv_sample(jnp.transpose(ext_s, (1, 0, 2)), w_dw_pad, b_dw, width=width)
    out_s = trunk_tail(xs_, o_s, y_s, sa_s, sb_s, p_sample[0].reshape(bd, -1))
    new_conv_s = ext_s[:, 1:, :]

    xp = x_prompt.reshape(b * s, d)
    cos_p, sin_p = tables(True, s, 0, 1)
    qn, qr, kn, v, krd, ckv_p, kr_p, u_p, sa_p, sb_p = _inproj(
        xp, win_r, qg, kvg, wuq_p, wukv, cos_p, sin_p, n_rep=n_heads // 2, seq=s, make_kv=True)
    o_p = _prompt_attention(qn, qr, kn, krd, v, batch=b, seq=s)
    out_p = trunk_tail(xp, o_p, u_p, sa_p, sb_p, p_prompt[0].reshape(b * s, -1),
                       conv=(w_dw_pad, b_dw, s, width))
    u_p3 = u_p.reshape(b, s, c_conv)
    new_conv_p = u_p3[:, s - (width - 1):, :]

    return (out_p.reshape(b, s, d), out_s.reshape(bd, 1, d),
            ckv_p.reshape(1, b, s, kvl), kr_p.reshape(1, b, s, QK_ROPE), new_conv_p[None],
            ckv_s.reshape(1, bd, 1, kvl), kr_s.reshape(1, bd, 1, QK_ROPE), new_conv_s[None])
```

```python
import functools

import numpy as np
import jax
import jax.numpy as jnp
from jax import lax
from jax.experimental import pallas as pl
from jax.experimental.pallas import tpu as pltpu

F32 = jnp.float32
BF16 = jnp.bfloat16

QK_NOPE = 64
QK_ROPE = 32
HALF_ROPE = QK_ROPE // 2
V_HEAD = 64
ROPE_THETA = 10000.0
ATTN_SCALE = (QK_NOPE + QK_ROPE) ** -0.5
LOG2E = 1.4426950408889634
N_GROUPS = 4
EXPERTS_PER_GROUP = 8
DEPTH = 1
DN_ALPHA = (2 * DEPTH) ** 0.25
LN_EPS = 1e-5
RMS_EPS = 1e-6
NEG_INF = -1e30

LANES = 128
SUBLANES = 8
VMEM_LIMIT = 56 * 1024 * 1024

TOKEN_TILE = 512
ATTN_TILE = 512
CONV_CHUNK = 64
CONV_HALO = 32
MOE_BLOCK = 512
MOE_BLOCK_MIN = 128
PAGES_PER_CHUNK = 16
SEQS_PER_STEP = 4
DMA_UNROLL = 8


def _cparams(sem, vmem=VMEM_LIMIT):
    return pltpu.CompilerParams(dimension_semantics=sem, vmem_limit_bytes=vmem)


def _full(shape):
    n = len(shape)
    return pl.BlockSpec(shape, lambda *_: (0,) * n)


def _layer_norm(x, g, b):
    mu = jnp.mean(x, axis=-1, keepdims=True)
    xc = x - mu
    var = jnp.mean(xc * xc, axis=-1, keepdims=True)
    return xc * lax.rsqrt(var + LN_EPS) * g + b


def _rms_norm(x, g):
    return x * lax.rsqrt(jnp.mean(x * x, axis=-1, keepdims=True) + RMS_EPS) * g


def _rope_table_kernel(inv_ref, sign_ref, valid_ref, cos_ref, sin_ref, *, offset, step):
    rows = cos_ref.shape[0]
    r = lax.broadcasted_iota(jnp.int32, (rows, LANES), 0) + pl.program_id(0) * rows
    pos = (offset + step * r).astype(F32)
    ang = pos * inv_ref[...]
    cos_ref[...] = jnp.cos(ang) * valid_ref[...]
    sin_ref[...] = jnp.sin(ang) * sign_ref[...]


def _rope_tables(n_rows, offset, step, inv_l, sign_l, valid_l):
    tr = min(n_rows, 512)
    kern = functools.partial(_rope_table_kernel, offset=offset, step=step)
    return pl.pallas_call(
        kern,
        grid=(n_rows // tr,),
        in_specs=[_full((1, LANES))] * 3,
        out_specs=[pl.BlockSpec((tr, LANES), lambda i: (i, 0))] * 2,
        out_shape=[jax.ShapeDtypeStruct((n_rows, LANES), F32)] * 2,
        compiler_params=_cparams(("arbitrary",)),
        name="rope_tables",
    )(inv_l, sign_l, valid_l)


C_Q, C_KV, C_CA, C_CB, C_GA, C_GB, C_KA, C_KB, C_END = 0, 384, 640, 1152, 1664, 2688, 3712, 3840, 3968


def _inproj_kernel(x_ref, win_ref, qg_ref, kvg_ref, wuq_ref, *rest, n_rep, make_kv, bcast, q_scale):
    if make_kv:
        (wukv_ref, cos_ref, sin_ref, qn_o, qr_o, kn_o, v_o, krd_o, ckv_o, kr_o, u_o, sa_o, sb_o) = rest
    else:
        (cos_ref, sin_ref, qn_o, qr_o, ckv_o, kr_o, u_o, sa_o, sb_o) = rest
    xb = x_ref[...].astype(BF16)

    def proj(a, b):
        return jnp.dot(xb, win_ref[:, a:b], preferred_element_type=F32)

    if bcast:
        c1, s1 = cos_ref[0:1, :], sin_ref[0:1, :]
    else:
        c1, s1 = cos_ref[...], sin_ref[...]
    cn = jnp.concatenate([c1] * n_rep, axis=1)
    sn = jnp.concatenate([s1] * n_rep, axis=1)
    d_nope = qn_o.shape[1]
    nr = LANES * n_rep

    cqn = _rms_norm(proj(C_Q, C_KV), qg_ref[...])
    qall = jnp.dot(cqn.astype(BF16), wuq_ref[...], preferred_element_type=F32)
    qn_o[...] = (qall[:, :d_nope] * q_scale).astype(qn_o.dtype)
    qa = qall[:, d_nope:d_nope + nr]
    qb = qall[:, d_nope + nr:d_nope + 2 * nr]
    qr_o[...] = ((qa * cn + qb * sn) * q_scale).astype(qr_o.dtype)

    ckvn = _rms_norm(proj(C_KV, C_CA), kvg_ref[...])
    ckv_o[...] = ckvn
    krot = proj(C_KA, C_KB) * c1 + proj(C_KB, C_END) * s1
    kr_o[...] = krot[:, :QK_ROPE]
    if make_kv:
        kv = jnp.dot(ckvn.astype(BF16), wukv_ref[...], preferred_element_type=F32)
        half = kv.shape[1] // 2
        kn_o[...] = kv[:, :half].astype(BF16)
        v_o[...] = kv[:, half:].astype(BF16)
        krd_o[...] = krot.astype(BF16)

    u_o[...] = proj(C_CA, C_CB) * jax.nn.sigmoid(proj(C_CB, C_GA))
    sa_o[...] = jax.nn.sigmoid(proj(C_GA, C_GB)).astype(BF16)
    sb_o[...] = jax.nn.sigmoid(proj(C_GB, C_KA)).astype(BF16)


def _inproj(x, win_r, qg, kvg, wuq_r, wukv, cos_t, sin_t, *, n_rep, seq, make_kv):
    t, d = x.shape
    tm = min(TOKEN_TILE, t)
    d_nope = wuq_r.shape[1] - 2 * LANES * n_rep
    nr = LANES * n_rep
    c_conv = C_CB - C_CA
    kvl = C_CA - C_KV
    bcast = seq < tm
    if bcast:
        tab_spec = _full(cos_t.shape)
    else:
        nst = seq // tm
        tab_spec = pl.BlockSpec((tm, LANES), lambda i: (i % nst, 0))
    row = lambda w: pl.BlockSpec((tm, w), lambda i: (i, 0))
    in_specs = [row(d), _full(win_r.shape), _full(qg.shape), _full(kvg.shape), _full(wuq_r.shape)]
    args = [x, win_r, qg, kvg, wuq_r]
    if make_kv:
        in_specs.append(_full(wukv.shape))
        args.append(wukv)
    in_specs += [tab_spec, tab_spec]
    args += [cos_t, sin_t]
    q_dt = BF16 if make_kv else F32
    outs = [(d_nope, BF16), (nr, q_dt)]
    if make_kv:
        outs += [(d_nope, BF16), (d_nope, BF16), (LANES, BF16)]
    outs += [(kvl, F32), (QK_ROPE, F32), (c_conv, F32), (d, BF16), (d, BF16)]
    q_scale = ATTN_SCALE * LOG2E if make_kv else ATTN_SCALE
    kern = functools.partial(_inproj_kernel, n_rep=n_rep, make_kv=make_kv, bcast=bcast, q_scale=q_scale)
    return pl.pallas_call(
        kern,
        grid=(t // tm,),
        in_specs=in_specs,
        out_specs=[row(w) for w, _ in outs],
        out_shape=[jax.ShapeDtypeStruct((t, w), dt) for w, dt in outs],
        compiler_params=_cparams(("arbitrary",)),
        name="inproj_kv" if make_kv else "inproj_q",
    )(*args)


def _online_softmax(state, s, v):
    m, l, acc = state
    m_new = jnp.maximum(m, jnp.max(s, axis=1, keepdims=True))
    alpha = jnp.exp2(m - m_new)
    p = jnp.exp2(s - jnp.concatenate([m_new] * (s.shape[1] // LANES), axis=1))
    l = alpha * l + jnp.sum(p, axis=1, keepdims=True)
    acc = alpha * acc + jnp.dot(p.astype(BF16), v, preferred_element_type=F32)
    return m_new, l, acc


def _pattn_kernel(qn_ref, qr_ref, kn_ref, krd_ref, v_ref, o_ref, m_s, l_s, acc_s, *, blk, nq):
    half = blk // 2
    lane = lax.broadcasted_iota(jnp.int32, (blk, LANES), 1)
    row_a = lax.broadcasted_iota(jnp.int32, (blk, half), 0)
    col_a = lax.broadcasted_iota(jnp.int32, (blk, half), 1)
    row_b = lax.broadcasted_iota(jnp.int32, (half, half), 0)
    col_b = lax.broadcasted_iota(jnp.int32, (half, half), 1)
    dn_t = (((1,), (1,)), ((), ()))
    for i in range(nq):
        qn = qn_ref[i * blk:(i + 1) * blk, :]
        qr = qr_ref[i * blk:(i + 1) * blk, :]
        zero = jnp.zeros_like(qn)
        q_heads = (
            jnp.concatenate([jnp.where(lane < QK_NOPE, qn, zero), jnp.where(lane < QK_ROPE, qr, zero)], axis=1),
            jnp.concatenate([jnp.where(lane >= QK_NOPE, qn, zero),
                             jnp.where((lane >= QK_ROPE) & (lane < 2 * QK_ROPE), qr, zero)], axis=1),
        )
        state = [(jnp.full((blk, LANES), NEG_INF, F32), jnp.zeros((blk, LANES), F32),
                  jnp.zeros((blk, LANES), F32))] * 2
        for j in range(i):
            k = jnp.concatenate([kn_ref[j * blk:(j + 1) * blk, :], krd_ref[j * blk:(j + 1) * blk, :]], axis=1)
            v = v_ref[j * blk:(j + 1) * blk, :]
            for h in range(2):
                s = lax.dot_general(q_heads[h], k, dn_t, preferred_element_type=F32)
                state[h] = _online_softmax(state[h], s, v)
        k0 = i * blk
        for h in range(2):
            k_a = jnp.concatenate([kn_ref[k0:k0 + half, :], krd_ref[k0:k0 + half, :]], axis=1)
            s_a = lax.dot_general(q_heads[h], k_a, dn_t, preferred_element_type=F32)
            s_a = jnp.where(col_a <= row_a, s_a, NEG_INF)
            m, l, acc = _online_softmax(state[h], s_a, v_ref[k0:k0 + half, :])
            m_s[i, h], l_s[i, h], acc_s[i, h] = m, l, acc
            k_b = jnp.concatenate([kn_ref[k0 + half:k0 + blk, :], krd_ref[k0 + half:k0 + blk, :]], axis=1)
            s_b = lax.dot_general(q_heads[h][half:], k_b, dn_t, preferred_element_type=F32)
            s_b = jnp.where(col_b <= row_b, s_b, NEG_INF)
            _, l_b, acc_b = _online_softmax(
                (m_s[i, h, half:, :], l_s[i, h, half:, :], acc_s[i, h, half:, :]), s_b,
                v_ref[k0 + half:k0 + blk, :])
            l_s[i, h, half:, :] = l_b
            acc_s[i, h, half:, :] = acc_b
        o = jnp.where(lane < V_HEAD, acc_s[i, 0] / l_s[i, 0], acc_s[i, 1] / l_s[i, 1])
        o_ref[i * blk:(i + 1) * blk, :] = o.astype(o_ref.dtype)


def _prompt_attention(qn, qr, kn, krd, v, *, batch, seq):
    t, d = qn.shape
    blk = min(ATTN_TILE, seq)
    nq = seq // blk
    n_pairs = d // LANES
    kspec = pl.BlockSpec((seq, LANES), lambda b, j: (b, j))
    kern = functools.partial(_pattn_kernel, blk=blk, nq=nq)
    return pl.pallas_call(
        kern,
        grid=(batch, n_pairs),
        in_specs=[kspec, kspec, kspec, pl.BlockSpec((seq, LANES), lambda b, j: (b, 0)), kspec],
        out_specs=kspec,
        out_shape=jax.ShapeDtypeStruct((t, d), BF16),
        scratch_shapes=[pltpu.VMEM((nq, 2, blk, LANES), F32)] * 3,
        compiler_params=_cparams(("arbitrary",) * 2),
        name="prompt_attention",
    )(qn, qr, kn, krd, v)


def _dense_kernel(a_ref, w_ref, o_ref):
    o_ref[...] = jnp.dot(a_ref[...].astype(BF16), w_ref[...], preferred_element_type=F32).astype(o_ref.dtype)


def _dense(a, w, out_dtype):
    m, n = a.shape[0], w.shape[1]
    return pl.pallas_call(
        _dense_kernel,
        grid=(1,),
        in_specs=[_full(a.shape), _full(w.shape)],
        out_specs=_full((m, n)),
        out_shape=jax.ShapeDtypeStruct((m, n), out_dtype),
        compiler_params=_cparams(("arbitrary",)),
        name="dense",
    )(a, w)


def _sattn_kernel(pt_ref, ptn_ref, ql_ref, qr_ref, cnew_ref, knew_ref, ckv_hbm, krt_hbm, o_ref,
                  cbuf, kbuf, sems, *, n_chunks, ppc, nb):
    step = pl.program_id(0)
    n_steps = pl.num_programs(0)
    kvl = cbuf.shape[4]
    kc = ppc * cbuf.shape[3]
    n_heads = ql_ref.shape[1]
    dn_t = (((1,), (1,)), ((), ()))

    def chunk_copies(tab_ref, chunk, slot):
        out = []
        for b in range(nb):
            for pg in range(ppc):
                page = tab_ref[b, 0, chunk * ppc + pg]
                out.append(pltpu.make_async_copy(ckv_hbm.at[page], cbuf.at[slot, b, pg], sems.at[0, slot]))
                out.append(pltpu.make_async_copy(krt_hbm.at[page], kbuf.at[slot, b, pg], sems.at[1, slot]))
        return out

    def start_chunk(tab_ref, chunk, slot):
        for cp in chunk_copies(tab_ref, chunk, slot):
            cp.start()

    @pl.when(step == 0)
    def _():
        start_chunk(pt_ref, 0, 0)

    first = step * n_chunks
    qls = [ql_ref[b].astype(BF16) for b in range(nb)]
    qrs = [qr_ref[b].astype(BF16) for b in range(nb)]

    def body(c, carry):
        slot = (first + c) % 2

        @pl.when(c + 1 < n_chunks)
        def _():
            start_chunk(pt_ref, c + 1, 1 - slot)

        @pl.when((c + 1 == n_chunks) & (step + 1 < n_steps))
        def _():
            start_chunk(ptn_ref, 0, 1 - slot)

        for cp in chunk_copies(pt_ref, c, slot):
            cp.wait()
        new = []
        for b in range(nb):
            m, l, acc = carry[3 * b:3 * b + 3]
            ck = cbuf[slot, b].reshape(kc, kvl).astype(BF16)
            krt = jnp.concatenate([kbuf[slot, b, pg] for pg in range(ppc)], axis=1).astype(BF16)
            s = lax.dot_general(qls[b], ck, dn_t, preferred_element_type=F32)
            s = s + jnp.dot(qrs[b], krt, preferred_element_type=F32)
            m_new = jnp.maximum(m, jnp.max(s, axis=1, keepdims=True))
            alpha = jnp.exp(m - m_new)
            p = jnp.exp(s - m_new)
            l = alpha * l + jnp.sum(p, axis=1, keepdims=True)
            acc = alpha * acc + jnp.dot(p.astype(BF16), ck, preferred_element_type=F32)
            new += [m_new, l, acc]
        return tuple(new)

    init = (jnp.full((n_heads, 1), NEG_INF, F32), jnp.zeros((n_heads, 1), F32),
            jnp.zeros((n_heads, kvl), F32)) * nb
    carry = lax.fori_loop(0, n_chunks, body, init)

    for b in range(nb):
        m, l, acc = carry[3 * b:3 * b + 3]
        cnew = cnew_ref[b]
        knew = knew_ref[b]
        s_new = (jnp.sum(ql_ref[b] * cnew, axis=1, keepdims=True)
                 + jnp.sum(qr_ref[b] * knew, axis=1, keepdims=True))
        m_new = jnp.maximum(m, s_new)
        alpha = jnp.exp(m - m_new)
        p_new = jnp.exp(s_new - m_new)
        l = alpha * l + p_new
        acc = alpha * acc + p_new * cnew
        o_ref[b] = acc / l


def _sample_attention(page_table, q_lat, q_rope, c_new, k_new, cache_ckv, cache_krope_t):
    bd, n_heads, kvl = q_lat.shape
    n_pages = page_table.shape[1]
    page_rows = cache_ckv.shape[1]
    ppc = min(PAGES_PER_CHUNK, n_pages)
    n_chunks = n_pages // ppc
    nb = min(SEQS_PER_STEP, bd)
    n_steps = bd // nb
    pt3 = page_table.reshape(bd, 1, n_pages)
    per_b = lambda s: pl.BlockSpec((nb,) + s, lambda i: (i, 0, 0))
    kern = functools.partial(_sattn_kernel, n_chunks=n_chunks, ppc=ppc, nb=nb)
    return pl.pallas_call(
        kern,
        grid=(n_steps,),
        in_specs=[
            pl.BlockSpec((nb, 1, n_pages), lambda i: (i, 0, 0), memory_space=pltpu.SMEM),
            pl.BlockSpec((nb, 1, n_pages), lambda i: (jnp.minimum(i + 1, n_steps - 1), 0, 0),
                         memory_space=pltpu.SMEM),
            per_b((n_heads, kvl)), per_b((n_heads, QK_ROPE)), per_b((1, kvl)), per_b((1, QK_ROPE)),
            pl.BlockSpec(memory_space=pl.ANY), pl.BlockSpec(memory_space=pl.ANY),
        ],
        out_specs=per_b((n_heads, kvl)),
        out_shape=jax.ShapeDtypeStruct((bd, n_heads, kvl), F32),
        scratch_shapes=[
            pltpu.VMEM((2, nb, ppc, page_rows, kvl), F32),
            pltpu.VMEM((2, nb, ppc, QK_ROPE, page_rows), F32),
            pltpu.SemaphoreType.DMA((2, 2)),
        ],
        compiler_params=_cparams(("arbitrary",)),
        name="sample_attention",
    )(pt3, pt3, q_lat, q_rope, c_new, k_new, cache_ckv, cache_krope_t)


def _conv_rows(u_ref, halo_ref, w_ref, b_ref, y_ref, ext_s, z_s, first_tile, width):
    ts, ch = u_ref.shape
    left = jnp.where(first_tile, jnp.zeros_like(halo_ref), halo_ref[...])
    ext_s[0:CONV_HALO, :] = left
    ext_s[CONV_HALO:CONV_HALO + ts, :] = u_ref[...]
    ext_s[CONV_HALO + ts:CONV_HALO + ts + SUBLANES, :] = jnp.zeros((SUBLANES, ch), F32)
    off = CONV_HALO - (width - 1)
    win = CONV_CHUNK + SUBLANES
    for c in range(ts // CONV_CHUNK):
        r0 = c * CONV_CHUNK
        acc = jnp.broadcast_to(b_ref[...], (CONV_CHUNK, ch))
        for r in range(SUBLANES):
            z = None
            for a in range((off + width + SUBLANES - 1) // SUBLANES):
                k = SUBLANES * a + r - off
                if 0 <= k < width:
                    term = w_ref[k:k + 1, :] * ext_s[r0 + SUBLANES * a:r0 + SUBLANES * a + win, :]
                    z = term if z is None else z + term
            if r == 0:
                acc = acc + z[:CONV_CHUNK]
            else:
                z_s[r] = z
                acc = acc + z_s[r, r:r + CONV_CHUNK, :]
        y_ref[r0:r0 + CONV_CHUNK, :] = acc


def _conv_sample_kernel(ext_ref, w_ref, b_ref, y_ref, *, width):
    acc = jnp.broadcast_to(b_ref[...], y_ref.shape)
    for k in range(width):
        acc = acc + w_ref[k:k + 1, :] * ext_ref[k]
    y_ref[...] = acc


def _conv_sample(ext_t, w_pad, b, *, width):
    _, bd, c = ext_t.shape
    kern = functools.partial(_conv_sample_kernel, width=width)
    return pl.pallas_call(
        kern,
        grid=(1,),
        in_specs=[_full(ext_t.shape), _full(w_pad.shape), _full(b.shape)],
        out_specs=_full((bd, c)),
        out_shape=jax.ShapeDtypeStruct((bd, c), F32),
        compiler_params=_cparams(("arbitrary",)),
        name="conv_sample",
    )(ext_t, w_pad, b)


def _store_token_tiles(ref, x):
    rows, d = x.shape
    tiles = d // LANES
    for c in range(tiles):
        ref[pl.ds(c, rows, stride=tiles), :] = x[:, c * LANES:(c + 1) * LANES]


def _load_token_tiles(ref, rows, tiles):
    return jnp.concatenate([ref[pl.ds(c, rows, stride=tiles), :] for c in range(tiles)], axis=1)


def _mix_kernel(x_ref, o_ref, *rest, conv_tiles, width):
    if conv_tiles:
        (u_ref, halo_ref, wdw_ref, bdw_ref, sa_ref, sb_ref, wo_ref, wpw_ref, wout_ref, cg_ref, cb_ref,
         g1_ref, b1_ref, wgh_ref, wgl_ref, x1_ref, x1t_ref, ri_ref, ext_s, z_s, y_ref) = rest
        first_tile = pl.program_id(0) % conv_tiles == 0
        _conv_rows(u_ref, halo_ref, wdw_ref, bdw_ref, y_ref, ext_s, z_s, first_tile, width)
    else:
        (y_ref, sa_ref, sb_ref, wo_ref, wpw_ref, wout_ref, cg_ref, cb_ref,
         g1_ref, b1_ref, wgh_ref, wgl_ref, x1_ref, x1t_ref, ri_ref) = rest
    branch_b = jnp.dot(o_ref[...], wo_ref[...], preferred_element_type=F32)
    z = _layer_norm(y_ref[...], cg_ref[...], cb_ref[...])
    z = z * jax.nn.sigmoid(z)
    branch_a = jnp.dot(z.astype(BF16), wpw_ref[...], preferred_element_type=F32)
    mixin = sa_ref[...].astype(F32) * branch_a + sb_ref[...].astype(F32) * branch_b
    mix = jnp.dot(mixin.astype(BF16), wout_ref[...], preferred_element_type=F32)
    x1 = _layer_norm(DN_ALPHA * x_ref[...] + mix, g1_ref[...], b1_ref[...])
    x1_ref[...] = x1
    _store_token_tiles(x1t_ref, x1)

    x_hi = x1.astype(BF16)
    x_lo = (x1 - x_hi.astype(F32)).astype(BF16)
    lg = (jnp.dot(x_hi, wgh_ref[...], preferred_element_type=F32)
          + jnp.dot(x_lo, wgh_ref[...], preferred_element_type=F32)
          + jnp.dot(x_hi, wgl_ref[...], preferred_element_type=F32))
    n_exp = N_GROUPS * EXPERTS_PER_GROUP
    lane = lax.broadcasted_iota(jnp.int32, lg.shape, 1)
    lane_f = lane.astype(F32)
    big = float(LANES)
    gmask = lane < N_GROUPS
    lgm = jnp.where(gmask, lg, NEG_INF)
    gmax = jnp.max(lgm, axis=1, keepdims=True)
    gidx = jnp.min(jnp.where(lgm == gmax, lane_f, big), axis=1, keepdims=True)
    pg_sel = 1.0 / jnp.sum(jnp.where(gmask, jnp.exp(lgm - gmax), 0.0), axis=1, keepdims=True)
    egroup = jnp.floor((lane_f - N_GROUPS) * (1.0 / EXPERTS_PER_GROUP))
    emask = (lane >= N_GROUPS) & (lane < N_GROUPS + n_exp) & (egroup == gidx)
    le = jnp.where(emask, lg, NEG_INF)
    v1 = jnp.max(le, axis=1, keepdims=True)
    i1 = jnp.min(jnp.where(le == v1, lane_f, big), axis=1, keepdims=True)
    le2 = jnp.where(lane_f == i1, NEG_INF, le)
    v2 = jnp.max(le2, axis=1, keepdims=True)
    i2 = jnp.min(jnp.where(le2 == v2, lane_f, big), axis=1, keepdims=True)
    e = jnp.exp(v2 - v1)
    gate1 = pg_sel / (1.0 + e)
    gate2 = pg_sel * e / (1.0 + e)
    ri = jnp.where(lane == 0, i1 - N_GROUPS,
                   jnp.where(lane == 1, i2 - N_GROUPS,
                             jnp.where(lane == 2, gate1, jnp.where(lane == 3, gate2, 0.0))))
    ri_ref[...] = ri


def _mix(x, o, y_or_u, sa, sb, wo, wpw, wout, cg, cb, g1, b1, wgh, wgl, conv=None):
    t, d = x.shape
    tm = min(TOKEN_TILE, t)
    c = y_or_u.shape[1]
    row = lambda w: pl.BlockSpec((tm, w), lambda i: (i, 0))
    consts = [wo, wpw, wout, cg, cb, g1, b1, wgh, wgl]
    tiles = d // LANES
    if conv is None:
        kern = functools.partial(_mix_kernel, conv_tiles=0, width=0)
        lead_specs, lead, scratch = [row(c)], [y_or_u], []
    else:
        w_pad, b_dw, seq, width = conv
        assert seq % tm == 0
        hb = tm // CONV_HALO
        kern = functools.partial(_mix_kernel, conv_tiles=seq // tm, width=width)
        lead_specs = [row(c), pl.BlockSpec((CONV_HALO, c), lambda i: (jnp.maximum(i * hb - 1, 0), 0)),
                      _full(w_pad.shape), _full(b_dw.shape)]
        lead = [y_or_u, y_or_u, w_pad, b_dw]
        scratch = [pltpu.VMEM((CONV_HALO + tm + SUBLANES, c), F32),
                   pltpu.VMEM((SUBLANES, CONV_CHUNK + SUBLANES, c), F32), pltpu.VMEM((tm, c), F32)]
    return pl.pallas_call(
        kern,
        grid=(t // tm,),
        in_specs=[row(d), row(d)] + lead_specs + [row(d), row(d)] + [_full(a.shape) for a in consts],
        out_specs=[row(d), pl.BlockSpec((tm * tiles, LANES), lambda i: (i, 0)), row(LANES)],
        out_shape=[jax.ShapeDtypeStruct((t, d), F32), jax.ShapeDtypeStruct((t * tiles, LANES), F32),
                   jax.ShapeDtypeStruct((t, LANES), F32)],
        scratch_shapes=scratch,
        compiler_params=_cparams(("arbitrary",)),
        name="mix_ln1_router",
    )(x, o, *lead, sa, sb, *consts)


def _route_kernel(ri_ref, pos_ref, be_ref, cnt_s, base_s, *, tm, bm, n_exp):
    ph = pl.program_id(0)
    i = pl.program_id(1)
    lane_f = lax.broadcasted_iota(jnp.int32, (tm, LANES), 1).astype(F32)
    ri = ri_ref[...]
    oh0 = lane_f == ri[:, 0:1]
    oh1 = lane_f == ri[:, 1:2]
    c = jnp.where(oh0 | oh1, 1.0, 0.0)
    csum = jnp.sum(c, axis=0, keepdims=True)

    @pl.when((ph == 0) & (i == 0))
    def _():
        cnt_s[...] = jnp.zeros(cnt_s.shape, F32)

    @pl.when(ph == 0)
    def _():
        cnt_s[0:1, :] = cnt_s[0:1, :] + csum

    @pl.when((ph == 1) & (i == 0))
    def _():
        cnt = cnt_s[...]
        pc = jnp.floor((cnt + (bm - 1)) * (1.0 / bm)) * bm
        r = lax.broadcasted_iota(jnp.int32, (LANES, LANES), 0)
        cc = lax.broadcasted_iota(jnp.int32, (LANES, LANES), 1)
        upper = jnp.where(r <= cc, 1.0, 0.0)
        pend = jnp.dot(pc, upper, precision=lax.Precision.HIGHEST, preferred_element_type=F32)
        base_s[...] = pend - pc
        nbp = be_ref.shape[0]
        bstart = (lax.broadcasted_iota(jnp.int32, (nbp, LANES), 0) * bm).astype(F32)
        lane_b = lax.broadcasted_iota(jnp.int32, (nbp, LANES), 1)
        hit = jnp.where((lane_b < n_exp) & (pend[0:1, :] <= bstart), 1.0, 0.0)
        be = jnp.minimum(jnp.sum(hit, axis=1, keepdims=True), float(n_exp - 1))
        pstart = pend[0:1, :] - pc[0:1, :]
        span = jnp.minimum(pstart + cnt[0:1, :], bstart + bm) - jnp.maximum(pstart, bstart)
        nvalid = jnp.sum(jnp.where(lane_b < n_exp, jnp.maximum(span, 0.0), 0.0), axis=1, keepdims=True)
        be_ref[...] = jnp.where(lane_b == 0, be, jnp.where(lane_b == 1, nvalid, 0.0)).astype(jnp.int32)

    @pl.when(ph == 1)
    def _():
        r = lax.broadcasted_iota(jnp.int32, (tm, tm), 0)
        cc = lax.broadcasted_iota(jnp.int32, (tm, tm), 1)
        lower = jnp.where(cc < r, 1.0, 0.0).astype(BF16)
        cum = jnp.dot(lower, c.astype(BF16), preferred_element_type=F32)
        tot = cum + base_s[0:1, :]
        p0 = jnp.sum(jnp.where(oh0, tot, 0.0), axis=1, keepdims=True)
        p1 = jnp.sum(jnp.where(oh1, tot, 0.0), axis=1, keepdims=True)
        pos = jnp.where(lane_f == 0.0, p0, jnp.where(lane_f == 1.0, p1, 0.0))
        pos_ref[...] = pos.astype(jnp.int32)
        base_s[0:1, :] = base_s[0:1, :] + csum


def _route(ri, *, n_blocks, n_exp, bm):
    t = ri.shape[0]
    tm = min(TOKEN_TILE, t)
    nbp = -(-n_blocks // SUBLANES) * SUBLANES
    kern = functools.partial(_route_kernel, tm=tm, bm=bm, n_exp=n_exp)
    return pl.pallas_call(
        kern,
        grid=(2, t // tm),
        in_specs=[pl.BlockSpec((tm, LANES), lambda p, i: (i, 0))],
        out_specs=[pl.BlockSpec((tm, LANES), lambda p, i: (i * p, 0)), _full((nbp, LANES))],
        out_shape=[jax.ShapeDtypeStruct((t, LANES), jnp.int32), jax.ShapeDtypeStruct((nbp, LANES), jnp.int32)],
        scratch_shapes=[pltpu.VMEM((SUBLANES, LANES), F32)] * 2,
        compiler_params=_cparams(("arbitrary", "arbitrary")),
        name="moe_route",
    )(ri)


def _scatter_kernel(pos_ref, nv_ref, x_ref, xs_out, zbuf, sem, zsem, *, tm, tiles, n_blocks, bm):
    step = pl.program_id(0)
    blk_rows = bm * tiles

    def zero_copy(i):
        dst = pl.multiple_of(i * blk_rows, blk_rows)
        return pltpu.make_async_copy(zbuf, xs_out.at[pl.ds(dst, blk_rows), :], zsem)

    @pl.when(step == 0)
    def _():
        zbuf[...] = jnp.zeros(zbuf.shape, F32)

        def zstart(i, carry):
            @pl.when(nv_ref[i] < bm)
            def _():
                zero_copy(i).start()
            return carry

        def zwait(i, carry):
            @pl.when(nv_ref[i] < bm)
            def _():
                zero_copy(i).wait()
            return carry

        lax.fori_loop(0, n_blocks, zstart, 0)
        lax.fori_loop(0, n_blocks, zwait, 0)

    def row_copy(t, k):
        src = pl.multiple_of(t * tiles, tiles)
        dst = pl.multiple_of(pos_ref[2 * t + k] * tiles, tiles)
        return pltpu.make_async_copy(x_ref.at[pl.ds(src, tiles), :], xs_out.at[pl.ds(dst, tiles), :], sem)

    def start(t, carry):
        row_copy(t, 0).start(priority=0)
        row_copy(t, 1).start(priority=1)
        return carry

    def wait(t, carry):
        row_copy(t, 0).wait()
        row_copy(t, 1).wait()
        return carry

    lax.fori_loop(0, tm, start, 0, unroll=DMA_UNROLL)
    lax.fori_loop(0, tm, wait, 0, unroll=DMA_UNROLL)


def _scatter(pos_flat, nvalid, x1t, n_blocks, tiles, bm):
    t = x1t.shape[0] // tiles
    tm = min(TOKEN_TILE, t)
    kern = functools.partial(_scatter_kernel, tm=tm, tiles=tiles, n_blocks=n_blocks, bm=bm)
    return pl.pallas_call(
        kern,
        grid=(t // tm,),
        in_specs=[
            pl.BlockSpec((2 * tm,), lambda i: (i,), memory_space=pltpu.SMEM),
            pl.BlockSpec(memory_space=pltpu.SMEM),
            pl.BlockSpec((tm * tiles, LANES), lambda i: (i, 0)),
        ],
        out_specs=pl.BlockSpec(memory_space=pl.ANY),
        out_shape=jax.ShapeDtypeStruct((n_blocks * bm * tiles, LANES), F32),
        scratch_shapes=[pltpu.VMEM((bm * tiles, LANES), F32),
                        pltpu.SemaphoreType.DMA(()), pltpu.SemaphoreType.DMA(())],
        compiler_params=_cparams(("arbitrary",)),
        name="moe_scatter",
    )(pos_flat, nvalid, x1t)


def _expert_kernel(be_ref, nv_ref, x_ref, wg_ref, wu_ref, wd_ref, y_ref, wg_s, wu_s, wd_s, *, tiles):
    i = pl.program_id(0)
    n_valid = nv_ref[i]

    @pl.when((i == 0) | (be_ref[i] != be_ref[jnp.maximum(i - 1, 0)]))
    def _():
        wg_s[...] = wg_ref[0].astype(BF16)
        wu_s[...] = wu_ref[0].astype(BF16)
        wd_s[...] = wd_ref[0].astype(BF16)

    @pl.when(n_valid == 0)
    def _():
        y_ref[...] = jnp.zeros(y_ref.shape, F32)

    @pl.when(n_valid > 0)
    def _():
        xb = _load_token_tiles(x_ref, x_ref.shape[0] // tiles, tiles).astype(BF16)
        g = jnp.dot(xb, wg_s[...], preferred_element_type=F32)
        u = jnp.dot(xb, wu_s[...], preferred_element_type=F32)
        h = g * jax.nn.sigmoid(g) * u
        _store_token_tiles(y_ref, jnp.dot(h.astype(BF16), wd_s[...], preferred_element_type=F32))


def _experts(block_expert, nvalid, xs, wg, wu, wd, bm):
    d, de = wg.shape[1], wg.shape[2]
    tiles = d // LANES
    blk_rows = bm * tiles
    nb = xs.shape[0] // blk_rows
    grid_spec = pltpu.PrefetchScalarGridSpec(
        num_scalar_prefetch=2,
        grid=(nb,),
        in_specs=[
            pl.BlockSpec((blk_rows, LANES), lambda i, be, nv: (i, 0)),
            pl.BlockSpec((1, d, de), lambda i, be, nv: (be[i], 0, 0)),
            pl.BlockSpec((1, d, de), lambda i, be, nv: (be[i], 0, 0)),
            pl.BlockSpec((1, de, d), lambda i, be, nv: (be[i], 0, 0)),
        ],
        out_specs=pl.BlockSpec((blk_rows, LANES), lambda i, be, nv: (i, 0)),
        scratch_shapes=[pltpu.VMEM((d, de), BF16), pltpu.VMEM((d, de), BF16), pltpu.VMEM((de, d), BF16)],
    )
    return pl.pallas_call(
        functools.partial(_expert_kernel, tiles=tiles),
        grid_spec=grid_spec,
        out_shape=jax.ShapeDtypeStruct(xs.shape, F32),
        compiler_params=_cparams(("arbitrary",)),
        name="moe_experts",
    )(block_expert, nvalid, xs, wg, wu, wd)


def _final_kernel(pos_ref, posn_ref, x1_ref, ri_ref, p_ref, ys_hbm, wpg_ref, wpp_ref, g2_ref, b2_ref,
                  pg_ref, out_ref, ybuf, sems, *, tm, tiles):
    step = pl.program_id(0)
    n_steps = pl.num_programs(0)
    slot = step % 2

    def row_copy(tab_ref, t, k, sl):
        src = pl.multiple_of(tab_ref[2 * t + k] * tiles, tiles)
        dst = pl.multiple_of(t * tiles, tiles)
        return pltpu.make_async_copy(ys_hbm.at[pl.ds(src, tiles), :], ybuf.at[sl, k, pl.ds(dst, tiles), :],
                                     sems.at[sl])

    def start_tile(tab_ref, sl):
        def body(t, carry):
            row_copy(tab_ref, t, 0, sl).start(priority=0)
            row_copy(tab_ref, t, 1, sl).start(priority=1)
            return carry

        lax.fori_loop(0, tm, body, 0, unroll=DMA_UNROLL)

    @pl.when(step == 0)
    def _():
        start_tile(pos_ref, 0)

    @pl.when(step + 1 < n_steps)
    def _():
        start_tile(posn_ref, 1 - slot)

    def wait(t, carry):
        row_copy(pos_ref, t, 0, slot).wait()
        row_copy(pos_ref, t, 1, slot).wait()
        return carry

    lax.fori_loop(0, tm, wait, 0, unroll=DMA_UNROLL)

    ri = ri_ref[...]
    y0 = _load_token_tiles(ybuf.at[slot, 0], tm, tiles)
    y1 = _load_token_tiles(ybuf.at[slot, 1], tm, tiles)
    ffn = ri[:, 2:3] * y0 + ri[:, 3:4] * y1
    x2 = _layer_norm(DN_ALPHA * x1_ref[...] + ffn, g2_ref[...], b2_ref[...])
    gate = jax.nn.sigmoid(jnp.dot(x2.astype(BF16), wpg_ref[...], preferred_element_type=F32))
    proj = jnp.dot(p_ref[...].astype(BF16), wpp_ref[...], preferred_element_type=F32)
    out_ref[...] = x2 + _rms_norm(gate * proj, pg_ref[...])


def _final(pos_flat, x1, ri, p, ys, wpg, wpp, g2, b2, pg):
    t, d = x1.shape
    tm = min(TOKEN_TILE, t)
    n_steps = t // tm
    tiles = d // LANES
    row = lambda w: pl.BlockSpec((tm, w), lambda i: (i, 0))
    consts = [wpg, wpp, g2, b2, pg]
    kern = functools.partial(_final_kernel, tm=tm, tiles=tiles)
    return pl.pallas_call(
        kern,
        grid=(n_steps,),
        in_specs=[pl.BlockSpec((2 * tm,), lambda i: (i,), memory_space=pltpu.SMEM),
                  pl.BlockSpec((2 * tm,), lambda i: (jnp.minimum(i + 1, n_steps - 1),), memory_space=pltpu.SMEM),
                  row(d), row(LANES), row(p.shape[1]), pl.BlockSpec(memory_space=pl.ANY)]
                 + [_full(a.shape) for a in consts],
        out_specs=row(d),
        out_shape=jax.ShapeDtypeStruct((t, d), F32),
        scratch_shapes=[pltpu.VMEM((2, 2, tm * tiles, LANES), F32), pltpu.SemaphoreType.DMA((2,))],
        compiler_params=_cparams(("arbitrary",)),
        name="combine_ln2_ple",
    )(pos_flat, pos_flat, x1, ri, p, ys, *consts)


def _lane_patterns(pair_layout):
    lane = np.arange(LANES)
    if pair_layout:
        valid = (lane < 2 * QK_ROPE).astype(np.float32)
    else:
        valid = np.ones(LANES, np.float32)
    sign = np.where((lane % QK_ROPE) < HALF_ROPE, -1.0, 1.0).astype(np.float32) * valid
    return lane % HALF_ROPE, sign, valid


def _prep_weights(w_in, w_uq, w_uk, w_uv, w_group, w_router):
    d = w_in.shape[0]
    ql = w_uq.shape[0]
    kvl, n_heads, _ = w_uk.shape
    o_kv, o_kr = ql, ql + kvl
    o_conv = o_kr + QK_ROPE
    c_conv = (w_in.shape[1] - o_conv - 2 * d) // 2
    o_ga = o_conv + 2 * c_conv
    o_gb = o_ga + d
    kr = w_in[:, o_kr:o_conv]
    kr_sw = jnp.concatenate([kr[:, HALF_ROPE:], kr[:, :HALF_ROPE]], axis=1)
    zpad = jnp.zeros((d, LANES - 2 * QK_ROPE), w_in.dtype)
    win_r = jnp.concatenate([
        w_in[:, :o_kv], w_in[:, o_kv:o_kr], w_in[:, o_conv:o_ga], w_in[:, o_ga:o_gb], w_in[:, o_gb:],
        kr, kr, zpad, kr_sw, kr_sw, zpad], axis=1).astype(BF16)
    assert win_r.shape[1] == C_END and c_conv == C_CB - C_CA and kvl == C_CA - C_KV and ql == C_KV

    hd = QK_NOPE + QK_ROPE
    wq = w_uq.reshape(ql, n_heads, hd)
    nope = wq[:, :, :QK_NOPE].reshape(ql, n_heads * QK_NOPE)
    x1 = wq[:, :, QK_NOPE:QK_NOPE + HALF_ROPE]
    x2 = wq[:, :, QK_NOPE + HALF_ROPE:]
    rope_a = jnp.concatenate([x1, x2], axis=2)
    rope_b = jnp.concatenate([x2, x1], axis=2)
    wuq_s = jnp.concatenate([nope, rope_a.reshape(ql, -1), rope_b.reshape(ql, -1)], axis=1).astype(BF16)

    def pair_layout(r):
        r = r.reshape(ql, n_heads // 2, 2 * QK_ROPE)
        z = jnp.zeros((ql, n_heads // 2, LANES - 2 * QK_ROPE), r.dtype)
        return jnp.concatenate([r, z], axis=2).reshape(ql, -1)

    wuq_p = jnp.concatenate([nope, pair_layout(rope_a), pair_layout(rope_b)], axis=1).astype(BF16)

    wuk_flat = w_uk.reshape(kvl, n_heads * QK_NOPE)
    wuv_flat = w_uv.reshape(kvl, n_heads * V_HEAD)
    wukv = jnp.concatenate([wuk_flat, wuv_flat], axis=1).astype(BF16)
    eye = jnp.eye(n_heads, dtype=w_uk.dtype)
    wk_bd = jnp.einsum('lhn,hg->hngl', w_uk, eye).reshape(n_heads * QK_NOPE, n_heads * kvl).astype(BF16)
    wv_bd = jnp.einsum('lhv,hg->hlgv', w_uv, eye).reshape(n_heads * kvl, n_heads * V_HEAD).astype(BF16)

    n_exp = w_router.shape[1]
    wgr = jnp.concatenate([w_group, w_router,
                           jnp.zeros((d, LANES - N_GROUPS - n_exp), w_group.dtype)], axis=1)
    wgh = wgr.astype(BF16)
    wgl = (wgr - wgh.astype(F32)).astype(BF16)
    return win_r, wuq_p, wuq_s, wukv, wk_bd, wv_bd, wgh, wgl


def _moe(x1t, ri, wg, wu, wd, n_exp):
    t = ri.shape[0]
    tiles = x1t.shape[0] // t
    n_assign = 2 * t
    bm = MOE_BLOCK if n_assign >= 2 * MOE_BLOCK * n_exp else MOE_BLOCK_MIN
    n_blocks = (n_assign + n_exp * (bm - 1) + bm - 1) // bm
    pos, be = _route(ri, n_blocks=n_blocks, n_exp=n_exp, bm=bm)
    pos_flat = pos[:, :2].reshape(-1)
    block_expert = be[:n_blocks, 0]
    nvalid = be[:n_blocks, 1]
    xs = _scatter(pos_flat, nvalid, x1t, n_blocks, tiles, bm)
    ys = _experts(block_expert, nvalid, xs, wg, wu, wd, bm)
    return pos_flat, ys


def kernel(x_prompt, x_sample, cache_ckv, cache_krope, state_conv, page_table, p_prompt, p_sample, w_in, q_norm_g, w_uq, kv_norm_g, w_uk, w_uv, w_o_attn, w_dw, b_dw, conv_ln_g, conv_ln_b, w_pw2, w_out, ln1_g, ln1_b, w_group, w_router, w_gate, w_up, w_down, ln2_g, ln2_b, w_ple_gate, w_ple_proj, ple_norm_g):
    assert w_in.shape[0] == DEPTH
    b, s, d = x_prompt.shape
    bd, sd, _ = x_sample.shape
    assert sd == 1
    n_pages = page_table.shape[1]
    page_rows = cache_ckv.shape[2]
    past = n_pages * page_rows
    n_heads = w_uk.shape[2]
    kvl = w_uk.shape[1]
    width = w_dw.shape[1]
    n_exp = w_router.shape[2]
    c_conv = w_dw.shape[2]

    win_r, wuq_p, wuq_s, wukv, wk_bd, wv_bd, wgh, wgl = _prep_weights(
        w_in[0], w_uq[0], w_uk[0], w_uv[0], w_group[0], w_router[0])
    qg, kvg = q_norm_g, kv_norm_g
    wo, wpw, wout = w_o_attn[0].astype(BF16), w_pw2[0].astype(BF16), w_out[0].astype(BF16)
    wgate, wup, wdown = w_gate[0], w_up[0], w_down[0]
    wpg, wpp = w_ple_gate[0].astype(BF16), w_ple_proj[0].astype(BF16)
    w_dw_pad = jnp.concatenate([w_dw[0], jnp.zeros((CONV_HALO - width, c_conv), F32)], axis=0)

    inv16 = 1.0 / (ROPE_THETA ** (jnp.arange(HALF_ROPE, dtype=F32) / HALF_ROPE))

    def tables(pair_layout, n_rows, offset, step):
        idx, sign, valid = _lane_patterns(pair_layout)
        inv_l = (inv16[idx] * valid)[None, :]
        return _rope_tables(n_rows, offset, step, inv_l, jnp.asarray(sign)[None, :], jnp.asarray(valid)[None, :])

    def trunk_tail(x2d, o, y_or_u, sa, sb, p2d, conv=None):
        x1, x1t, ri = _mix(x2d, o, y_or_u, sa, sb, wo, wpw, wout, conv_ln_g, conv_ln_b, ln1_g, ln1_b,
                           wgh, wgl, conv=conv)
        pos_flat, ys = _moe(x1t, ri, wgate, wup, wdown, n_exp)
        return _final(pos_flat, x1, ri, p2d, ys, wpg, wpp, ln2_g, ln2_b, ple_norm_g)

    xs_ = x_sample.reshape(bd, d)
    cos_s, sin_s = tables(False, SUBLANES, past, 0)
    qn_s, qr_s, ckv_s, kr_s, u_s, sa_s, sb_s = _inproj(
        xs_, win_r, qg, kvg, wuq_s, None, cos_s, sin_s, n_rep=n_heads * QK_ROPE // LANES, seq=1, make_kv=False)
    q_lat = _dense(qn_s, wk_bd, F32).reshape(bd, n_heads, kvl)
    o_lat = _sample_attention(page_table, q_lat, qr_s.reshape(bd, n_heads, QK_ROPE),
                              ckv_s.reshape(bd, 1, kvl), kr_s.reshape(bd, 1, QK_ROPE),
                              cache_ckv[0], jnp.swapaxes(cache_krope[0], 1, 2))
    o_s = _dense(o_lat.reshape(bd, n_heads * kvl), wv_bd, BF16)
    ext_s = jnp.concatenate([state_conv[0], u_s[:, None, :]], axis=1)
    y_s = _conv_sample(jnp.transpose(ext_s, (1, 0, 2)), w_dw_pad, b_dw, width=width)
    out_s = trunk_tail(xs_, o_s, y_s, sa_s, sb_s, p_sample[0].reshape(bd, -1))
    new_conv_s = ext_s[:, 1:, :]

    xp = x_prompt.reshape(b * s, d)
    cos_p, sin_p = tables(True, s, 0, 1)
    qn, qr, kn, v, krd, ckv_p, kr_p, u_p, sa_p, sb_p = _inproj(
        xp, win_r, qg, kvg, wuq_p, wukv, cos_p, sin_p, n_rep=n_heads // 2, seq=s, make_kv=True)
    o_p = _prompt_attention(qn, qr, kn, krd, v, batch=b, seq=s)
    out_p = trunk_tail(xp, o_p, u_p, sa_p, sb_p, p_prompt[0].reshape(b * s, -1),
                       conv=(w_dw_pad, b_dw, s, width))
    u_p3 = u_p.reshape(b, s, c_conv)
    new_conv_p = u_p3[:, s - (width - 1):, :]

    return (out_p.reshape(b, s, d), out_s.reshape(bd, 1, d),
            ckv_p.reshape(1, b, s, kvl), kr_p.reshape(1, b, s, QK_ROPE), new_conv_p[None],
            ckv_s.reshape(1, bd, 1, kvl), kr_s.reshape(1, bd, 1, QK_ROPE), new_conv_s[None])
```

```python
import functools

import numpy as np
import jax
import jax.numpy as jnp
from jax import lax
from jax.experimental import pallas as pl
from jax.experimental.pallas import tpu as pltpu

F32 = jnp.float32
BF16 = jnp.bfloat16

QK_NOPE = 64
QK_ROPE = 32
HALF_ROPE = QK_ROPE // 2
V_HEAD = 64
ROPE_THETA = 10000.0
ATTN_SCALE = (QK_NOPE + QK_ROPE) ** -0.5
LOG2E = 1.4426950408889634
N_GROUPS = 4
EXPERTS_PER_GROUP = 8
DEPTH = 1
DN_ALPHA = (2 * DEPTH) ** 0.25
LN_EPS = 1e-5
RMS_EPS = 1e-6
NEG_INF = -1e30

LANES = 128
SUBLANES = 8
VMEM_LIMIT = 56 * 1024 * 1024

TOKEN_TILE = 512
ATTN_TILE = 512
CONV_CHUNK = 64
CONV_HALO = 32
MOE_BLOCK = 512
MOE_BLOCK_MIN = 128
PAGES_PER_CHUNK = 16
SEQS_PER_STEP = 4
DMA_UNROLL = 8


def _cparams(sem, vmem=VMEM_LIMIT):
    return pltpu.CompilerParams(dimension_semantics=sem, vmem_limit_bytes=vmem)


def _full(shape):
    n = len(shape)
    return pl.BlockSpec(shape, lambda *_: (0,) * n)


def _layer_norm(x, g, b):
    mu = jnp.mean(x, axis=-1, keepdims=True)
    xc = x - mu
    var = jnp.mean(xc * xc, axis=-1, keepdims=True)
    return xc * lax.rsqrt(var + LN_EPS) * g + b


def _rms_norm(x, g):
    return x * lax.rsqrt(jnp.mean(x * x, axis=-1, keepdims=True) + RMS_EPS) * g


def _rope_table_kernel(inv_ref, sign_ref, valid_ref, cos_ref, sin_ref, *, offset, step):
    rows = cos_ref.shape[0]
    r = lax.broadcasted_iota(jnp.int32, (rows, LANES), 0) + pl.program_id(0) * rows
    pos = (offset + step * r).astype(F32)
    ang = pos * inv_ref[...]
    cos_ref[...] = jnp.cos(ang) * valid_ref[...]
    sin_ref[...] = jnp.sin(ang) * sign_ref[...]


def _rope_tables(n_rows, offset, step, inv_l, sign_l, valid_l):
    tr = min(n_rows, 512)
    kern = functools.partial(_rope_table_kernel, offset=offset, step=step)
    return pl.pallas_call(
        kern,
        grid=(n_rows // tr,),
        in_specs=[_full((1, LANES))] * 3,
        out_specs=[pl.BlockSpec((tr, LANES), lambda i: (i, 0))] * 2,
        out_shape=[jax.ShapeDtypeStruct((n_rows, LANES), F32)] * 2,
        compiler_params=_cparams(("arbitrary",)),
        name="rope_tables",
    )(inv_l, sign_l, valid_l)


C_Q, C_KV, C_CA, C_CB, C_GA, C_GB, C_KA, C_END = 0, 384, 640, 1152, 1664, 2688, 3712, 3840


def _rotary_partner(a):
    lane = lax.broadcasted_iota(jnp.int32, (a.shape[0], LANES), 1)
    first_half = (lane % QK_ROPE) < HALF_ROPE
    blocks = []
    for c in range(a.shape[1] // LANES):
        blk = a[:, c * LANES:(c + 1) * LANES]
        blocks.append(jnp.where(first_half, pltpu.roll(blk, LANES - HALF_ROPE, 1), pltpu.roll(blk, HALF_ROPE, 1)))
    return blocks[0] if len(blocks) == 1 else jnp.concatenate(blocks, axis=1)


def _inproj_kernel(x_ref, win_ref, qg_ref, kvg_ref, wuq_ref, *rest, n_rep, make_kv, bcast, q_scale):
    if make_kv:
        (wukv_ref, cos_ref, sin_ref, qn_o, qr_o, kn_o, v_o, krd_o, ckv_o, kr_o, u_o, sa_o, sb_o) = rest
    else:
        (cos_ref, sin_ref, qn_o, qr_o, ckv_o, kr_o, u_o, sa_o, sb_o) = rest
    xb = x_ref[...].astype(BF16)

    def proj(a, b):
        return jnp.dot(xb, win_ref[:, a:b], preferred_element_type=F32)

    if bcast:
        c1, s1 = cos_ref[0:1, :], sin_ref[0:1, :]
    else:
        c1, s1 = cos_ref[...], sin_ref[...]
    cn = jnp.concatenate([c1] * n_rep, axis=1)
    sn = jnp.concatenate([s1] * n_rep, axis=1)
    d_nope = qn_o.shape[1]
    nr = LANES * n_rep

    cqn = _rms_norm(proj(C_Q, C_KV), qg_ref[...])
    qall = jnp.dot(cqn.astype(BF16), wuq_ref[...], preferred_element_type=F32)
    qn_o[...] = (qall[:, :d_nope] * q_scale).astype(qn_o.dtype)
    qa = qall[:, d_nope:d_nope + nr]
    qb = _rotary_partner(qa)
    qr_o[...] = ((qa * cn + qb * sn) * q_scale).astype(qr_o.dtype)

    ckvn = _rms_norm(proj(C_KV, C_CA), kvg_ref[...])
    ckv_o[...] = ckvn
    kra = proj(C_KA, C_END)
    krot = kra * c1 + _rotary_partner(kra) * s1
    kr_o[...] = krot[:, :QK_ROPE]
    if make_kv:
        kv = jnp.dot(ckvn.astype(BF16), wukv_ref[...], preferred_element_type=F32)
        half = kv.shape[1] // 2
        kn_o[...] = kv[:, :half].astype(BF16)
        v_o[...] = kv[:, half:].astype(BF16)
        krd_o[...] = krot.astype(BF16)

    u_o[...] = proj(C_CA, C_CB) * jax.nn.sigmoid(proj(C_CB, C_GA))
    sa_o[...] = jax.nn.sigmoid(proj(C_GA, C_GB)).astype(BF16)
    sb_o[...] = jax.nn.sigmoid(proj(C_GB, C_KA)).astype(BF16)


def _inproj(x, win_r, qg, kvg, wuq_r, wukv, cos_t, sin_t, *, n_rep, seq, make_kv):
    t, d = x.shape
    tm = min(TOKEN_TILE, t)
    d_nope = wuq_r.shape[1] - LANES * n_rep
    nr = LANES * n_rep
    c_conv = C_CB - C_CA
    kvl = C_CA - C_KV
    bcast = seq < tm
    if bcast:
        tab_spec = _full(cos_t.shape)
    else:
        nst = seq // tm
        tab_spec = pl.BlockSpec((tm, LANES), lambda i: (i % nst, 0))
    row = lambda w: pl.BlockSpec((tm, w), lambda i: (i, 0))
    in_specs = [row(d), _full(win_r.shape), _full(qg.shape), _full(kvg.shape), _full(wuq_r.shape)]
    args = [x, win_r, qg, kvg, wuq_r]
    if make_kv:
        in_specs.append(_full(wukv.shape))
        args.append(wukv)
    in_specs += [tab_spec, tab_spec]
    args += [cos_t, sin_t]
    q_dt = BF16 if make_kv else F32
    outs = [(d_nope, BF16), (nr, q_dt)]
    if make_kv:
        outs += [(d_nope, BF16), (d_nope, BF16), (LANES, BF16)]
    outs += [(kvl, F32), (QK_ROPE, F32), (c_conv, F32), (d, BF16), (d, BF16)]
    q_scale = ATTN_SCALE * LOG2E if make_kv else ATTN_SCALE
    kern = functools.partial(_inproj_kernel, n_rep=n_rep, make_kv=make_kv, bcast=bcast, q_scale=q_scale)
    return pl.pallas_call(
        kern,
        grid=(t // tm,),
        in_specs=in_specs,
        out_specs=[row(w) for w, _ in outs],
        out_shape=[jax.ShapeDtypeStruct((t, w), dt) for w, dt in outs],
        compiler_params=_cparams(("arbitrary",)),
        name="inproj_kv" if make_kv else "inproj_q",
    )(*args)


def _online_softmax(state, s, v):
    m, l, acc = state
    m_new = jnp.maximum(m, jnp.max(s, axis=1, keepdims=True))
    alpha = jnp.exp2(m - m_new)
    p = jnp.exp2(s - jnp.concatenate([m_new] * (s.shape[1] // LANES), axis=1))
    l = alpha * l + jnp.sum(p, axis=1, keepdims=True)
    acc = alpha * acc + jnp.dot(p.astype(BF16), v, preferred_element_type=F32)
    return m_new, l, acc


def _pattn_kernel(qn_ref, qr_ref, kn_ref, krd_ref, v_ref, o_ref, m_s, l_s, acc_s, *, blk, nq):
    half = blk // 2
    lane = lax.broadcasted_iota(jnp.int32, (blk, LANES), 1)
    row_a = lax.broadcasted_iota(jnp.int32, (blk, half), 0)
    col_a = lax.broadcasted_iota(jnp.int32, (blk, half), 1)
    row_b = lax.broadcasted_iota(jnp.int32, (half, half), 0)
    col_b = lax.broadcasted_iota(jnp.int32, (half, half), 1)
    dn_t = (((1,), (1,)), ((), ()))
    for i in range(nq):
        qn = qn_ref[i * blk:(i + 1) * blk, :]
        qr = qr_ref[i * blk:(i + 1) * blk, :]
        zero = jnp.zeros_like(qn)
        q_heads = (
            jnp.concatenate([jnp.where(lane < QK_NOPE, qn, zero), jnp.where(lane < QK_ROPE, qr, zero)], axis=1),
            jnp.concatenate([jnp.where(lane >= QK_NOPE, qn, zero),
                             jnp.where((lane >= QK_ROPE) & (lane < 2 * QK_ROPE), qr, zero)], axis=1),
        )
        state = [(jnp.full((blk, LANES), NEG_INF, F32), jnp.zeros((blk, LANES), F32),
                  jnp.zeros((blk, LANES), F32))] * 2
        for j in range(i):
            k = jnp.concatenate([kn_ref[j * blk:(j + 1) * blk, :], krd_ref[j * blk:(j + 1) * blk, :]], axis=1)
            v = v_ref[j * blk:(j + 1) * blk, :]
            for h in range(2):
                s = lax.dot_general(q_heads[h], k, dn_t, preferred_element_type=F32)
                state[h] = _online_softmax(state[h], s, v)
        k0 = i * blk
        for h in range(2):
            k_a = jnp.concatenate([kn_ref[k0:k0 + half, :], krd_ref[k0:k0 + half, :]], axis=1)
            s_a = lax.dot_general(q_heads[h], k_a, dn_t, preferred_element_type=F32)
            s_a = jnp.where(col_a <= row_a, s_a, NEG_INF)
            m, l, acc = _online_softmax(state[h], s_a, v_ref[k0:k0 + half, :])
            m_s[i, h], l_s[i, h], acc_s[i, h] = m, l, acc
            k_b = jnp.concatenate([kn_ref[k0 + half:k0 + blk, :], krd_ref[k0 + half:k0 + blk, :]], axis=1)
            s_b = lax.dot_general(q_heads[h][half:], k_b, dn_t, preferred_element_type=F32)
            s_b = jnp.where(col_b <= row_b, s_b, NEG_INF)
            _, l_b, acc_b = _online_softmax(
                (m_s[i, h, half:, :], l_s[i, h, half:, :], acc_s[i, h, half:, :]), s_b,
                v_ref[k0 + half:k0 + blk, :])
            l_s[i, h, half:, :] = l_b
            acc_s[i, h, half:, :] = acc_b
        o = jnp.where(lane < V_HEAD, acc_s[i, 0] / l_s[i, 0], acc_s[i, 1] / l_s[i, 1])
        o_ref[i * blk:(i + 1) * blk, :] = o.astype(o_ref.dtype)


def _prompt_attention(qn, qr, kn, krd, v, *, batch, seq):
    t, d = qn.shape
    blk = min(ATTN_TILE, seq)
    nq = seq // blk
    n_pairs = d // LANES
    kspec = pl.BlockSpec((seq, LANES), lambda b, j: (b, j))
    kern = functools.partial(_pattn_kernel, blk=blk, nq=nq)
    return pl.pallas_call(
        kern,
        grid=(batch, n_pairs),
        in_specs=[kspec, kspec, kspec, pl.BlockSpec((seq, LANES), lambda b, j: (b, 0)), kspec],
        out_specs=kspec,
        out_shape=jax.ShapeDtypeStruct((t, d), BF16),
        scratch_shapes=[pltpu.VMEM((nq, 2, blk, LANES), F32)] * 3,
        compiler_params=_cparams(("arbitrary",) * 2),
        name="prompt_attention",
    )(qn, qr, kn, krd, v)


def _dense_kernel(a_ref, w_ref, o_ref):
    o_ref[...] = jnp.dot(a_ref[...].astype(BF16), w_ref[...], preferred_element_type=F32).astype(o_ref.dtype)


def _dense(a, w, out_dtype):
    m, n = a.shape[0], w.shape[1]
    return pl.pallas_call(
        _dense_kernel,
        grid=(1,),
        in_specs=[_full(a.shape), _full(w.shape)],
        out_specs=_full((m, n)),
        out_shape=jax.ShapeDtypeStruct((m, n), out_dtype),
        compiler_params=_cparams(("arbitrary",)),
        name="dense",
    )(a, w)


def _sattn_kernel(pt_ref, ptn_ref, ql_ref, qr_ref, cnew_ref, knew_ref, ckv_hbm, krt_hbm, o_ref,
                  cbuf, kbuf, sems, *, n_chunks, ppc, nb):
    step = pl.program_id(0)
    n_steps = pl.num_programs(0)
    kvl = cbuf.shape[4]
    kc = ppc * cbuf.shape[3]
    n_heads = ql_ref.shape[1]
    dn_t = (((1,), (1,)), ((), ()))

    def chunk_copies(tab_ref, chunk, slot):
        out = []
        for b in range(nb):
            for pg in range(ppc):
                page = tab_ref[b, 0, chunk * ppc + pg]
                out.append(pltpu.make_async_copy(ckv_hbm.at[page], cbuf.at[slot, b, pg], sems.at[0, slot]))
                out.append(pltpu.make_async_copy(krt_hbm.at[page], kbuf.at[slot, b, pg], sems.at[1, slot]))
        return out

    def start_chunk(tab_ref, chunk, slot):
        for cp in chunk_copies(tab_ref, chunk, slot):
            cp.start()

    @pl.when(step == 0)
    def _():
        start_chunk(pt_ref, 0, 0)

    first = step * n_chunks
    qls = [ql_ref[b].astype(BF16) for b in range(nb)]
    qrs = [qr_ref[b].astype(BF16) for b in range(nb)]

    def body(c, carry):
        slot = (first + c) % 2

        @pl.when(c + 1 < n_chunks)
        def _():
            start_chunk(pt_ref, c + 1, 1 - slot)

        @pl.when((c + 1 == n_chunks) & (step + 1 < n_steps))
        def _():
            start_chunk(ptn_ref, 0, 1 - slot)

        for cp in chunk_copies(pt_ref, c, slot):
            cp.wait()
        new = []
        for b in range(nb):
            m, l, acc = carry[3 * b:3 * b + 3]
            ck = cbuf[slot, b].reshape(kc, kvl).astype(BF16)
            krt = jnp.concatenate([kbuf[slot, b, pg] for pg in range(ppc)], axis=1).astype(BF16)
            s = lax.dot_general(qls[b], ck, dn_t, preferred_element_type=F32)
            s = s + jnp.dot(qrs[b], krt, preferred_element_type=F32)
            m_new = jnp.maximum(m, jnp.max(s, axis=1, keepdims=True))
            alpha = jnp.exp(m - m_new)
            p = jnp.exp(s - m_new)
            l = alpha * l + jnp.sum(p, axis=1, keepdims=True)
            acc = alpha * acc + jnp.dot(p.astype(BF16), ck, preferred_element_type=F32)
            new += [m_new, l, acc]
        return tuple(new)

    init = (jnp.full((n_heads, 1), NEG_INF, F32), jnp.zeros((n_heads, 1), F32),
            jnp.zeros((n_heads, kvl), F32)) * nb
    carry = lax.fori_loop(0, n_chunks, body, init)

    for b in range(nb):
        m, l, acc = carry[3 * b:3 * b + 3]
        cnew = cnew_ref[b]
        knew = knew_ref[b]
        s_new = (jnp.sum(ql_ref[b] * cnew, axis=1, keepdims=True)
                 + jnp.sum(qr_ref[b] * knew, axis=1, keepdims=True))
        m_new = jnp.maximum(m, s_new)
        alpha = jnp.exp(m - m_new)
        p_new = jnp.exp(s_new - m_new)
        l = alpha * l + p_new
        acc = alpha * acc + p_new * cnew
        o_ref[b] = acc / l


def _sample_attention(page_table, q_lat, q_rope, c_new, k_new, cache_ckv, cache_krope_t):
    bd, n_heads, kvl = q_lat.shape
    n_pages = page_table.shape[1]
    page_rows = cache_ckv.shape[1]
    ppc = min(PAGES_PER_CHUNK, n_pages)
    n_chunks = n_pages // ppc
    nb = min(SEQS_PER_STEP, bd)
    n_steps = bd // nb
    pt3 = page_table.reshape(bd, 1, n_pages)
    per_b = lambda s: pl.BlockSpec((nb,) + s, lambda i: (i, 0, 0))
    kern = functools.partial(_sattn_kernel, n_chunks=n_chunks, ppc=ppc, nb=nb)
    return pl.pallas_call(
        kern,
        grid=(n_steps,),
        in_specs=[
            pl.BlockSpec((nb, 1, n_pages), lambda i: (i, 0, 0), memory_space=pltpu.SMEM),
            pl.BlockSpec((nb, 1, n_pages), lambda i: (jnp.minimum(i + 1, n_steps - 1), 0, 0),
                         memory_space=pltpu.SMEM),
            per_b((n_heads, kvl)), per_b((n_heads, QK_ROPE)), per_b((1, kvl)), per_b((1, QK_ROPE)),
            pl.BlockSpec(memory_space=pl.ANY), pl.BlockSpec(memory_space=pl.ANY),
        ],
        out_specs=per_b((n_heads, kvl)),
        out_shape=jax.ShapeDtypeStruct((bd, n_heads, kvl), F32),
        scratch_shapes=[
            pltpu.VMEM((2, nb, ppc, page_rows, kvl), F32),
            pltpu.VMEM((2, nb, ppc, QK_ROPE, page_rows), F32),
            pltpu.SemaphoreType.DMA((2, 2)),
        ],
        compiler_params=_cparams(("arbitrary",)),
        name="sample_attention",
    )(pt3, pt3, q_lat, q_rope, c_new, k_new, cache_ckv, cache_krope_t)


def _conv_rows(u_ref, halo_ref, w_ref, b_ref, y_ref, ext_s, z_s, first_tile, width):
    ts, ch = u_ref.shape
    left = jnp.where(first_tile, jnp.zeros_like(halo_ref), halo_ref[...])
    ext_s[0:CONV_HALO, :] = left
    ext_s[CONV_HALO:CONV_HALO + ts, :] = u_ref[...]
    ext_s[CONV_HALO + ts:CONV_HALO + ts + SUBLANES, :] = jnp.zeros((SUBLANES, ch), F32)
    off = CONV_HALO - (width - 1)
    win = CONV_CHUNK + SUBLANES
    for c in range(ts // CONV_CHUNK):
        r0 = c * CONV_CHUNK
        acc = jnp.broadcast_to(b_ref[...], (CONV_CHUNK, ch))
        for r in range(SUBLANES):
            z = None
            for a in range((off + width + SUBLANES - 1) // SUBLANES):
                k = SUBLANES * a + r - off
                if 0 <= k < width:
                    term = w_ref[k:k + 1, :] * ext_s[r0 + SUBLANES * a:r0 + SUBLANES * a + win, :]
                    z = term if z is None else z + term
            if r == 0:
                acc = acc + z[:CONV_CHUNK]
            else:
                z_s[r] = z
                acc = acc + z_s[r, r:r + CONV_CHUNK, :]
        y_ref[r0:r0 + CONV_CHUNK, :] = acc


def _conv_sample_kernel(ext_ref, w_ref, b_ref, y_ref, *, width):
    acc = jnp.broadcast_to(b_ref[...], y_ref.shape)
    for k in range(width):
        acc = acc + w_ref[k:k + 1, :] * ext_ref[k]
    y_ref[...] = acc


def _conv_sample(ext_t, w_pad, b, *, width):
    _, bd, c = ext_t.shape
    kern = functools.partial(_conv_sample_kernel, width=width)
    return pl.pallas_call(
        kern,
        grid=(1,),
        in_specs=[_full(ext_t.shape), _full(w_pad.shape), _full(b.shape)],
        out_specs=_full((bd, c)),
        out_shape=jax.ShapeDtypeStruct((bd, c), F32),
        compiler_params=_cparams(("arbitrary",)),
        name="conv_sample",
    )(ext_t, w_pad, b)


def _store_token_tiles(ref, x):
    rows, d = x.shape
    tiles = d // LANES
    for c in range(tiles):
        ref[pl.ds(c, rows, stride=tiles), :] = x[:, c * LANES:(c + 1) * LANES]


def _load_token_tiles(ref, rows, tiles):
    return jnp.concatenate([ref[pl.ds(c, rows, stride=tiles), :] for c in range(tiles)], axis=1)


def _mix_kernel(x_ref, o_ref, *rest, conv_tiles, width):
    if conv_tiles:
        (u_ref, halo_ref, wdw_ref, bdw_ref, sa_ref, sb_ref, wo_ref, wpw_ref, wout_ref, cg_ref, cb_ref,
         g1_ref, b1_ref, wgh_ref, wgl_ref, x1_ref, x1t_ref, ri_ref, cnt_ref, ext_s, z_s, y_ref) = rest
        first_tile = pl.program_id(0) % conv_tiles == 0
        _conv_rows(u_ref, halo_ref, wdw_ref, bdw_ref, y_ref, ext_s, z_s, first_tile, width)
    else:
        (y_ref, sa_ref, sb_ref, wo_ref, wpw_ref, wout_ref, cg_ref, cb_ref,
         g1_ref, b1_ref, wgh_ref, wgl_ref, x1_ref, x1t_ref, ri_ref, cnt_ref) = rest
    branch_b = jnp.dot(o_ref[...], wo_ref[...], preferred_element_type=F32)
    z = _layer_norm(y_ref[...], cg_ref[...], cb_ref[...])
    z = z * jax.nn.sigmoid(z)
    branch_a = jnp.dot(z.astype(BF16), wpw_ref[...], preferred_element_type=F32)
    mixin = sa_ref[...].astype(F32) * branch_a + sb_ref[...].astype(F32) * branch_b
    mix = jnp.dot(mixin.astype(BF16), wout_ref[...], preferred_element_type=F32)
    x1 = _layer_norm(DN_ALPHA * x_ref[...] + mix, g1_ref[...], b1_ref[...])
    x1_ref[...] = x1
    _store_token_tiles(x1t_ref, x1)

    x_hi = x1.astype(BF16)
    x_lo = (x1 - x_hi.astype(F32)).astype(BF16)
    lg = (jnp.dot(x_hi, wgh_ref[...], preferred_element_type=F32)
          + jnp.dot(x_lo, wgh_ref[...], preferred_element_type=F32)
          + jnp.dot(x_hi, wgl_ref[...], preferred_element_type=F32))
    n_exp = N_GROUPS * EXPERTS_PER_GROUP
    lane = lax.broadcasted_iota(jnp.int32, lg.shape, 1)
    lane_f = lane.astype(F32)
    big = float(LANES)
    gmask = lane < N_GROUPS
    lgm = jnp.where(gmask, lg, NEG_INF)
    gmax = jnp.max(lgm, axis=1, keepdims=True)
    gidx = jnp.min(jnp.where(lgm == gmax, lane_f, big), axis=1, keepdims=True)
    pg_sel = 1.0 / jnp.sum(jnp.where(gmask, jnp.exp(lgm - gmax), 0.0), axis=1, keepdims=True)
    egroup = jnp.floor((lane_f - N_GROUPS) * (1.0 / EXPERTS_PER_GROUP))
    emask = (lane >= N_GROUPS) & (lane < N_GROUPS + n_exp) & (egroup == gidx)
    le = jnp.where(emask, lg, NEG_INF)
    v1 = jnp.max(le, axis=1, keepdims=True)
    i1 = jnp.min(jnp.where(le == v1, lane_f, big), axis=1, keepdims=True)
    le2 = jnp.where(lane_f == i1, NEG_INF, le)
    v2 = jnp.max(le2, axis=1, keepdims=True)
    i2 = jnp.min(jnp.where(le2 == v2, lane_f, big), axis=1, keepdims=True)
    e = jnp.exp(v2 - v1)
    gate1 = pg_sel / (1.0 + e)
    gate2 = pg_sel * e / (1.0 + e)
    ri = jnp.where(lane == 0, i1 - N_GROUPS,
                   jnp.where(lane == 1, i2 - N_GROUPS,
                             jnp.where(lane == 2, gate1, jnp.where(lane == 3, gate2, 0.0))))
    ri_ref[...] = ri

    chosen = (lane_f == i1 - N_GROUPS) | (lane_f == i2 - N_GROUPS)

    @pl.when(pl.program_id(0) == 0)
    def _():
        cnt_ref[...] = jnp.zeros(cnt_ref.shape, F32)

    cnt_ref[0:1, :] = cnt_ref[0:1, :] + jnp.sum(jnp.where(chosen, 1.0, 0.0), axis=0, keepdims=True)


def _mix(x, o, y_or_u, sa, sb, wo, wpw, wout, cg, cb, g1, b1, wgh, wgl, conv=None):
    t, d = x.shape
    tm = min(TOKEN_TILE, t)
    c = y_or_u.shape[1]
    row = lambda w: pl.BlockSpec((tm, w), lambda i: (i, 0))
    consts = [wo, wpw, wout, cg, cb, g1, b1, wgh, wgl]
    tiles = d // LANES
    if conv is None:
        kern = functools.partial(_mix_kernel, conv_tiles=0, width=0)
        lead_specs, lead, scratch = [row(c)], [y_or_u], []
    else:
        w_pad, b_dw, seq, width = conv
        assert seq % tm == 0
        hb = tm // CONV_HALO
        kern = functools.partial(_mix_kernel, conv_tiles=seq // tm, width=width)
        lead_specs = [row(c), pl.BlockSpec((CONV_HALO, c), lambda i: (jnp.maximum(i * hb - 1, 0), 0)),
                      _full(w_pad.shape), _full(b_dw.shape)]
        lead = [y_or_u, y_or_u, w_pad, b_dw]
        scratch = [pltpu.VMEM((CONV_HALO + tm + SUBLANES, c), F32),
                   pltpu.VMEM((SUBLANES, CONV_CHUNK + SUBLANES, c), F32), pltpu.VMEM((tm, c), F32)]
    return pl.pallas_call(
        kern,
        grid=(t // tm,),
        in_specs=[row(d), row(d)] + lead_specs + [row(d), row(d)] + [_full(a.shape) for a in consts],
        out_specs=[row(d), pl.BlockSpec((tm * tiles, LANES), lambda i: (i, 0)), row(LANES),
                   _full((SUBLANES, LANES))],
        out_shape=[jax.ShapeDtypeStruct((t, d), F32), jax.ShapeDtypeStruct((t * tiles, LANES), F32),
                   jax.ShapeDtypeStruct((t, LANES), F32), jax.ShapeDtypeStruct((SUBLANES, LANES), F32)],
        scratch_shapes=scratch,
        compiler_params=_cparams(("arbitrary",)),
        name="mix_ln1_router",
    )(x, o, *lead, sa, sb, *consts)


def _route_kernel(ri_ref, cnt_ref, pos_ref, be_ref, base_s, *, tm, bm, n_exp):
    i = pl.program_id(0)
    lane_f = lax.broadcasted_iota(jnp.int32, (tm, LANES), 1).astype(F32)
    ri = ri_ref[...]
    oh0 = lane_f == ri[:, 0:1]
    oh1 = lane_f == ri[:, 1:2]
    c = jnp.where(oh0 | oh1, 1.0, 0.0)
    csum = jnp.sum(c, axis=0, keepdims=True)

    @pl.when(i == 0)
    def _():
        cnt = cnt_ref[...]
        pc = jnp.floor((cnt + (bm - 1)) * (1.0 / bm)) * bm
        r = lax.broadcasted_iota(jnp.int32, (LANES, LANES), 0)
        cc = lax.broadcasted_iota(jnp.int32, (LANES, LANES), 1)
        upper = jnp.where(r <= cc, 1.0, 0.0)
        pend = jnp.dot(pc, upper, precision=lax.Precision.HIGHEST, preferred_element_type=F32)
        base_s[...] = pend - pc
        nbp = be_ref.shape[0]
        bstart = (lax.broadcasted_iota(jnp.int32, (nbp, LANES), 0) * bm).astype(F32)
        lane_b = lax.broadcasted_iota(jnp.int32, (nbp, LANES), 1)
        hit = jnp.where((lane_b < n_exp) & (pend[0:1, :] <= bstart), 1.0, 0.0)
        be = jnp.minimum(jnp.sum(hit, axis=1, keepdims=True), float(n_exp - 1))
        pstart = pend[0:1, :] - pc[0:1, :]
        span = jnp.minimum(pstart + cnt[0:1, :], bstart + bm) - jnp.maximum(pstart, bstart)
        nvalid = jnp.sum(jnp.where(lane_b < n_exp, jnp.maximum(span, 0.0), 0.0), axis=1, keepdims=True)
        be_ref[...] = jnp.where(lane_b == 0, be, jnp.where(lane_b == 1, nvalid, 0.0)).astype(jnp.int32)

    r = lax.broadcasted_iota(jnp.int32, (tm, tm), 0)
    cc = lax.broadcasted_iota(jnp.int32, (tm, tm), 1)
    lower = jnp.where(cc < r, 1.0, 0.0).astype(BF16)
    cum = jnp.dot(lower, c.astype(BF16), preferred_element_type=F32)
    tot = cum + base_s[0:1, :]
    p0 = jnp.sum(jnp.where(oh0, tot, 0.0), axis=1, keepdims=True)
    p1 = jnp.sum(jnp.where(oh1, tot, 0.0), axis=1, keepdims=True)
    pos = jnp.where(lane_f == 0.0, p0, jnp.where(lane_f == 1.0, p1, 0.0))
    pos_ref[...] = pos.astype(jnp.int32)
    base_s[0:1, :] = base_s[0:1, :] + csum


def _route(ri, cnt, *, n_blocks, n_exp, bm):
    t = ri.shape[0]
    tm = min(TOKEN_TILE, t)
    nbp = -(-n_blocks // SUBLANES) * SUBLANES
    kern = functools.partial(_route_kernel, tm=tm, bm=bm, n_exp=n_exp)
    return pl.pallas_call(
        kern,
        grid=(t // tm,),
        in_specs=[pl.BlockSpec((tm, LANES), lambda i: (i, 0)), _full(cnt.shape)],
        out_specs=[pl.BlockSpec((tm, LANES), lambda i: (i, 0)), _full((nbp, LANES))],
        out_shape=[jax.ShapeDtypeStruct((t, LANES), jnp.int32), jax.ShapeDtypeStruct((nbp, LANES), jnp.int32)],
        scratch_shapes=[pltpu.VMEM((SUBLANES, LANES), F32)],
        compiler_params=_cparams(("arbitrary",)),
        name="moe_route",
    )(ri, cnt)


def _scatter_kernel(pos_ref, nv_ref, x_ref, xs_out, zbuf, sem, zsem, *, tm, tiles, n_blocks, bm):
    step = pl.program_id(0)
    blk_rows = bm * tiles

    def zero_copy(i):
        dst = pl.multiple_of(i * blk_rows, blk_rows)
        return pltpu.make_async_copy(zbuf, xs_out.at[pl.ds(dst, blk_rows), :], zsem)

    @pl.when(step == 0)
    def _():
        zbuf[...] = jnp.zeros(zbuf.shape, F32)

        def zstart(i, carry):
            @pl.when(nv_ref[i] < bm)
            def _():
                zero_copy(i).start()
            return carry

        def zwait(i, carry):
            @pl.when(nv_ref[i] < bm)
            def _():
                zero_copy(i).wait()
            return carry

        lax.fori_loop(0, n_blocks, zstart, 0)
        lax.fori_loop(0, n_blocks, zwait, 0)

    def row_copy(t, k):
        src = pl.multiple_of(t * tiles, tiles)
        dst = pl.multiple_of(pos_ref[2 * t + k] * tiles, tiles)
        return pltpu.make_async_copy(x_ref.at[pl.ds(src, tiles), :], xs_out.at[pl.ds(dst, tiles), :], sem)

    def start(t, carry):
        row_copy(t, 0).start(priority=0)
        row_copy(t, 1).start(priority=1)
        return carry

    def wait(t, carry):
        row_copy(t, 0).wait()
        row_copy(t, 1).wait()
        return carry

    lax.fori_loop(0, tm, start, 0, unroll=DMA_UNROLL)
    lax.fori_loop(0, tm, wait, 0, unroll=DMA_UNROLL)


def _scatter(pos_flat, nvalid, x1t, n_blocks, tiles, bm):
    t = x1t.shape[0] // tiles
    tm = min(TOKEN_TILE, t)
    kern = functools.partial(_scatter_kernel, tm=tm, tiles=tiles, n_blocks=n_blocks, bm=bm)
    return pl.pallas_call(
        kern,
        grid=(t // tm,),
        in_specs=[
            pl.BlockSpec((2 * tm,), lambda i: (i,), memory_space=pltpu.SMEM),
            pl.BlockSpec(memory_space=pltpu.SMEM),
            pl.BlockSpec((tm * tiles, LANES), lambda i: (i, 0)),
        ],
        out_specs=pl.BlockSpec(memory_space=pl.ANY),
        out_shape=jax.ShapeDtypeStruct((n_blocks * bm * tiles, LANES), F32),
        scratch_shapes=[pltpu.VMEM((bm * tiles, LANES), F32),
                        pltpu.SemaphoreType.DMA(()), pltpu.SemaphoreType.DMA(())],
        compiler_params=_cparams(("arbitrary",)),
        name="moe_scatter",
    )(pos_flat, nvalid, x1t)


def _expert_kernel(be_ref, nv_ref, x_ref, wg_ref, wu_ref, wd_ref, y_ref, wg_s, wu_s, wd_s, *, tiles):
    i = pl.program_id(0)
    n_valid = nv_ref[i]

    @pl.when((i == 0) | (be_ref[i] != be_ref[jnp.maximum(i - 1, 0)]))
    def _():
        wg_s[...] = wg_ref[0].astype(BF16)
        wu_s[...] = wu_ref[0].astype(BF16)
        wd_s[...] = wd_ref[0].astype(BF16)

    @pl.when(n_valid == 0)
    def _():
        y_ref[...] = jnp.zeros(y_ref.shape, F32)

    @pl.when(n_valid > 0)
    def _():
        xb = _load_token_tiles(x_ref, x_ref.shape[0] // tiles, tiles).astype(BF16)
        g = jnp.dot(xb, wg_s[...], preferred_element_type=F32)
        u = jnp.dot(xb, wu_s[...], preferred_element_type=F32)
        h = g * jax.nn.sigmoid(g) * u
        _store_token_tiles(y_ref, jnp.dot(h.astype(BF16), wd_s[...], preferred_element_type=F32))


def _experts(block_expert, nvalid, xs, wg, wu, wd, bm):
    d, de = wg.shape[1], wg.shape[2]
    tiles = d // LANES
    blk_rows = bm * tiles
    nb = xs.shape[0] // blk_rows
    grid_spec = pltpu.PrefetchScalarGridSpec(
        num_scalar_prefetch=2,
        grid=(nb,),
        in_specs=[
            pl.BlockSpec((blk_rows, LANES), lambda i, be, nv: (i, 0)),
            pl.BlockSpec((1, d, de), lambda i, be, nv: (be[i], 0, 0)),
            pl.BlockSpec((1, d, de), lambda i, be, nv: (be[i], 0, 0)),
            pl.BlockSpec((1, de, d), lambda i, be, nv: (be[i], 0, 0)),
        ],
        out_specs=pl.BlockSpec((blk_rows, LANES), lambda i, be, nv: (i, 0)),
        scratch_shapes=[pltpu.VMEM((d, de), BF16), pltpu.VMEM((d, de), BF16), pltpu.VMEM((de, d), BF16)],
    )
    return pl.pallas_call(
        functools.partial(_expert_kernel, tiles=tiles),
        grid_spec=grid_spec,
        out_shape=jax.ShapeDtypeStruct(xs.shape, F32),
        compiler_params=_cparams(("arbitrary",)),
        name="moe_experts",
    )(block_expert, nvalid, xs, wg, wu, wd)


def _final_kernel(pos_ref, posn_ref, x1_ref, ri_ref, p_ref, ys_hbm, wpg_ref, wpp_ref, g2_ref, b2_ref,
                  pg_ref, out_ref, ybuf, sems, *, tm, tiles):
    step = pl.program_id(0)
    n_steps = pl.num_programs(0)
    slot = step % 2

    def row_copy(tab_ref, t, k, sl):
        src = pl.multiple_of(tab_ref[2 * t + k] * tiles, tiles)
        dst = pl.multiple_of(t * tiles, tiles)
        return pltpu.make_async_copy(ys_hbm.at[pl.ds(src, tiles), :], ybuf.at[sl, k, pl.ds(dst, tiles), :],
                                     sems.at[sl])

    def start_tile(tab_ref, sl):
        def body(t, carry):
            row_copy(tab_ref, t, 0, sl).start(priority=0)
            row_copy(tab_ref, t, 1, sl).start(priority=1)
            return carry

        lax.fori_loop(0, tm, body, 0, unroll=DMA_UNROLL)

    @pl.when(step == 0)
    def _():
        start_tile(pos_ref, 0)

    @pl.when(step + 1 < n_steps)
    def _():
        start_tile(posn_ref, 1 - slot)

    def wait(t, carry):
        row_copy(pos_ref, t, 0, slot).wait()
        row_copy(pos_ref, t, 1, slot).wait()
        return carry

    lax.fori_loop(0, tm, wait, 0, unroll=DMA_UNROLL)

    ri = ri_ref[...]
    y0 = _load_token_tiles(ybuf.at[slot, 0], tm, tiles)
    y1 = _load_token_tiles(ybuf.at[slot, 1], tm, tiles)
    ffn = ri[:, 2:3] * y0 + ri[:, 3:4] * y1
    x2 = _layer_norm(DN_ALPHA * x1_ref[...] + ffn, g2_ref[...], b2_ref[...])
    gate = jax.nn.sigmoid(jnp.dot(x2.astype(BF16), wpg_ref[...], preferred_element_type=F32))
    proj = jnp.dot(p_ref[...].astype(BF16), wpp_ref[...], preferred_element_type=F32)
    out_ref[...] = x2 + _rms_norm(gate * proj, pg_ref[...])


def _final(pos_flat, x1, ri, p, ys, wpg, wpp, g2, b2, pg):
    t, d = x1.shape
    tm = min(TOKEN_TILE, t)
    n_steps = t // tm
    tiles = d // LANES
    row = lambda w: pl.BlockSpec((tm, w), lambda i: (i, 0))
    consts = [wpg, wpp, g2, b2, pg]
    kern = functools.partial(_final_kernel, tm=tm, tiles=tiles)
    return pl.pallas_call(
        kern,
        grid=(n_steps,),
        in_specs=[pl.BlockSpec((2 * tm,), lambda i: (i,), memory_space=pltpu.SMEM),
                  pl.BlockSpec((2 * tm,), lambda i: (jnp.minimum(i + 1, n_steps - 1),), memory_space=pltpu.SMEM),
                  row(d), row(LANES), row(p.shape[1]), pl.BlockSpec(memory_space=pl.ANY)]
                 + [_full(a.shape) for a in consts],
        out_specs=row(d),
        out_shape=jax.ShapeDtypeStruct((t, d), F32),
        scratch_shapes=[pltpu.VMEM((2, 2, tm * tiles, LANES), F32), pltpu.SemaphoreType.DMA((2,))],
        compiler_params=_cparams(("arbitrary",)),
        name="combine_ln2_ple",
    )(pos_flat, pos_flat, x1, ri, p, ys, *consts)


def _lane_patterns(pair_layout):
    lane = np.arange(LANES)
    if pair_layout:
        valid = (lane < 2 * QK_ROPE).astype(np.float32)
    else:
        valid = np.ones(LANES, np.float32)
    sign = np.where((lane % QK_ROPE) < HALF_ROPE, -1.0, 1.0).astype(np.float32) * valid
    return lane % HALF_ROPE, sign, valid


def _prep_weights(w_in, w_uq, w_uk, w_uv, w_group, w_router):
    d = w_in.shape[0]
    ql = w_uq.shape[0]
    kvl, n_heads, _ = w_uk.shape
    o_kv, o_kr = ql, ql + kvl
    o_conv = o_kr + QK_ROPE
    c_conv = (w_in.shape[1] - o_conv - 2 * d) // 2
    o_ga = o_conv + 2 * c_conv
    o_gb = o_ga + d
    kr = w_in[:, o_kr:o_conv]
    zpad = jnp.zeros((d, LANES - 2 * QK_ROPE), w_in.dtype)
    win_r = jnp.concatenate([
        w_in[:, :o_kv], w_in[:, o_kv:o_kr], w_in[:, o_conv:o_ga], w_in[:, o_ga:o_gb], w_in[:, o_gb:],
        kr, kr, zpad], axis=1).astype(BF16)
    assert win_r.shape[1] == C_END and c_conv == C_CB - C_CA and kvl == C_CA - C_KV and ql == C_KV

    hd = QK_NOPE + QK_ROPE
    wq = w_uq.reshape(ql, n_heads, hd)
    nope = wq[:, :, :QK_NOPE].reshape(ql, n_heads * QK_NOPE)
    x1 = wq[:, :, QK_NOPE:QK_NOPE + HALF_ROPE]
    x2 = wq[:, :, QK_NOPE + HALF_ROPE:]
    rope_a = jnp.concatenate([x1, x2], axis=2)
    wuq_s = jnp.concatenate([nope, rope_a.reshape(ql, -1)], axis=1).astype(BF16)

    def pair_layout(r):
        r = r.reshape(ql, n_heads // 2, 2 * QK_ROPE)
        z = jnp.zeros((ql, n_heads // 2, LANES - 2 * QK_ROPE), r.dtype)
        return jnp.concatenate([r, z], axis=2).reshape(ql, -1)

    wuq_p = jnp.concatenate([nope, pair_layout(rope_a)], axis=1).astype(BF16)

    wuk_flat = w_uk.reshape(kvl, n_heads * QK_NOPE)
    wuv_flat = w_uv.reshape(kvl, n_heads * V_HEAD)
    wukv = jnp.concatenate([wuk_flat, wuv_flat], axis=1).astype(BF16)
    eye = jnp.eye(n_heads, dtype=w_uk.dtype)
    wk_bd = jnp.einsum('lhn,hg->hngl', w_uk, eye).reshape(n_heads * QK_NOPE, n_heads * kvl).astype(BF16)
    wv_bd = jnp.einsum('lhv,hg->hlgv', w_uv, eye).reshape(n_heads * kvl, n_heads * V_HEAD).astype(BF16)

    n_exp = w_router.shape[1]
    wgr = jnp.concatenate([w_group, w_router,
                           jnp.zeros((d, LANES - N_GROUPS - n_exp), w_group.dtype)], axis=1)
    wgh = wgr.astype(BF16)
    wgl = (wgr - wgh.astype(F32)).astype(BF16)
    return win_r, wuq_p, wuq_s, wukv, wk_bd, wv_bd, wgh, wgl


def _moe(x1t, ri, cnt, wg, wu, wd, n_exp):
    t = ri.shape[0]
    tiles = x1t.shape[0] // t
    n_assign = 2 * t
    bm = MOE_BLOCK if n_assign >= 2 * MOE_BLOCK * n_exp else MOE_BLOCK_MIN
    n_blocks = (n_assign + n_exp * (bm - 1) + bm - 1) // bm
    pos, be = _route(ri, cnt, n_blocks=n_blocks, n_exp=n_exp, bm=bm)
    pos_flat = pos[:, :2].reshape(-1)
    block_expert = be[:n_blocks, 0]
    nvalid = be[:n_blocks, 1]
    xs = _scatter(pos_flat, nvalid, x1t, n_blocks, tiles, bm)
    ys = _experts(block_expert, nvalid, xs, wg, wu, wd, bm)
    return pos_flat, ys


def kernel(x_prompt, x_sample, cache_ckv, cache_krope, state_conv, page_table, p_prompt, p_sample, w_in, q_norm_g, w_uq, kv_norm_g, w_uk, w_uv, w_o_attn, w_dw, b_dw, conv_ln_g, conv_ln_b, w_pw2, w_out, ln1_g, ln1_b, w_group, w_router, w_gate, w_up, w_down, ln2_g, ln2_b, w_ple_gate, w_ple_proj, ple_norm_g):
    assert w_in.shape[0] == DEPTH
    b, s, d = x_prompt.shape
    bd, sd, _ = x_sample.shape
    assert sd == 1
    n_pages = page_table.shape[1]
    page_rows = cache_ckv.shape[2]
    past = n_pages * page_rows
    n_heads = w_uk.shape[2]
    kvl = w_uk.shape[1]
    width = w_dw.shape[1]
    n_exp = w_router.shape[2]
    c_conv = w_dw.shape[2]

    win_r, wuq_p, wuq_s, wukv, wk_bd, wv_bd, wgh, wgl = _prep_weights(
        w_in[0], w_uq[0], w_uk[0], w_uv[0], w_group[0], w_router[0])
    qg, kvg = q_norm_g, kv_norm_g
    wo, wpw, wout = w_o_attn[0].astype(BF16), w_pw2[0].astype(BF16), w_out[0].astype(BF16)
    wgate, wup, wdown = w_gate[0], w_up[0], w_down[0]
    wpg, wpp = w_ple_gate[0].astype(BF16), w_ple_proj[0].astype(BF16)
    w_dw_pad = jnp.concatenate([w_dw[0], jnp.zeros((CONV_HALO - width, c_conv), F32)], axis=0)

    inv16 = 1.0 / (ROPE_THETA ** (jnp.arange(HALF_ROPE, dtype=F32) / HALF_ROPE))

    def tables(pair_layout, n_rows, offset, step):
        idx, sign, valid = _lane_patterns(pair_layout)
        inv_l = (inv16[idx] * valid)[None, :]
        return _rope_tables(n_rows, offset, step, inv_l, jnp.asarray(sign)[None, :], jnp.asarray(valid)[None, :])

    def trunk_tail(x2d, o, y_or_u, sa, sb, p2d, conv=None):
        x1, x1t, ri, cnt = _mix(x2d, o, y_or_u, sa, sb, wo, wpw, wout, conv_ln_g, conv_ln_b, ln1_g, ln1_b,
                           wgh, wgl, conv=conv)
        pos_flat, ys = _moe(x1t, ri, cnt, wgate, wup, wdown, n_exp)
        return _final(pos_flat, x1, ri, p2d, ys, wpg, wpp, ln2_g, ln2_b, ple_norm_g)

    xs_ = x_sample.reshape(bd, d)
    cos_s, sin_s = tables(False, SUBLANES, past, 0)
    qn_s, qr_s, ckv_s, kr_s, u_s, sa_s, sb_s = _inproj(
        xs_, win_r, qg, kvg, wuq_s, None, cos_s, sin_s, n_rep=n_heads * QK_ROPE // LANES, seq=1, make_kv=False)
    q_lat = _dense(qn_s, wk_bd, F32).reshape(bd, n_heads, kvl)
    o_lat = _sample_attention(page_table, q_lat, qr_s.reshape(bd, n_heads, QK_ROPE),
                              ckv_s.reshape(bd, 1, kvl), kr_s.reshape(bd, 1, QK_ROPE),
                              cache_ckv[0], jnp.swapaxes(cache_krope[0], 1, 2))
    o_s = _dense(o_lat.reshape(bd, n_heads * kvl), wv_bd, BF16)
    ext_s = jnp.concatenate([state_conv[0], u_s[:, None, :]], axis=1)
    y_s = _conv_sample(jnp.transpose(ext_s, (1, 0, 2)), w_dw_pad, b_dw, width=width)
    out_s = trunk_tail(xs_, o_s, y_s, sa_s, sb_s, p_sample[0].reshape(bd, -1))
    new_conv_s = ext_s[:, 1:, :]

    xp = x_prompt.reshape(b * s, d)
    cos_p, sin_p = tables(True, s, 0, 1)
    qn, qr, kn, v, krd, ckv_p, kr_p, u_p, sa_p, sb_p = _inproj(
        xp, win_r, qg, kvg, wuq_p, wukv, cos_p, sin_p, n_rep=n_heads // 2, seq=s, make_kv=True)
    o_p = _prompt_attention(qn, qr, kn, krd, v, batch=b, seq=s)
    out_p = trunk_tail(xp, o_p, u_p, sa_p, sb_p, p_prompt[0].reshape(b * s, -1),
                       conv=(w_dw_pad, b_dw, s, width))
    u_p3 = u_p.reshape(b, s, c_conv)
    new_conv_p = u_p3[:, s - (width - 1):, :]

    return (out_p.reshape(b, s, d), out_s.reshape(bd, 1, d),
            ckv_p.reshape(1, b, s, kvl), kr_p.reshape(1, b, s, QK_ROPE), new_conv_p[None],
            ckv_s.reshape(1, bd, 1, kvl), kr_s.reshape(1, bd, 1, QK_ROPE), new_conv_s[None])
```

```python
import functools

import numpy as np
import jax
import jax.numpy as jnp
from jax import lax
from jax.experimental import pallas as pl
from jax.experimental.pallas import tpu as pltpu

F32 = jnp.float32
BF16 = jnp.bfloat16

QK_NOPE = 64
QK_ROPE = 32
HALF_ROPE = QK_ROPE // 2
V_HEAD = 64
ROPE_THETA = 10000.0
ATTN_SCALE = (QK_NOPE + QK_ROPE) ** -0.5
LOG2E = 1.4426950408889634
N_GROUPS = 4
EXPERTS_PER_GROUP = 8
DEPTH = 1
DN_ALPHA = (2 * DEPTH) ** 0.25
LN_EPS = 1e-5
RMS_EPS = 1e-6
NEG_INF = -1e30

LANES = 128
SUBLANES = 8
VMEM_LIMIT = 56 * 1024 * 1024

TOKEN_TILE = 512
ATTN_TILE = 512
ATTN_PHASE_GROUP = 4
CONV_CHUNK = 64
CONV_HALO = 32
MOE_BLOCK = 512
MOE_BLOCK_MIN = 128
PAGES_PER_CHUNK = 16
SEQS_PER_STEP = 4
DMA_UNROLL = 8


def _cparams(sem, vmem=VMEM_LIMIT):
    return pltpu.CompilerParams(dimension_semantics=sem, vmem_limit_bytes=vmem)


def _full(shape):
    n = len(shape)
    return pl.BlockSpec(shape, lambda *_: (0,) * n)


def _layer_norm(x, g, b):
    mu = jnp.mean(x, axis=-1, keepdims=True)
    xc = x - mu
    var = jnp.mean(xc * xc, axis=-1, keepdims=True)
    return xc * lax.rsqrt(var + LN_EPS) * g + b


def _rms_norm(x, g):
    return x * lax.rsqrt(jnp.mean(x * x, axis=-1, keepdims=True) + RMS_EPS) * g


def _rope_table_kernel(inv_ref, sign_ref, valid_ref, cos_ref, sin_ref, *, offset, step):
    rows = cos_ref.shape[0]
    r = lax.broadcasted_iota(jnp.int32, (rows, LANES), 0) + pl.program_id(0) * rows
    pos = (offset + step * r).astype(F32)
    ang = pos * inv_ref[...]
    cos_ref[...] = jnp.cos(ang) * valid_ref[...]
    sin_ref[...] = jnp.sin(ang) * sign_ref[...]


def _rope_tables(n_rows, offset, step, inv_l, sign_l, valid_l):
    tr = min(n_rows, 512)
    kern = functools.partial(_rope_table_kernel, offset=offset, step=step)
    return pl.pallas_call(
        kern,
        grid=(n_rows // tr,),
        in_specs=[_full((1, LANES))] * 3,
        out_specs=[pl.BlockSpec((tr, LANES), lambda i: (i, 0))] * 2,
        out_shape=[jax.ShapeDtypeStruct((n_rows, LANES), F32)] * 2,
        compiler_params=_cparams(("arbitrary",)),
        name="rope_tables",
    )(inv_l, sign_l, valid_l)


C_Q, C_KV, C_CA, C_CB, C_GA, C_GB, C_KA, C_END = 0, 384, 640, 1152, 1664, 2688, 3712, 3840


def _rotary_partner(a):
    lane = lax.broadcasted_iota(jnp.int32, (a.shape[0], LANES), 1)
    first_half = (lane % QK_ROPE) < HALF_ROPE
    blocks = []
    for c in range(a.shape[1] // LANES):
        blk = a[:, c * LANES:(c + 1) * LANES]
        blocks.append(jnp.where(first_half, pltpu.roll(blk, LANES - HALF_ROPE, 1), pltpu.roll(blk, HALF_ROPE, 1)))
    return blocks[0] if len(blocks) == 1 else jnp.concatenate(blocks, axis=1)


def _inproj_kernel(x_ref, win_ref, qg_ref, kvg_ref, wuq_ref, *rest, n_rep, make_kv, bcast, q_scale):
    if make_kv:
        (wukv_ref, cos_ref, sin_ref, qn_o, qr_o, kn_o, v_o, krd_o, ckv_o, kr_o, u_o, sa_o, sb_o) = rest
    else:
        (cos_ref, sin_ref, qn_o, qr_o, ckv_o, kr_o, u_o, sa_o, sb_o) = rest
    xb = x_ref[...].astype(BF16)

    def proj(a, b):
        return jnp.dot(xb, win_ref[:, a:b], preferred_element_type=F32)

    if bcast:
        c1, s1 = cos_ref[0:1, :], sin_ref[0:1, :]
    else:
        c1, s1 = cos_ref[...], sin_ref[...]
    cn = jnp.concatenate([c1] * n_rep, axis=1)
    sn = jnp.concatenate([s1] * n_rep, axis=1)
    d_nope = qn_o.shape[1]
    nr = LANES * n_rep

    cqn = _rms_norm(proj(C_Q, C_KV), qg_ref[...])
    qall = jnp.dot(cqn.astype(BF16), wuq_ref[...], preferred_element_type=F32)
    qn_o[...] = (qall[:, :d_nope] * q_scale).astype(qn_o.dtype)
    qa = qall[:, d_nope:d_nope + nr]
    qb = _rotary_partner(qa)
    qr_o[...] = ((qa * cn + qb * sn) * q_scale).astype(qr_o.dtype)

    ckvn = _rms_norm(proj(C_KV, C_CA), kvg_ref[...])
    ckv_o[...] = ckvn
    kra = proj(C_KA, C_END)
    krot = kra * c1 + _rotary_partner(kra) * s1
    kr_o[...] = krot[:, :QK_ROPE]
    if make_kv:
        kv = jnp.dot(ckvn.astype(BF16), wukv_ref[...], preferred_element_type=F32)
        half = kv.shape[1] // 2
        kn_o[...] = kv[:, :half].astype(BF16)
        v_o[...] = kv[:, half:].astype(BF16)
        krd_o[...] = krot.astype(BF16)

    u_o[...] = proj(C_CA, C_CB) * jax.nn.sigmoid(proj(C_CB, C_GA))
    sa_o[...] = jax.nn.sigmoid(proj(C_GA, C_GB)).astype(BF16)
    sb_o[...] = jax.nn.sigmoid(proj(C_GB, C_KA)).astype(BF16)


def _inproj(x, win_r, qg, kvg, wuq_r, wukv, cos_t, sin_t, *, n_rep, seq, make_kv):
    t, d = x.shape
    tm = min(TOKEN_TILE, t)
    d_nope = wuq_r.shape[1] - LANES * n_rep
    nr = LANES * n_rep
    c_conv = C_CB - C_CA
    kvl = C_CA - C_KV
    bcast = seq < tm
    if bcast:
        tab_spec = _full(cos_t.shape)
    else:
        nst = seq // tm
        tab_spec = pl.BlockSpec((tm, LANES), lambda i: (i % nst, 0))
    row = lambda w: pl.BlockSpec((tm, w), lambda i: (i, 0))
    in_specs = [row(d), _full(win_r.shape), _full(qg.shape), _full(kvg.shape), _full(wuq_r.shape)]
    args = [x, win_r, qg, kvg, wuq_r]
    if make_kv:
        in_specs.append(_full(wukv.shape))
        args.append(wukv)
    in_specs += [tab_spec, tab_spec]
    args += [cos_t, sin_t]
    q_dt = BF16 if make_kv else F32
    outs = [(d_nope, BF16), (nr, q_dt)]
    if make_kv:
        outs += [(d_nope, BF16), (d_nope, BF16), (LANES, BF16)]
    outs += [(kvl, F32), (QK_ROPE, F32), (c_conv, F32), (d, BF16), (d, BF16)]
    q_scale = ATTN_SCALE * LOG2E if make_kv else ATTN_SCALE
    kern = functools.partial(_inproj_kernel, n_rep=n_rep, make_kv=make_kv, bcast=bcast, q_scale=q_scale)
    return pl.pallas_call(
        kern,
        grid=(t // tm,),
        in_specs=in_specs,
        out_specs=[row(w) for w, _ in outs],
        out_shape=[jax.ShapeDtypeStruct((t, w), dt) for w, dt in outs],
        compiler_params=_cparams(("arbitrary",)),
        name="inproj_kv" if make_kv else "inproj_q",
    )(*args)


def _softmax_stats(state, s):
    m, l, _ = state
    m_new = jnp.maximum(m, jnp.max(s, axis=1, keepdims=True))
    alpha = jnp.exp2(m - m_new)
    p = jnp.exp2(s - jnp.concatenate([m_new] * (s.shape[1] // LANES), axis=1))
    return m_new, alpha * l + jnp.sum(p, axis=1, keepdims=True), alpha, p.astype(BF16)


def _pattn_kernel(qn_ref, qr_ref, kn_ref, krd_ref, v_ref, o_ref, m_s, l_s, acc_s, *, blk, nq):
    half = blk // 2
    lane = lax.broadcasted_iota(jnp.int32, (blk, LANES), 1)
    row_a = lax.broadcasted_iota(jnp.int32, (blk, half), 0)
    col_a = lax.broadcasted_iota(jnp.int32, (blk, half), 1)
    row_b = lax.broadcasted_iota(jnp.int32, (half, half), 0)
    col_b = lax.broadcasted_iota(jnp.int32, (half, half), 1)
    dn_t = (((1,), (1,)), ((), ()))

    def q_heads(i):
        qn = qn_ref[i * blk:(i + 1) * blk, :]
        qr = qr_ref[i * blk:(i + 1) * blk, :]
        zero = jnp.zeros_like(qn)
        return (
            jnp.concatenate([jnp.where(lane < QK_NOPE, qn, zero), jnp.where(lane < QK_ROPE, qr, zero)], axis=1),
            jnp.concatenate([jnp.where(lane >= QK_NOPE, qn, zero),
                             jnp.where((lane >= QK_ROPE) & (lane < 2 * QK_ROPE), qr, zero)], axis=1),
        )

    qh = [q_heads(i) for i in range(nq)]
    pairs = [(i, h) for i in range(nq) for h in range(2)]
    state = {ih: (jnp.full((blk, LANES), NEG_INF, F32), jnp.zeros((blk, LANES), F32),
                  jnp.zeros((blk, LANES), F32)) for ih in pairs}
    for j in range(nq):
        k0 = j * blk
        k_a = jnp.concatenate([kn_ref[k0:k0 + half, :], krd_ref[k0:k0 + half, :]], axis=1)
        k_b = jnp.concatenate([kn_ref[k0 + half:k0 + blk, :], krd_ref[k0 + half:k0 + blk, :]], axis=1)
        k_full = jnp.concatenate([k_a, k_b], axis=0)
        v_a = v_ref[k0:k0 + half, :]
        v_b = v_ref[k0 + half:k0 + blk, :]
        v_full = v_ref[k0:k0 + blk, :]
        work = [((j, h), k_a, v_a, True) for h in range(2)]
        work += [((i, h), k_full, v_full, False) for i in range(j + 1, nq) for h in range(2)]
        for g0 in range(0, len(work), ATTN_PHASE_GROUP):
            group = work[g0:g0 + ATTN_PHASE_GROUP]
            scores = []
            for (i, h), k, _, diag in group:
                s = lax.dot_general(qh[i][h], k, dn_t, preferred_element_type=F32)
                scores.append(jnp.where(col_a <= row_a, s, NEG_INF) if diag else s)
            stats = [_softmax_stats(state[ih], s) for (ih, _, _, _), s in zip(group, scores)]
            for (ih, _, v, _), (m_new, l, alpha, p) in zip(group, stats):
                state[ih] = (m_new, l, alpha * state[ih][2] + jnp.dot(p, v, preferred_element_type=F32))
        for h in range(2):
            m_s[j, h], l_s[j, h], acc_s[j, h] = state[(j, h)]
        low = [(m_s[j, h, half:, :], l_s[j, h, half:, :], acc_s[j, h, half:, :]) for h in range(2)]
        scores = [jnp.where(col_b <= row_b,
                            lax.dot_general(qh[j][h][half:], k_b, dn_t, preferred_element_type=F32), NEG_INF)
                  for h in range(2)]
        stats = [_softmax_stats(low[h], scores[h]) for h in range(2)]
        for h in range(2):
            _, l, alpha, p = stats[h]
            l_s[j, h, half:, :] = l
            acc_s[j, h, half:, :] = alpha * low[h][2] + jnp.dot(p, v_b, preferred_element_type=F32)
        o = jnp.where(lane < V_HEAD, acc_s[j, 0] / l_s[j, 0], acc_s[j, 1] / l_s[j, 1])
        o_ref[k0:k0 + blk, :] = o.astype(o_ref.dtype)


def _prompt_attention(qn, qr, kn, krd, v, *, batch, seq):
    t, d = qn.shape
    blk = min(ATTN_TILE, seq)
    nq = seq // blk
    n_pairs = d // LANES
    kspec = pl.BlockSpec((seq, LANES), lambda b, j: (b, j))
    kern = functools.partial(_pattn_kernel, blk=blk, nq=nq)
    return pl.pallas_call(
        kern,
        grid=(batch, n_pairs),
        in_specs=[kspec, kspec, kspec, pl.BlockSpec((seq, LANES), lambda b, j: (b, 0)), kspec],
        out_specs=kspec,
        out_shape=jax.ShapeDtypeStruct((t, d), BF16),
        scratch_shapes=[pltpu.VMEM((nq, 2, blk, LANES), F32)] * 3,
        compiler_params=_cparams(("arbitrary",) * 2),
        name="prompt_attention",
    )(qn, qr, kn, krd, v)


def _dense_kernel(a_ref, w_ref, o_ref):
    o_ref[...] = jnp.dot(a_ref[...].astype(BF16), w_ref[...], preferred_element_type=F32).astype(o_ref.dtype)


def _dense(a, w, out_dtype):
    m, n = a.shape[0], w.shape[1]
    return pl.pallas_call(
        _dense_kernel,
        grid=(1,),
        in_specs=[_full(a.shape), _full(w.shape)],
        out_specs=_full((m, n)),
        out_shape=jax.ShapeDtypeStruct((m, n), out_dtype),
        compiler_params=_cparams(("arbitrary",)),
        name="dense",
    )(a, w)


def _sattn_kernel(pt_ref, ptn_ref, ql_ref, qr_ref, cnew_ref, knew_ref, ckv_hbm, krt_hbm, o_ref,
                  cbuf, kbuf, ckb, sems, *, n_chunks, ppc, nb):
    step = pl.program_id(0)
    n_steps = pl.num_programs(0)
    kvl = cbuf.shape[4]
    kc = ppc * cbuf.shape[3]
    n_heads = ql_ref.shape[1]
    dn_t = (((1,), (1,)), ((), ()))

    def chunk_copies(tab_ref, chunk, slot):
        out = []
        for b in range(nb):
            for pg in range(ppc):
                page = tab_ref[b, 0, chunk * ppc + pg]
                out.append(pltpu.make_async_copy(ckv_hbm.at[page], cbuf.at[slot, b, pg], sems.at[0, slot]))
                out.append(pltpu.make_async_copy(krt_hbm.at[page], kbuf.at[slot, b, pg], sems.at[1, slot]))
        return out

    def start_chunk(tab_ref, chunk, slot):
        for cp in chunk_copies(tab_ref, chunk, slot):
            cp.start()

    @pl.when(step == 0)
    def _():
        start_chunk(pt_ref, 0, 0)

    first = step * n_chunks
    qls = [ql_ref[b].astype(BF16) for b in range(nb)]
    qrs = [qr_ref[b].astype(BF16) for b in range(nb)]

    def body(c, carry):
        slot = (first + c) % 2

        @pl.when(c + 1 < n_chunks)
        def _():
            start_chunk(pt_ref, c + 1, 1 - slot)

        @pl.when((c + 1 == n_chunks) & (step + 1 < n_steps))
        def _():
            start_chunk(ptn_ref, 0, 1 - slot)

        for cp in chunk_copies(pt_ref, c, slot):
            cp.wait()
        for b in range(nb):
            ckb[b] = cbuf[slot, b].reshape(kc, kvl).astype(BF16)
        scores = []
        for b in range(nb):
            krt = jnp.concatenate([kbuf[slot, b, pg] for pg in range(ppc)], axis=1).astype(BF16)
            scores.append(lax.dot_general(qls[b], ckb[b], dn_t, preferred_element_type=F32)
                          + jnp.dot(qrs[b], krt, preferred_element_type=F32))
        stats = []
        for b in range(nb):
            m, l, _ = carry[3 * b:3 * b + 3]
            m_new = jnp.maximum(m, jnp.max(scores[b], axis=1, keepdims=True))
            alpha = jnp.exp(m - m_new)
            p = jnp.exp(scores[b] - m_new)
            stats.append((m_new, alpha * l + jnp.sum(p, axis=1, keepdims=True), alpha, p.astype(BF16)))
        new = []
        for b in range(nb):
            m_new, l, alpha, p = stats[b]
            acc = alpha * carry[3 * b + 2] + jnp.dot(p, ckb[b], preferred_element_type=F32)
            new += [m_new, l, acc]
        return tuple(new)

    init = (jnp.full((n_heads, 1), NEG_INF, F32), jnp.zeros((n_heads, 1), F32),
            jnp.zeros((n_heads, kvl), F32)) * nb
    carry = lax.fori_loop(0, n_chunks, body, init)

    for b in range(nb):
        m, l, acc = carry[3 * b:3 * b + 3]
        cnew = cnew_ref[b]
        knew = knew_ref[b]
        s_new = (jnp.sum(ql_ref[b] * cnew, axis=1, keepdims=True)
                 + jnp.sum(qr_ref[b] * knew, axis=1, keepdims=True))
        m_new = jnp.maximum(m, s_new)
        alpha = jnp.exp(m - m_new)
        p_new = jnp.exp(s_new - m_new)
        l = alpha * l + p_new
        acc = alpha * acc + p_new * cnew
        o_ref[b] = acc / l


def _sample_attention(page_table, q_lat, q_rope, c_new, k_new, cache_ckv, cache_krope_t):
    bd, n_heads, kvl = q_lat.shape
    n_pages = page_table.shape[1]
    page_rows = cache_ckv.shape[1]
    ppc = min(PAGES_PER_CHUNK, n_pages)
    n_chunks = n_pages // ppc
    nb = min(SEQS_PER_STEP, bd)
    n_steps = bd // nb
    pt3 = page_table.reshape(bd, 1, n_pages)
    per_b = lambda s: pl.BlockSpec((nb,) + s, lambda i: (i, 0, 0))
    kern = functools.partial(_sattn_kernel, n_chunks=n_chunks, ppc=ppc, nb=nb)
    return pl.pallas_call(
        kern,
        grid=(n_steps,),
        in_specs=[
            pl.BlockSpec((nb, 1, n_pages), lambda i: (i, 0, 0), memory_space=pltpu.SMEM),
            pl.BlockSpec((nb, 1, n_pages), lambda i: (jnp.minimum(i + 1, n_steps - 1), 0, 0),
                         memory_space=pltpu.SMEM),
            per_b((n_heads, kvl)), per_b((n_heads, QK_ROPE)), per_b((1, kvl)), per_b((1, QK_ROPE)),
            pl.BlockSpec(memory_space=pl.ANY), pl.BlockSpec(memory_space=pl.ANY),
        ],
        out_specs=per_b((n_heads, kvl)),
        out_shape=jax.ShapeDtypeStruct((bd, n_heads, kvl), F32),
        scratch_shapes=[
            pltpu.VMEM((2, nb, ppc, page_rows, kvl), F32),
            pltpu.VMEM((2, nb, ppc, QK_ROPE, page_rows), F32),
            pltpu.VMEM((nb, ppc * page_rows, kvl), BF16),
            pltpu.SemaphoreType.DMA((2, 2)),
        ],
        compiler_params=_cparams(("arbitrary",)),
        name="sample_attention",
    )(pt3, pt3, q_lat, q_rope, c_new, k_new, cache_ckv, cache_krope_t)


def _conv_rows(u_ref, halo_ref, w_ref, b_ref, y_ref, ext_s, z_s, first_tile, width):
    ts, ch = u_ref.shape
    left = jnp.where(first_tile, jnp.zeros_like(halo_ref), halo_ref[...])
    ext_s[0:CONV_HALO, :] = left
    ext_s[CONV_HALO:CONV_HALO + ts, :] = u_ref[...]
    ext_s[CONV_HALO + ts:CONV_HALO + ts + SUBLANES, :] = jnp.zeros((SUBLANES, ch), F32)
    off = CONV_HALO - (width - 1)
    win = CONV_CHUNK + SUBLANES
    for c in range(ts // CONV_CHUNK):
        r0 = c * CONV_CHUNK
        acc = jnp.broadcast_to(b_ref[...], (CONV_CHUNK, ch))
        for r in range(SUBLANES):
            z = None
            for a in range((off + width + SUBLANES - 1) // SUBLANES):
                k = SUBLANES * a + r - off
                if 0 <= k < width:
                    term = w_ref[k:k + 1, :] * ext_s[r0 + SUBLANES * a:r0 + SUBLANES * a + win, :]
                    z = term if z is None else z + term
            if r == 0:
                acc = acc + z[:CONV_CHUNK]
            else:
                z_s[r] = z
                acc = acc + z_s[r, r:r + CONV_CHUNK, :]
        y_ref[r0:r0 + CONV_CHUNK, :] = acc


def _conv_sample_kernel(ext_ref, w_ref, b_ref, y_ref, *, width):
    acc = jnp.broadcast_to(b_ref[...], y_ref.shape)
    for k in range(width):
        acc = acc + w_ref[k:k + 1, :] * ext_ref[k]
    y_ref[...] = acc


def _conv_sample(ext_t, w_pad, b, *, width):
    _, bd, c = ext_t.shape
    kern = functools.partial(_conv_sample_kernel, width=width)
    return pl.pallas_call(
        kern,
        grid=(1,),
        in_specs=[_full(ext_t.shape), _full(w_pad.shape), _full(b.shape)],
        out_specs=_full((bd, c)),
        out_shape=jax.ShapeDtypeStruct((bd, c), F32),
        compiler_params=_cparams(("arbitrary",)),
        name="conv_sample",
    )(ext_t, w_pad, b)


def _store_token_tiles(ref, x):
    rows, d = x.shape
    tiles = d // LANES
    for c in range(tiles):
        ref[pl.ds(c, rows, stride=tiles), :] = x[:, c * LANES:(c + 1) * LANES]


def _load_token_tiles(ref, rows, tiles):
    return jnp.concatenate([ref[pl.ds(c, rows, stride=tiles), :] for c in range(tiles)], axis=1)


def _mix_kernel(x_ref, o_ref, *rest, conv_tiles, width):
    if conv_tiles:
        (u_ref, halo_ref, wdw_ref, bdw_ref, sa_ref, sb_ref, wo_ref, wpw_ref, wout_ref, cg_ref, cb_ref,
         g1_ref, b1_ref, wgh_ref, wgl_ref, x1_ref, x1t_ref, ri_ref, cnt_ref, ext_s, z_s, y_ref) = rest
        first_tile = pl.program_id(0) % conv_tiles == 0
        _conv_rows(u_ref, halo_ref, wdw_ref, bdw_ref, y_ref, ext_s, z_s, first_tile, width)
    else:
        (y_ref, sa_ref, sb_ref, wo_ref, wpw_ref, wout_ref, cg_ref, cb_ref,
         g1_ref, b1_ref, wgh_ref, wgl_ref, x1_ref, x1t_ref, ri_ref, cnt_ref) = rest
    branch_b = jnp.dot(o_ref[...], wo_ref[...], preferred_element_type=F32)
    z = _layer_norm(y_ref[...], cg_ref[...], cb_ref[...])
    z = z * jax.nn.sigmoid(z)
    branch_a = jnp.dot(z.astype(BF16), wpw_ref[...], preferred_element_type=F32)
    mixin = sa_ref[...].astype(F32) * branch_a + sb_ref[...].astype(F32) * branch_b
    mix = jnp.dot(mixin.astype(BF16), wout_ref[...], preferred_element_type=F32)
    x1 = _layer_norm(DN_ALPHA * x_ref[...] + mix, g1_ref[...], b1_ref[...])
    x1_ref[...] = x1
    _store_token_tiles(x1t_ref, x1)

    x_hi = x1.astype(BF16)
    x_lo = (x1 - x_hi.astype(F32)).astype(BF16)
    lg = (jnp.dot(x_hi, wgh_ref[...], preferred_element_type=F32)
          + jnp.dot(x_lo, wgh_ref[...], preferred_element_type=F32)
          + jnp.dot(x_hi, wgl_ref[...], preferred_element_type=F32))
    n_exp = N_GROUPS * EXPERTS_PER_GROUP
    lane = lax.broadcasted_iota(jnp.int32, lg.shape, 1)
    lane_f = lane.astype(F32)
    big = float(LANES)
    gmask = lane < N_GROUPS
    lgm = jnp.where(gmask, lg, NEG_INF)
    gmax = jnp.max(lgm, axis=1, keepdims=True)
    gidx = jnp.min(jnp.where(lgm == gmax, lane_f, big), axis=1, keepdims=True)
    pg_sel = 1.0 / jnp.sum(jnp.where(gmask, jnp.exp(lgm - gmax), 0.0), axis=1, keepdims=True)
    egroup = jnp.floor((lane_f - N_GROUPS) * (1.0 / EXPERTS_PER_GROUP))
    emask = (lane >= N_GROUPS) & (lane < N_GROUPS + n_exp) & (egroup == gidx)
    le = jnp.where(emask, lg, NEG_INF)
    v1 = jnp.max(le, axis=1, keepdims=True)
    i1 = jnp.min(jnp.where(le == v1, lane_f, big), axis=1, keepdims=True)
    le2 = jnp.where(lane_f == i1, NEG_INF, le)
    v2 = jnp.max(le2, axis=1, keepdims=True)
    i2 = jnp.min(jnp.where(le2 == v2, lane_f, big), axis=1, keepdims=True)
    e = jnp.exp(v2 - v1)
    gate1 = pg_sel / (1.0 + e)
    gate2 = pg_sel * e / (1.0 + e)
    ri = jnp.where(lane == 0, i1 - N_GROUPS,
                   jnp.where(lane == 1, i2 - N_GROUPS,
                             jnp.where(lane == 2, gate1, jnp.where(lane == 3, gate2, 0.0))))
    ri_ref[...] = ri

    chosen = (lane_f == i1 - N_GROUPS) | (lane_f == i2 - N_GROUPS)

    @pl.when(pl.program_id(0) == 0)
    def _():
        cnt_ref[...] = jnp.zeros(cnt_ref.shape, F32)

    cnt_ref[0:1, :] = cnt_ref[0:1, :] + jnp.sum(jnp.where(chosen, 1.0, 0.0), axis=0, keepdims=True)


def _mix(x, o, y_or_u, sa, sb, wo, wpw, wout, cg, cb, g1, b1, wgh, wgl, conv=None):
    t, d = x.shape
    tm = min(TOKEN_TILE, t)
    c = y_or_u.shape[1]
    row = lambda w: pl.BlockSpec((tm, w), lambda i: (i, 0))
    consts = [wo, wpw, wout, cg, cb, g1, b1, wgh, wgl]
    tiles = d // LANES
    if conv is None:
        kern = functools.partial(_mix_kernel, conv_tiles=0, width=0)
        lead_specs, lead, scratch = [row(c)], [y_or_u], []
    else:
        w_pad, b_dw, seq, width = conv
        assert seq % tm == 0
        hb = tm // CONV_HALO
        kern = functools.partial(_mix_kernel, conv_tiles=seq // tm, width=width)
        lead_specs = [row(c), pl.BlockSpec((CONV_HALO, c), lambda i: (jnp.maximum(i * hb - 1, 0), 0)),
                      _full(w_pad.shape), _full(b_dw.shape)]
        lead = [y_or_u, y_or_u, w_pad, b_dw]
        scratch = [pltpu.VMEM((CONV_HALO + tm + SUBLANES, c), F32),
                   pltpu.VMEM((SUBLANES, CONV_CHUNK + SUBLANES, c), F32), pltpu.VMEM((tm, c), F32)]
    return pl.pallas_call(
        kern,
        grid=(t // tm,),
        in_specs=[row(d), row(d)] + lead_specs + [row(d), row(d)] + [_full(a.shape) for a in consts],
        out_specs=[row(d), pl.BlockSpec((tm * tiles, LANES), lambda i: (i, 0)), row(LANES),
                   _full((SUBLANES, LANES))],
        out_shape=[jax.ShapeDtypeStruct((t, d), F32), jax.ShapeDtypeStruct((t * tiles, LANES), F32),
                   jax.ShapeDtypeStruct((t, LANES), F32), jax.ShapeDtypeStruct((SUBLANES, LANES), F32)],
        scratch_shapes=scratch,
        compiler_params=_cparams(("arbitrary",)),
        name="mix_ln1_router",
    )(x, o, *lead, sa, sb, *consts)


def _route_kernel(ri_ref, cnt_ref, pos_ref, be_ref, base_s, *, tm, bm, n_exp):
    i = pl.program_id(0)
    lane_f = lax.broadcasted_iota(jnp.int32, (tm, LANES), 1).astype(F32)
    ri = ri_ref[...]
    oh0 = lane_f == ri[:, 0:1]
    oh1 = lane_f == ri[:, 1:2]
    c = jnp.where(oh0 | oh1, 1.0, 0.0)
    csum = jnp.sum(c, axis=0, keepdims=True)

    @pl.when(i == 0)
    def _():
        cnt = cnt_ref[...]
        pc = jnp.floor((cnt + (bm - 1)) * (1.0 / bm)) * bm
        r = lax.broadcasted_iota(jnp.int32, (LANES, LANES), 0)
        cc = lax.broadcasted_iota(jnp.int32, (LANES, LANES), 1)
        upper = jnp.where(r <= cc, 1.0, 0.0)
        pend = jnp.dot(pc, upper, precision=lax.Precision.HIGHEST, preferred_element_type=F32)
        base_s[...] = pend - pc
        nbp = be_ref.shape[0]
        bstart = (lax.broadcasted_iota(jnp.int32, (nbp, LANES), 0) * bm).astype(F32)
        lane_b = lax.broadcasted_iota(jnp.int32, (nbp, LANES), 1)
        hit = jnp.where((lane_b < n_exp) & (pend[0:1, :] <= bstart), 1.0, 0.0)
        be = jnp.minimum(jnp.sum(hit, axis=1, keepdims=True), float(n_exp - 1))
        pstart = pend[0:1, :] - pc[0:1, :]
        span = jnp.minimum(pstart + cnt[0:1, :], bstart + bm) - jnp.maximum(pstart, bstart)
        nvalid = jnp.sum(jnp.where(lane_b < n_exp, jnp.maximum(span, 0.0), 0.0), axis=1, keepdims=True)
        be_ref[...] = jnp.where(lane_b == 0, be, jnp.where(lane_b == 1, nvalid, 0.0)).astype(jnp.int32)

    r = lax.broadcasted_iota(jnp.int32, (tm, tm), 0)
    cc = lax.broadcasted_iota(jnp.int32, (tm, tm), 1)
    lower = jnp.where(cc < r, 1.0, 0.0).astype(BF16)
    cum = jnp.dot(lower, c.astype(BF16), preferred_element_type=F32)
    tot = cum + base_s[0:1, :]
    p0 = jnp.sum(jnp.where(oh0, tot, 0.0), axis=1, keepdims=True)
    p1 = jnp.sum(jnp.where(oh1, tot, 0.0), axis=1, keepdims=True)
    pos = jnp.where(lane_f == 0.0, p0, jnp.where(lane_f == 1.0, p1, 0.0))
    pos_ref[...] = pos.astype(jnp.int32)
    base_s[0:1, :] = base_s[0:1, :] + csum


def _route(ri, cnt, *, n_blocks, n_exp, bm):
    t = ri.shape[0]
    tm = min(TOKEN_TILE, t)
    nbp = -(-n_blocks // SUBLANES) * SUBLANES
    kern = functools.partial(_route_kernel, tm=tm, bm=bm, n_exp=n_exp)
    return pl.pallas_call(
        kern,
        grid=(t // tm,),
        in_specs=[pl.BlockSpec((tm, LANES), lambda i: (i, 0)), _full(cnt.shape)],
        out_specs=[pl.BlockSpec((tm, LANES), lambda i: (i, 0)), _full((nbp, LANES))],
        out_shape=[jax.ShapeDtypeStruct((t, LANES), jnp.int32), jax.ShapeDtypeStruct((nbp, LANES), jnp.int32)],
        scratch_shapes=[pltpu.VMEM((SUBLANES, LANES), F32)],
        compiler_params=_cparams(("arbitrary",)),
        name="moe_route",
    )(ri, cnt)


def _scatter_kernel(pos_ref, nv_ref, x_ref, xs_out, zbuf, sem, zsem, *, tm, tiles, n_blocks, bm):
    step = pl.program_id(0)
    blk_rows = bm * tiles

    def zero_copy(i):
        dst = pl.multiple_of(i * blk_rows, blk_rows)
        return pltpu.make_async_copy(zbuf, xs_out.at[pl.ds(dst, blk_rows), :], zsem)

    @pl.when(step == 0)
    def _():
        zbuf[...] = jnp.zeros(zbuf.shape, F32)

        def zstart(i, carry):
            @pl.when(nv_ref[i] < bm)
            def _():
                zero_copy(i).start()
            return carry

        def zwait(i, carry):
            @pl.when(nv_ref[i] < bm)
            def _():
                zero_copy(i).wait()
            return carry

        lax.fori_loop(0, n_blocks, zstart, 0)
        lax.fori_loop(0, n_blocks, zwait, 0)

    def row_copy(t, k):
        src = pl.multiple_of(t * tiles, tiles)
        dst = pl.multiple_of(pos_ref[2 * t + k] * tiles, tiles)
        return pltpu.make_async_copy(x_ref.at[pl.ds(src, tiles), :], xs_out.at[pl.ds(dst, tiles), :], sem)

    def start(t, carry):
        row_copy(t, 0).start(priority=0)
        row_copy(t, 1).start(priority=1)
        return carry

    def wait(t, carry):
        row_copy(t, 0).wait()
        row_copy(t, 1).wait()
        return carry

    lax.fori_loop(0, tm, start, 0, unroll=DMA_UNROLL)
    lax.fori_loop(0, tm, wait, 0, unroll=DMA_UNROLL)


def _scatter(pos_flat, nvalid, x1t, n_blocks, tiles, bm):
    t = x1t.shape[0] // tiles
    tm = min(TOKEN_TILE, t)
    kern = functools.partial(_scatter_kernel, tm=tm, tiles=tiles, n_blocks=n_blocks, bm=bm)
    return pl.pallas_call(
        kern,
        grid=(t // tm,),
        in_specs=[
            pl.BlockSpec((2 * tm,), lambda i: (i,), memory_space=pltpu.SMEM),
            pl.BlockSpec(memory_space=pltpu.SMEM),
            pl.BlockSpec((tm * tiles, LANES), lambda i: (i, 0)),
        ],
        out_specs=pl.BlockSpec(memory_space=pl.ANY),
        out_shape=jax.ShapeDtypeStruct((n_blocks * bm * tiles, LANES), F32),
        scratch_shapes=[pltpu.VMEM((bm * tiles, LANES), F32),
                        pltpu.SemaphoreType.DMA(()), pltpu.SemaphoreType.DMA(())],
        compiler_params=_cparams(("arbitrary",)),
        name="moe_scatter",
    )(pos_flat, nvalid, x1t)


def _expert_kernel(be_ref, nv_ref, x_ref, wg_ref, wu_ref, wd_ref, y_ref, wg_s, wu_s, wd_s, *, tiles):
    i = pl.program_id(0)
    n_valid = nv_ref[i]

    @pl.when((i == 0) | (be_ref[i] != be_ref[jnp.maximum(i - 1, 0)]))
    def _():
        wg_s[...] = wg_ref[0].astype(BF16)
        wu_s[...] = wu_ref[0].astype(BF16)
        wd_s[...] = wd_ref[0].astype(BF16)

    @pl.when(n_valid == 0)
    def _():
        y_ref[...] = jnp.zeros(y_ref.shape, F32)

    @pl.when(n_valid > 0)
    def _():
        xb = _load_token_tiles(x_ref, x_ref.shape[0] // tiles, tiles).astype(BF16)
        g = jnp.dot(xb, wg_s[...], preferred_element_type=F32)
        u = jnp.dot(xb, wu_s[...], preferred_element_type=F32)
        h = g * jax.nn.sigmoid(g) * u
        _store_token_tiles(y_ref, jnp.dot(h.astype(BF16), wd_s[...], preferred_element_type=F32))


def _experts(block_expert, nvalid, xs, wg, wu, wd, bm):
    d, de = wg.shape[1], wg.shape[2]
    tiles = d // LANES
    blk_rows = bm * tiles
    nb = xs.shape[0] // blk_rows
    grid_spec = pltpu.PrefetchScalarGridSpec(
        num_scalar_prefetch=2,
        grid=(nb,),
        in_specs=[
            pl.BlockSpec((blk_rows, LANES), lambda i, be, nv: (i, 0)),
            pl.BlockSpec((1, d, de), lambda i, be, nv: (be[i], 0, 0)),
            pl.BlockSpec((1, d, de), lambda i, be, nv: (be[i], 0, 0)),
            pl.BlockSpec((1, de, d), lambda i, be, nv: (be[i], 0, 0)),
        ],
        out_specs=pl.BlockSpec((blk_rows, LANES), lambda i, be, nv: (i, 0)),
        scratch_shapes=[pltpu.VMEM((d, de), BF16), pltpu.VMEM((d, de), BF16), pltpu.VMEM((de, d), BF16)],
    )
    return pl.pallas_call(
        functools.partial(_expert_kernel, tiles=tiles),
        grid_spec=grid_spec,
        out_shape=jax.ShapeDtypeStruct(xs.shape, F32),
        compiler_params=_cparams(("arbitrary",)),
        name="moe_experts",
    )(block_expert, nvalid, xs, wg, wu, wd)


def _final_kernel(pos_ref, posn_ref, x1_ref, ri_ref, p_ref, ys_hbm, wpg_ref, wpp_ref, g2_ref, b2_ref,
                  pg_ref, out_ref, ybuf, sems, *, tm, tiles):
    step = pl.program_id(0)
    n_steps = pl.num_programs(0)
    slot = step % 2

    def row_copy(tab_ref, t, k, sl):
        src = pl.multiple_of(tab_ref[2 * t + k] * tiles, tiles)
        dst = pl.multiple_of(t * tiles, tiles)
        return pltpu.make_async_copy(ys_hbm.at[pl.ds(src, tiles), :], ybuf.at[sl, k, pl.ds(dst, tiles), :],
                                     sems.at[sl])

    def start_tile(tab_ref, sl):
        def body(t, carry):
            row_copy(tab_ref, t, 0, sl).start(priority=0)
            row_copy(tab_ref, t, 1, sl).start(priority=1)
            return carry

        lax.fori_loop(0, tm, body, 0, unroll=DMA_UNROLL)

    @pl.when(step == 0)
    def _():
        start_tile(pos_ref, 0)

    @pl.when(step + 1 < n_steps)
    def _():
        start_tile(posn_ref, 1 - slot)

    def wait(t, carry):
        row_copy(pos_ref, t, 0, slot).wait()
        row_copy(pos_ref, t, 1, slot).wait()
        return carry

    lax.fori_loop(0, tm, wait, 0, unroll=DMA_UNROLL)

    ri = ri_ref[...]
    y0 = _load_token_tiles(ybuf.at[slot, 0], tm, tiles)
    y1 = _load_token_tiles(ybuf.at[slot, 1], tm, tiles)
    ffn = ri[:, 2:3] * y0 + ri[:, 3:4] * y1
    x2 = _layer_norm(DN_ALPHA * x1_ref[...] + ffn, g2_ref[...], b2_ref[...])
    gate = jax.nn.sigmoid(jnp.dot(x2.astype(BF16), wpg_ref[...], preferred_element_type=F32))
    proj = jnp.dot(p_ref[...].astype(BF16), wpp_ref[...], preferred_element_type=F32)
    out_ref[...] = x2 + _rms_norm(gate * proj, pg_ref[...])


def _final(pos_flat, x1, ri, p, ys, wpg, wpp, g2, b2, pg):
    t, d = x1.shape
    tm = min(TOKEN_TILE, t)
    n_steps = t // tm
    tiles = d // LANES
    row = lambda w: pl.BlockSpec((tm, w), lambda i: (i, 0))
    consts = [wpg, wpp, g2, b2, pg]
    kern = functools.partial(_final_kernel, tm=tm, tiles=tiles)
    return pl.pallas_call(
        kern,
        grid=(n_steps,),
        in_specs=[pl.BlockSpec((2 * tm,), lambda i: (i,), memory_space=pltpu.SMEM),
                  pl.BlockSpec((2 * tm,), lambda i: (jnp.minimum(i + 1, n_steps - 1),), memory_space=pltpu.SMEM),
                  row(d), row(LANES), row(p.shape[1]), pl.BlockSpec(memory_space=pl.ANY)]
                 + [_full(a.shape) for a in consts],
        out_specs=row(d),
        out_shape=jax.ShapeDtypeStruct((t, d), F32),
        scratch_shapes=[pltpu.VMEM((2, 2, tm * tiles, LANES), F32), pltpu.SemaphoreType.DMA((2,))],
        compiler_params=_cparams(("arbitrary",)),
        name="combine_ln2_ple",
    )(pos_flat, pos_flat, x1, ri, p, ys, *consts)


def _lane_patterns(pair_layout):
    lane = np.arange(LANES)
    if pair_layout:
        valid = (lane < 2 * QK_ROPE).astype(np.float32)
    else:
        valid = np.ones(LANES, np.float32)
    sign = np.where((lane % QK_ROPE) < HALF_ROPE, -1.0, 1.0).astype(np.float32) * valid
    return lane % HALF_ROPE, sign, valid


def _prep_weights(w_in, w_uq, w_uk, w_uv, w_group, w_router):
    d = w_in.shape[0]
    ql = w_uq.shape[0]
    kvl, n_heads, _ = w_uk.shape
    o_kv, o_kr = ql, ql + kvl
    o_conv = o_kr + QK_ROPE
    c_conv = (w_in.shape[1] - o_conv - 2 * d) // 2
    o_ga = o_conv + 2 * c_conv
    o_gb = o_ga + d
    kr = w_in[:, o_kr:o_conv]
    zpad = jnp.zeros((d, LANES - 2 * QK_ROPE), w_in.dtype)
    win_r = jnp.concatenate([
        w_in[:, :o_kv], w_in[:, o_kv:o_kr], w_in[:, o_conv:o_ga], w_in[:, o_ga:o_gb], w_in[:, o_gb:],
        kr, kr, zpad], axis=1).astype(BF16)
    assert win_r.shape[1] == C_END and c_conv == C_CB - C_CA and kvl == C_CA - C_KV and ql == C_KV

    hd = QK_NOPE + QK_ROPE
    wq = w_uq.reshape(ql, n_heads, hd)
    nope = wq[:, :, :QK_NOPE].reshape(ql, n_heads * QK_NOPE)
    x1 = wq[:, :, QK_NOPE:QK_NOPE + HALF_ROPE]
    x2 = wq[:, :, QK_NOPE + HALF_ROPE:]
    rope_a = jnp.concatenate([x1, x2], axis=2)
    wuq_s = jnp.concatenate([nope, rope_a.reshape(ql, -1)], axis=1).astype(BF16)

    def pair_layout(r):
        r = r.reshape(ql, n_heads // 2, 2 * QK_ROPE)
        z = jnp.zeros((ql, n_heads // 2, LANES - 2 * QK_ROPE), r.dtype)
        return jnp.concatenate([r, z], axis=2).reshape(ql, -1)

    wuq_p = jnp.concatenate([nope, pair_layout(rope_a)], axis=1).astype(BF16)

    wuk_flat = w_uk.reshape(kvl, n_heads * QK_NOPE)
    wuv_flat = w_uv.reshape(kvl, n_heads * V_HEAD)
    wukv = jnp.concatenate([wuk_flat, wuv_flat], axis=1).astype(BF16)
    eye = jnp.eye(n_heads, dtype=w_uk.dtype)
    wk_bd = jnp.einsum('lhn,hg->hngl', w_uk, eye).reshape(n_heads * QK_NOPE, n_heads * kvl).astype(BF16)
    wv_bd = jnp.einsum('lhv,hg->hlgv', w_uv, eye).reshape(n_heads * kvl, n_heads * V_HEAD).astype(BF16)

    n_exp = w_router.shape[1]
    wgr = jnp.concatenate([w_group, w_router,
                           jnp.zeros((d, LANES - N_GROUPS - n_exp), w_group.dtype)], axis=1)
    wgh = wgr.astype(BF16)
    wgl = (wgr - wgh.astype(F32)).astype(BF16)
    return win_r, wuq_p, wuq_s, wukv, wk_bd, wv_bd, wgh, wgl


def _moe(x1t, ri, cnt, wg, wu, wd, n_exp):
    t = ri.shape[0]
    tiles = x1t.shape[0] // t
    n_assign = 2 * t
    bm = MOE_BLOCK if n_assign >= 2 * MOE_BLOCK * n_exp else MOE_BLOCK_MIN
    n_blocks = (n_assign + n_exp * (bm - 1) + bm - 1) // bm
    pos, be = _route(ri, cnt, n_blocks=n_blocks, n_exp=n_exp, bm=bm)
    pos_flat = pos[:, :2].reshape(-1)
    block_expert = be[:n_blocks, 0]
    nvalid = be[:n_blocks, 1]
    xs = _scatter(pos_flat, nvalid, x1t, n_blocks, tiles, bm)
    ys = _experts(block_expert, nvalid, xs, wg, wu, wd, bm)
    return pos_flat, ys


def kernel(x_prompt, x_sample, cache_ckv, cache_krope, state_conv, page_table, p_prompt, p_sample, w_in, q_norm_g, w_uq, kv_norm_g, w_uk, w_uv, w_o_attn, w_dw, b_dw, conv_ln_g, conv_ln_b, w_pw2, w_out, ln1_g, ln1_b, w_group, w_router, w_gate, w_up, w_down, ln2_g, ln2_b, w_ple_gate, w_ple_proj, ple_norm_g):
    assert w_in.shape[0] == DEPTH
    b, s, d = x_prompt.shape
    bd, sd, _ = x_sample.shape
    assert sd == 1
    n_pages = page_table.shape[1]
    page_rows = cache_ckv.shape[2]
    past = n_pages * page_rows
    n_heads = w_uk.shape[2]
    kvl = w_uk.shape[1]
    width = w_dw.shape[1]
    n_exp = w_router.shape[2]
    c_conv = w_dw.shape[2]

    win_r, wuq_p, wuq_s, wukv, wk_bd, wv_bd, wgh, wgl = _prep_weights(
        w_in[0], w_uq[0], w_uk[0], w_uv[0], w_group[0], w_router[0])
    qg, kvg = q_norm_g, kv_norm_g
    wo, wpw, wout = w_o_attn[0].astype(BF16), w_pw2[0].astype(BF16), w_out[0].astype(BF16)
    wgate, wup, wdown = w_gate[0], w_up[0], w_down[0]
    wpg, wpp = w_ple_gate[0].astype(BF16), w_ple_proj[0].astype(BF16)
    w_dw_pad = jnp.concatenate([w_dw[0], jnp.zeros((CONV_HALO - width, c_conv), F32)], axis=0)

    inv16 = 1.0 / (ROPE_THETA ** (jnp.arange(HALF_ROPE, dtype=F32) / HALF_ROPE))

    def tables(pair_layout, n_rows, offset, step):
        idx, sign, valid = _lane_patterns(pair_layout)
        inv_l = (inv16[idx] * valid)[None, :]
        return _rope_tables(n_rows, offset, step, inv_l, jnp.asarray(sign)[None, :], jnp.asarray(valid)[None, :])

    def trunk_tail(x2d, o, y_or_u, sa, sb, p2d, conv=None):
        x1, x1t, ri, cnt = _mix(x2d, o, y_or_u, sa, sb, wo, wpw, wout, conv_ln_g, conv_ln_b, ln1_g, ln1_b,
                           wgh, wgl, conv=conv)
        pos_flat, ys = _moe(x1t, ri, cnt, wgate, wup, wdown, n_exp)
        return _final(pos_flat, x1, ri, p2d, ys, wpg, wpp, ln2_g, ln2_b, ple_norm_g)

    xs_ = x_sample.reshape(bd, d)
    cos_s, sin_s = tables(False, SUBLANES, past, 0)
    qn_s, qr_s, ckv_s, kr_s, u_s, sa_s, sb_s = _inproj(
        xs_, win_r, qg, kvg, wuq_s, None, cos_s, sin_s, n_rep=n_heads * QK_ROPE // LANES, seq=1, make_kv=False)
    q_lat = _dense(qn_s, wk_bd, F32).reshape(bd, n_heads, kvl)
    o_lat = _sample_attention(page_table, q_lat, qr_s.reshape(bd, n_heads, QK_ROPE),
                              ckv_s.reshape(bd, 1, kvl), kr_s.reshape(bd, 1, QK_ROPE),
                              cache_ckv[0], jnp.swapaxes(cache_krope[0], 1, 2))
    o_s = _dense(o_lat.reshape(bd, n_heads * kvl), wv_bd, BF16)
    ext_s = jnp.concatenate([state_conv[0], u_s[:, None, :]], axis=1)
    y_s = _conv_sample(jnp.transpose(ext_s, (1, 0, 2)), w_dw_pad, b_dw, width=width)
    out_s = trunk_tail(xs_, o_s, y_s, sa_s, sb_s, p_sample[0].reshape(bd, -1))
    new_conv_s = ext_s[:, 1:, :]

    xp = x_prompt.reshape(b * s, d)
    cos_p, sin_p = tables(True, s, 0, 1)
    qn, qr, kn, v, krd, ckv_p, kr_p, u_p, sa_p, sb_p = _inproj(
        xp, win_r, qg, kvg, wuq_p, wukv, cos_p, sin_p, n_rep=n_heads // 2, seq=s, make_kv=True)
    o_p = _prompt_attention(qn, qr, kn, krd, v, batch=b, seq=s)
    out_p = trunk_tail(xp, o_p, u_p, sa_p, sb_p, p_prompt[0].reshape(b * s, -1),
                       conv=(w_dw_pad, b_dw, s, width))
    u_p3 = u_p.reshape(b, s, c_conv)
    new_conv_p = u_p3[:, s - (width - 1):, :]

    return (out_p.reshape(b, s, d), out_s.reshape(bd, 1, d),
            ckv_p.reshape(1, b, s, kvl), kr_p.reshape(1, b, s, QK_ROPE), new_conv_p[None],
            ckv_s.reshape(1, bd, 1, kvl), kr_s.reshape(1, bd, 1, QK_ROPE), new_conv_s[None])
```

```python
import functools

import numpy as np
import jax
import jax.numpy as jnp
from jax import lax
from jax.experimental import pallas as pl
from jax.experimental.pallas import tpu as pltpu

F32 = jnp.float32
BF16 = jnp.bfloat16

QK_NOPE = 64
QK_ROPE = 32
HALF_ROPE = QK_ROPE // 2
V_HEAD = 64
ROPE_THETA = 10000.0
ATTN_SCALE = (QK_NOPE + QK_ROPE) ** -0.5
LOG2E = 1.4426950408889634
N_GROUPS = 4
EXPERTS_PER_GROUP = 8
DEPTH = 1
DN_ALPHA = (2 * DEPTH) ** 0.25
LN_EPS = 1e-5
RMS_EPS = 1e-6
NEG_INF = -1e30

LANES = 128
SUBLANES = 8
VMEM_LIMIT = 56 * 1024 * 1024

TOKEN_TILE = 512
ATTN_TILE = 512
ATTN_PHASE_GROUP = 4
CONV_CHUNK = 64
CONV_HALO = 32
MOE_BLOCK = 512
MOE_BLOCK_MIN = 128
PAGES_PER_CHUNK = 16
SEQS_PER_STEP = 4
DMA_UNROLL = 8
FINAL_ISSUE_GROUPS = 8


def _cparams(sem, vmem=VMEM_LIMIT):
    return pltpu.CompilerParams(dimension_semantics=sem, vmem_limit_bytes=vmem)


def _full(shape):
    n = len(shape)
    return pl.BlockSpec(shape, lambda *_: (0,) * n)


def _layer_norm(x, g, b):
    mu = jnp.mean(x, axis=-1, keepdims=True)
    xc = x - mu
    var = jnp.mean(xc * xc, axis=-1, keepdims=True)
    return xc * lax.rsqrt(var + LN_EPS) * g + b


def _rms_norm(x, g):
    return x * lax.rsqrt(jnp.mean(x * x, axis=-1, keepdims=True) + RMS_EPS) * g


def _rope_table_kernel(inv_ref, sign_ref, valid_ref, cos_ref, sin_ref, *, offset, step):
    rows = cos_ref.shape[0]
    r = lax.broadcasted_iota(jnp.int32, (rows, LANES), 0) + pl.program_id(0) * rows
    pos = (offset + step * r).astype(F32)
    ang = pos * inv_ref[...]
    cos_ref[...] = jnp.cos(ang) * valid_ref[...]
    sin_ref[...] = jnp.sin(ang) * sign_ref[...]


def _rope_tables(n_rows, offset, step, inv_l, sign_l, valid_l):
    tr = min(n_rows, 512)
    kern = functools.partial(_rope_table_kernel, offset=offset, step=step)
    return pl.pallas_call(
        kern,
        grid=(n_rows // tr,),
        in_specs=[_full((1, LANES))] * 3,
        out_specs=[pl.BlockSpec((tr, LANES), lambda i: (i, 0))] * 2,
        out_shape=[jax.ShapeDtypeStruct((n_rows, LANES), F32)] * 2,
        compiler_params=_cparams(("arbitrary",)),
        name="rope_tables",
    )(inv_l, sign_l, valid_l)


C_Q, C_KV, C_CA, C_CB, C_GA, C_GB, C_KA, C_END = 0, 384, 640, 1152, 1664, 2688, 3712, 3840


def _rotary_partner(a):
    lane = lax.broadcasted_iota(jnp.int32, (a.shape[0], LANES), 1)
    first_half = (lane % QK_ROPE) < HALF_ROPE
    blocks = []
    for c in range(a.shape[1] // LANES):
        blk = a[:, c * LANES:(c + 1) * LANES]
        blocks.append(jnp.where(first_half, pltpu.roll(blk, LANES - HALF_ROPE, 1), pltpu.roll(blk, HALF_ROPE, 1)))
    return blocks[0] if len(blocks) == 1 else jnp.concatenate(blocks, axis=1)


def _inproj_kernel(x_ref, win_ref, qg_ref, kvg_ref, wuq_ref, *rest, n_rep, make_kv, bcast, q_scale):
    if make_kv:
        (wukv_ref, cos_ref, sin_ref, qn_o, qr_o, kn_o, v_o, krd_o, ckv_o, kr_o, u_o, sa_o, sb_o) = rest
    else:
        (cos_ref, sin_ref, qn_o, qr_o, ckv_o, kr_o, u_o, sa_o, sb_o) = rest
    xb = x_ref[...].astype(BF16)

    def proj(a, b):
        return jnp.dot(xb, win_ref[:, a:b], preferred_element_type=F32)

    if bcast:
        c1, s1 = cos_ref[0:1, :], sin_ref[0:1, :]
    else:
        c1, s1 = cos_ref[...], sin_ref[...]
    cn = jnp.concatenate([c1] * n_rep, axis=1)
    sn = jnp.concatenate([s1] * n_rep, axis=1)
    d_nope = qn_o.shape[1]
    nr = LANES * n_rep

    cqn = _rms_norm(proj(C_Q, C_KV), qg_ref[...])
    qall = jnp.dot(cqn.astype(BF16), wuq_ref[...], preferred_element_type=F32)
    qn_o[...] = (qall[:, :d_nope] * q_scale).astype(qn_o.dtype)
    qa = qall[:, d_nope:d_nope + nr]
    qb = _rotary_partner(qa)
    qr_o[...] = ((qa * cn + qb * sn) * q_scale).astype(qr_o.dtype)

    ckvn = _rms_norm(proj(C_KV, C_CA), kvg_ref[...])
    ckv_o[...] = ckvn
    kra = proj(C_KA, C_END)
    krot = kra * c1 + _rotary_partner(kra) * s1
    kr_o[...] = krot[:, :QK_ROPE]
    if make_kv:
        kv = jnp.dot(ckvn.astype(BF16), wukv_ref[...], preferred_element_type=F32)
        half = kv.shape[1] // 2
        kn_o[...] = kv[:, :half].astype(BF16)
        v_o[...] = kv[:, half:].astype(BF16)
        krd_o[...] = krot.astype(BF16)

    u_o[...] = proj(C_CA, C_CB) * jax.nn.sigmoid(proj(C_CB, C_GA))
    sa_o[...] = jax.nn.sigmoid(proj(C_GA, C_GB)).astype(BF16)
    sb_o[...] = jax.nn.sigmoid(proj(C_GB, C_KA)).astype(BF16)


def _inproj(x, win_r, qg, kvg, wuq_r, wukv, cos_t, sin_t, *, n_rep, seq, make_kv):
    t, d = x.shape
    tm = min(TOKEN_TILE, t)
    d_nope = wuq_r.shape[1] - LANES * n_rep
    nr = LANES * n_rep
    c_conv = C_CB - C_CA
    kvl = C_CA - C_KV
    bcast = seq < tm
    if bcast:
        tab_spec = _full(cos_t.shape)
    else:
        nst = seq // tm
        tab_spec = pl.BlockSpec((tm, LANES), lambda i: (i % nst, 0))
    row = lambda w: pl.BlockSpec((tm, w), lambda i: (i, 0))
    in_specs = [row(d), _full(win_r.shape), _full(qg.shape), _full(kvg.shape), _full(wuq_r.shape)]
    args = [x, win_r, qg, kvg, wuq_r]
    if make_kv:
        in_specs.append(_full(wukv.shape))
        args.append(wukv)
    in_specs += [tab_spec, tab_spec]
    args += [cos_t, sin_t]
    q_dt = BF16 if make_kv else F32
    outs = [(d_nope, BF16), (nr, q_dt)]
    if make_kv:
        outs += [(d_nope, BF16), (d_nope, BF16), (LANES, BF16)]
    outs += [(kvl, F32), (QK_ROPE, F32), (c_conv, F32), (d, BF16), (d, BF16)]
    q_scale = ATTN_SCALE * LOG2E if make_kv else ATTN_SCALE
    kern = functools.partial(_inproj_kernel, n_rep=n_rep, make_kv=make_kv, bcast=bcast, q_scale=q_scale)
    return pl.pallas_call(
        kern,
        grid=(t // tm,),
        in_specs=in_specs,
        out_specs=[row(w) for w, _ in outs],
        out_shape=[jax.ShapeDtypeStruct((t, w), dt) for w, dt in outs],
        compiler_params=_cparams(("arbitrary",)),
        name="inproj_kv" if make_kv else "inproj_q",
    )(*args)


def _softmax_stats(state, s):
    m, l, _ = state
    m_new = jnp.maximum(m, jnp.max(s, axis=1, keepdims=True))
    alpha = jnp.exp2(m - m_new)
    p = jnp.exp2(s - jnp.concatenate([m_new] * (s.shape[1] // LANES), axis=1))
    return m_new, alpha * l + jnp.sum(p, axis=1, keepdims=True), alpha, p.astype(BF16)


def _pattn_kernel(qn_ref, qr_ref, kn_ref, krd_ref, v_ref, o_ref, m_s, l_s, acc_s, *, blk, nq):
    half = blk // 2
    lane = lax.broadcasted_iota(jnp.int32, (blk, LANES), 1)
    row_a = lax.broadcasted_iota(jnp.int32, (blk, half), 0)
    col_a = lax.broadcasted_iota(jnp.int32, (blk, half), 1)
    row_b = lax.broadcasted_iota(jnp.int32, (half, half), 0)
    col_b = lax.broadcasted_iota(jnp.int32, (half, half), 1)
    dn_t = (((1,), (1,)), ((), ()))

    def q_heads(i):
        qn = qn_ref[i * blk:(i + 1) * blk, :]
        qr = qr_ref[i * blk:(i + 1) * blk, :]
        zero = jnp.zeros_like(qn)
        return (
            jnp.concatenate([jnp.where(lane < QK_NOPE, qn, zero), jnp.where(lane < QK_ROPE, qr, zero)], axis=1),
            jnp.concatenate([jnp.where(lane >= QK_NOPE, qn, zero),
                             jnp.where((lane >= QK_ROPE) & (lane < 2 * QK_ROPE), qr, zero)], axis=1),
        )

    qh = [q_heads(i) for i in range(nq)]
    pairs = [(i, h) for i in range(nq) for h in range(2)]
    state = {ih: (jnp.full((blk, LANES), NEG_INF, F32), jnp.zeros((blk, LANES), F32),
                  jnp.zeros((blk, LANES), F32)) for ih in pairs}
    for j in range(nq):
        k0 = j * blk
        k_a = jnp.concatenate([kn_ref[k0:k0 + half, :], krd_ref[k0:k0 + half, :]], axis=1)
        k_b = jnp.concatenate([kn_ref[k0 + half:k0 + blk, :], krd_ref[k0 + half:k0 + blk, :]], axis=1)
        k_full = jnp.concatenate([k_a, k_b], axis=0)
        v_a = v_ref[k0:k0 + half, :]
        v_b = v_ref[k0 + half:k0 + blk, :]
        v_full = v_ref[k0:k0 + blk, :]
        work = [((j, h), k_a, v_a, True) for h in range(2)]
        work += [((i, h), k_full, v_full, False) for i in range(j + 1, nq) for h in range(2)]
        for g0 in range(0, len(work), ATTN_PHASE_GROUP):
            group = work[g0:g0 + ATTN_PHASE_GROUP]
            scores = []
            for (i, h), k, _, diag in group:
                s = lax.dot_general(qh[i][h], k, dn_t, preferred_element_type=F32)
                scores.append(jnp.where(col_a <= row_a, s, NEG_INF) if diag else s)
            stats = [_softmax_stats(state[ih], s) for (ih, _, _, _), s in zip(group, scores)]
            for (ih, _, v, _), (m_new, l, alpha, p) in zip(group, stats):
                state[ih] = (m_new, l, alpha * state[ih][2] + jnp.dot(p, v, preferred_element_type=F32))
        for h in range(2):
            m_s[j, h], l_s[j, h], acc_s[j, h] = state[(j, h)]
        low = [(m_s[j, h, half:, :], l_s[j, h, half:, :], acc_s[j, h, half:, :]) for h in range(2)]
        scores = [jnp.where(col_b <= row_b,
                            lax.dot_general(qh[j][h][half:], k_b, dn_t, preferred_element_type=F32), NEG_INF)
                  for h in range(2)]
        stats = [_softmax_stats(low[h], scores[h]) for h in range(2)]
        for h in range(2):
            _, l, alpha, p = stats[h]
            l_s[j, h, half:, :] = l
            acc_s[j, h, half:, :] = alpha * low[h][2] + jnp.dot(p, v_b, preferred_element_type=F32)
        o = jnp.where(lane < V_HEAD, acc_s[j, 0] / l_s[j, 0], acc_s[j, 1] / l_s[j, 1])
        o_ref[k0:k0 + blk, :] = o.astype(o_ref.dtype)


def _prompt_attention(qn, qr, kn, krd, v, *, batch, seq):
    t, d = qn.shape
    blk = min(ATTN_TILE, seq)
    nq = seq // blk
    n_pairs = d // LANES
    kspec = pl.BlockSpec((seq, LANES), lambda b, j: (b, j))
    kern = functools.partial(_pattn_kernel, blk=blk, nq=nq)
    return pl.pallas_call(
        kern,
        grid=(batch, n_pairs),
        in_specs=[kspec, kspec, kspec, pl.BlockSpec((seq, LANES), lambda b, j: (b, 0)), kspec],
        out_specs=kspec,
        out_shape=jax.ShapeDtypeStruct((t, d), BF16),
        scratch_shapes=[pltpu.VMEM((nq, 2, blk, LANES), F32)] * 3,
        compiler_params=_cparams(("arbitrary",) * 2),
        name="prompt_attention",
    )(qn, qr, kn, krd, v)


def _dense_kernel(a_ref, w_ref, o_ref):
    o_ref[...] = jnp.dot(a_ref[...].astype(BF16), w_ref[...], preferred_element_type=F32).astype(o_ref.dtype)


def _dense(a, w, out_dtype):
    m, n = a.shape[0], w.shape[1]
    return pl.pallas_call(
        _dense_kernel,
        grid=(1,),
        in_specs=[_full(a.shape), _full(w.shape)],
        out_specs=_full((m, n)),
        out_shape=jax.ShapeDtypeStruct((m, n), out_dtype),
        compiler_params=_cparams(("arbitrary",)),
        name="dense",
    )(a, w)


def _sattn_kernel(pt_ref, ptn_ref, ql_ref, qr_ref, cnew_ref, knew_ref, ckv_hbm, krt_hbm, o_ref,
                  cbuf, kbuf, ckb, sems, *, n_chunks, ppc, nb):
    step = pl.program_id(0)
    n_steps = pl.num_programs(0)
    kvl = cbuf.shape[4]
    kc = ppc * cbuf.shape[3]
    n_heads = ql_ref.shape[1]
    dn_t = (((1,), (1,)), ((), ()))

    def chunk_copies(tab_ref, chunk, slot):
        out = []
        for b in range(nb):
            for pg in range(ppc):
                page = tab_ref[b, 0, chunk * ppc + pg]
                out.append(pltpu.make_async_copy(ckv_hbm.at[page], cbuf.at[slot, b, pg], sems.at[0, slot]))
                out.append(pltpu.make_async_copy(krt_hbm.at[page], kbuf.at[slot, b, pg], sems.at[1, slot]))
        return out

    def start_chunk(tab_ref, chunk, slot):
        for cp in chunk_copies(tab_ref, chunk, slot):
            cp.start()

    @pl.when(step == 0)
    def _():
        start_chunk(pt_ref, 0, 0)

    first = step * n_chunks
    qls = [ql_ref[b].astype(BF16) for b in range(nb)]
    qrs = [qr_ref[b].astype(BF16) for b in range(nb)]

    def body(c, carry):
        slot = (first + c) % 2

        @pl.when(c + 1 < n_chunks)
        def _():
            start_chunk(pt_ref, c + 1, 1 - slot)

        @pl.when((c + 1 == n_chunks) & (step + 1 < n_steps))
        def _():
            start_chunk(ptn_ref, 0, 1 - slot)

        for cp in chunk_copies(pt_ref, c, slot):
            cp.wait()
        for b in range(nb):
            ckb[b] = cbuf[slot, b].reshape(kc, kvl).astype(BF16)
        scores = []
        for b in range(nb):
            krt = jnp.concatenate([kbuf[slot, b, pg] for pg in range(ppc)], axis=1).astype(BF16)
            scores.append(lax.dot_general(qls[b], ckb[b], dn_t, preferred_element_type=F32)
                          + jnp.dot(qrs[b], krt, preferred_element_type=F32))
        stats = []
        for b in range(nb):
            m, l, _ = carry[3 * b:3 * b + 3]
            m_new = jnp.maximum(m, jnp.max(scores[b], axis=1, keepdims=True))
            alpha = jnp.exp(m - m_new)
            p = jnp.exp(scores[b] - m_new)
            stats.append((m_new, alpha * l + jnp.sum(p, axis=1, keepdims=True), alpha, p.astype(BF16)))
        new = []
        for b in range(nb):
            m_new, l, alpha, p = stats[b]
            acc = alpha * carry[3 * b + 2] + jnp.dot(p, ckb[b], preferred_element_type=F32)
            new += [m_new, l, acc]
        return tuple(new)

    init = (jnp.full((n_heads, 1), NEG_INF, F32), jnp.zeros((n_heads, 1), F32),
            jnp.zeros((n_heads, kvl), F32)) * nb
    carry = lax.fori_loop(0, n_chunks, body, init)

    for b in range(nb):
        m, l, acc = carry[3 * b:3 * b + 3]
        cnew = cnew_ref[b]
        knew = knew_ref[b]
        s_new = (jnp.sum(ql_ref[b] * cnew, axis=1, keepdims=True)
                 + jnp.sum(qr_ref[b] * knew, axis=1, keepdims=True))
        m_new = jnp.maximum(m, s_new)
        alpha = jnp.exp(m - m_new)
        p_new = jnp.exp(s_new - m_new)
        l = alpha * l + p_new
        acc = alpha * acc + p_new * cnew
        o_ref[b] = acc / l


def _sample_attention(page_table, q_lat, q_rope, c_new, k_new, cache_ckv, cache_krope_t):
    bd, n_heads, kvl = q_lat.shape
    n_pages = page_table.shape[1]
    page_rows = cache_ckv.shape[1]
    ppc = min(PAGES_PER_CHUNK, n_pages)
    n_chunks = n_pages // ppc
    nb = min(SEQS_PER_STEP, bd)
    n_steps = bd // nb
    pt3 = page_table.reshape(bd, 1, n_pages)
    per_b = lambda s: pl.BlockSpec((nb,) + s, lambda i: (i, 0, 0))
    kern = functools.partial(_sattn_kernel, n_chunks=n_chunks, ppc=ppc, nb=nb)
    return pl.pallas_call(
        kern,
        grid=(n_steps,),
        in_specs=[
            pl.BlockSpec((nb, 1, n_pages), lambda i: (i, 0, 0), memory_space=pltpu.SMEM),
            pl.BlockSpec((nb, 1, n_pages), lambda i: (jnp.minimum(i + 1, n_steps - 1), 0, 0),
                         memory_space=pltpu.SMEM),
            per_b((n_heads, kvl)), per_b((n_heads, QK_ROPE)), per_b((1, kvl)), per_b((1, QK_ROPE)),
            pl.BlockSpec(memory_space=pl.ANY), pl.BlockSpec(memory_space=pl.ANY),
        ],
        out_specs=per_b((n_heads, kvl)),
        out_shape=jax.ShapeDtypeStruct((bd, n_heads, kvl), F32),
        scratch_shapes=[
            pltpu.VMEM((2, nb, ppc, page_rows, kvl), F32),
            pltpu.VMEM((2, nb, ppc, QK_ROPE, page_rows), F32),
            pltpu.VMEM((nb, ppc * page_rows, kvl), BF16),
            pltpu.SemaphoreType.DMA((2, 2)),
        ],
        compiler_params=_cparams(("arbitrary",)),
        name="sample_attention",
    )(pt3, pt3, q_lat, q_rope, c_new, k_new, cache_ckv, cache_krope_t)


def _conv_rows(u_ref, halo_ref, w_ref, b_ref, y_ref, ext_s, z_s, first_tile, width):
    ts, ch = u_ref.shape
    left = jnp.where(first_tile, jnp.zeros_like(halo_ref), halo_ref[...])
    ext_s[0:CONV_HALO, :] = left
    ext_s[CONV_HALO:CONV_HALO + ts, :] = u_ref[...]
    ext_s[CONV_HALO + ts:CONV_HALO + ts + SUBLANES, :] = jnp.zeros((SUBLANES, ch), F32)
    off = CONV_HALO - (width - 1)
    win = CONV_CHUNK + SUBLANES
    for c in range(ts // CONV_CHUNK):
        r0 = c * CONV_CHUNK
        acc = jnp.broadcast_to(b_ref[...], (CONV_CHUNK, ch))
        for r in range(SUBLANES):
            z = None
            for a in range((off + width + SUBLANES - 1) // SUBLANES):
                k = SUBLANES * a + r - off
                if 0 <= k < width:
                    term = w_ref[k:k + 1, :] * ext_s[r0 + SUBLANES * a:r0 + SUBLANES * a + win, :]
                    z = term if z is None else z + term
            if r == 0:
                acc = acc + z[:CONV_CHUNK]
            else:
                z_s[r] = z
                acc = acc + z_s[r, r:r + CONV_CHUNK, :]
        y_ref[r0:r0 + CONV_CHUNK, :] = acc


def _conv_sample_kernel(ext_ref, w_ref, b_ref, y_ref, *, width):
    acc = jnp.broadcast_to(b_ref[...], y_ref.shape)
    for k in range(width):
        acc = acc + w_ref[k:k + 1, :] * ext_ref[k]
    y_ref[...] = acc


def _conv_sample(ext_t, w_pad, b, *, width):
    _, bd, c = ext_t.shape
    kern = functools.partial(_conv_sample_kernel, width=width)
    return pl.pallas_call(
        kern,
        grid=(1,),
        in_specs=[_full(ext_t.shape), _full(w_pad.shape), _full(b.shape)],
        out_specs=_full((bd, c)),
        out_shape=jax.ShapeDtypeStruct((bd, c), F32),
        compiler_params=_cparams(("arbitrary",)),
        name="conv_sample",
    )(ext_t, w_pad, b)


def _store_token_tiles(ref, x):
    rows, d = x.shape
    tiles = d // LANES
    for c in range(tiles):
        ref[pl.ds(c, rows, stride=tiles), :] = x[:, c * LANES:(c + 1) * LANES]


def _load_token_tiles(ref, rows, tiles):
    return jnp.concatenate([ref[pl.ds(c, rows, stride=tiles), :] for c in range(tiles)], axis=1)


def _mix_kernel(x_ref, o_ref, *rest, conv_tiles, width):
    if conv_tiles:
        (u_ref, halo_ref, wdw_ref, bdw_ref, sa_ref, sb_ref, wo_ref, wpw_ref, wout_ref, cg_ref, cb_ref,
         g1_ref, b1_ref, wgh_ref, wgl_ref, x1_ref, x1t_ref, ri_ref, cnt_ref, ext_s, z_s, y_ref) = rest
        first_tile = pl.program_id(0) % conv_tiles == 0
        _conv_rows(u_ref, halo_ref, wdw_ref, bdw_ref, y_ref, ext_s, z_s, first_tile, width)
    else:
        (y_ref, sa_ref, sb_ref, wo_ref, wpw_ref, wout_ref, cg_ref, cb_ref,
         g1_ref, b1_ref, wgh_ref, wgl_ref, x1_ref, x1t_ref, ri_ref, cnt_ref) = rest
    branch_b = jnp.dot(o_ref[...], wo_ref[...], preferred_element_type=F32)
    z = _layer_norm(y_ref[...], cg_ref[...], cb_ref[...])
    z = z * jax.nn.sigmoid(z)
    branch_a = jnp.dot(z.astype(BF16), wpw_ref[...], preferred_element_type=F32)
    mixin = sa_ref[...].astype(F32) * branch_a + sb_ref[...].astype(F32) * branch_b
    mix = jnp.dot(mixin.astype(BF16), wout_ref[...], preferred_element_type=F32)
    x1 = _layer_norm(DN_ALPHA * x_ref[...] + mix, g1_ref[...], b1_ref[...])
    x1_ref[...] = x1
    _store_token_tiles(x1t_ref, x1)

    x_hi = x1.astype(BF16)
    x_lo = (x1 - x_hi.astype(F32)).astype(BF16)
    lg = (jnp.dot(x_hi, wgh_ref[...], preferred_element_type=F32)
          + jnp.dot(x_lo, wgh_ref[...], preferred_element_type=F32)
          + jnp.dot(x_hi, wgl_ref[...], preferred_element_type=F32))
    n_exp = N_GROUPS * EXPERTS_PER_GROUP
    lane = lax.broadcasted_iota(jnp.int32, lg.shape, 1)
    lane_f = lane.astype(F32)
    big = float(LANES)
    gmask = lane < N_GROUPS
    lgm = jnp.where(gmask, lg, NEG_INF)
    gmax = jnp.max(lgm, axis=1, keepdims=True)
    gidx = jnp.min(jnp.where(lgm == gmax, lane_f, big), axis=1, keepdims=True)
    pg_sel = 1.0 / jnp.sum(jnp.where(gmask, jnp.exp(lgm - gmax), 0.0), axis=1, keepdims=True)
    egroup = jnp.floor((lane_f - N_GROUPS) * (1.0 / EXPERTS_PER_GROUP))
    emask = (lane >= N_GROUPS) & (lane < N_GROUPS + n_exp) & (egroup == gidx)
    le = jnp.where(emask, lg, NEG_INF)
    v1 = jnp.max(le, axis=1, keepdims=True)
    i1 = jnp.min(jnp.where(le == v1, lane_f, big), axis=1, keepdims=True)
    le2 = jnp.where(lane_f == i1, NEG_INF, le)
    v2 = jnp.max(le2, axis=1, keepdims=True)
    i2 = jnp.min(jnp.where(le2 == v2, lane_f, big), axis=1, keepdims=True)
    e = jnp.exp(v2 - v1)
    gate1 = pg_sel / (1.0 + e)
    gate2 = pg_sel * e / (1.0 + e)
    ri = jnp.where(lane == 0, i1 - N_GROUPS,
                   jnp.where(lane == 1, i2 - N_GROUPS,
                             jnp.where(lane == 2, gate1, jnp.where(lane == 3, gate2, 0.0))))
    ri_ref[...] = ri

    chosen = (lane_f == i1 - N_GROUPS) | (lane_f == i2 - N_GROUPS)

    @pl.when(pl.program_id(0) == 0)
    def _():
        cnt_ref[...] = jnp.zeros(cnt_ref.shape, F32)

    cnt_ref[0:1, :] = cnt_ref[0:1, :] + jnp.sum(jnp.where(chosen, 1.0, 0.0), axis=0, keepdims=True)


def _mix(x, o, y_or_u, sa, sb, wo, wpw, wout, cg, cb, g1, b1, wgh, wgl, conv=None):
    t, d = x.shape
    tm = min(TOKEN_TILE, t)
    c = y_or_u.shape[1]
    row = lambda w: pl.BlockSpec((tm, w), lambda i: (i, 0))
    consts = [wo, wpw, wout, cg, cb, g1, b1, wgh, wgl]
    tiles = d // LANES
    if conv is None:
        kern = functools.partial(_mix_kernel, conv_tiles=0, width=0)
        lead_specs, lead, scratch = [row(c)], [y_or_u], []
    else:
        w_pad, b_dw, seq, width = conv
        assert seq % tm == 0
        hb = tm // CONV_HALO
        kern = functools.partial(_mix_kernel, conv_tiles=seq // tm, width=width)
        lead_specs = [row(c), pl.BlockSpec((CONV_HALO, c), lambda i: (jnp.maximum(i * hb - 1, 0), 0)),
                      _full(w_pad.shape), _full(b_dw.shape)]
        lead = [y_or_u, y_or_u, w_pad, b_dw]
        scratch = [pltpu.VMEM((CONV_HALO + tm + SUBLANES, c), F32),
                   pltpu.VMEM((SUBLANES, CONV_CHUNK + SUBLANES, c), F32), pltpu.VMEM((tm, c), F32)]
    return pl.pallas_call(
        kern,
        grid=(t // tm,),
        in_specs=[row(d), row(d)] + lead_specs + [row(d), row(d)] + [_full(a.shape) for a in consts],
        out_specs=[row(d), pl.BlockSpec((tm * tiles, LANES), lambda i: (i, 0)), row(LANES),
                   _full((SUBLANES, LANES))],
        out_shape=[jax.ShapeDtypeStruct((t, d), F32), jax.ShapeDtypeStruct((t * tiles, LANES), F32),
                   jax.ShapeDtypeStruct((t, LANES), F32), jax.ShapeDtypeStruct((SUBLANES, LANES), F32)],
        scratch_shapes=scratch,
        compiler_params=_cparams(("arbitrary",)),
        name="mix_ln1_router",
    )(x, o, *lead, sa, sb, *consts)


def _route_kernel(ri_ref, cnt_ref, pos_ref, be_ref, base_s, *, tm, bm, n_exp):
    i = pl.program_id(0)
    lane_f = lax.broadcasted_iota(jnp.int32, (tm, LANES), 1).astype(F32)
    ri = ri_ref[...]
    oh0 = lane_f == ri[:, 0:1]
    oh1 = lane_f == ri[:, 1:2]
    c = jnp.where(oh0 | oh1, 1.0, 0.0)
    csum = jnp.sum(c, axis=0, keepdims=True)

    @pl.when(i == 0)
    def _():
        cnt = cnt_ref[...]
        pc = jnp.floor((cnt + (bm - 1)) * (1.0 / bm)) * bm
        r = lax.broadcasted_iota(jnp.int32, (LANES, LANES), 0)
        cc = lax.broadcasted_iota(jnp.int32, (LANES, LANES), 1)
        upper = jnp.where(r <= cc, 1.0, 0.0)
        pend = jnp.dot(pc, upper, precision=lax.Precision.HIGHEST, preferred_element_type=F32)
        base_s[...] = pend - pc
        nbp = be_ref.shape[0]
        bstart = (lax.broadcasted_iota(jnp.int32, (nbp, LANES), 0) * bm).astype(F32)
        lane_b = lax.broadcasted_iota(jnp.int32, (nbp, LANES), 1)
        hit = jnp.where((lane_b < n_exp) & (pend[0:1, :] <= bstart), 1.0, 0.0)
        be = jnp.minimum(jnp.sum(hit, axis=1, keepdims=True), float(n_exp - 1))
        pstart = pend[0:1, :] - pc[0:1, :]
        span = jnp.minimum(pstart + cnt[0:1, :], bstart + bm) - jnp.maximum(pstart, bstart)
        nvalid = jnp.sum(jnp.where(lane_b < n_exp, jnp.maximum(span, 0.0), 0.0), axis=1, keepdims=True)
        be_ref[...] = jnp.where(lane_b == 0, be, jnp.where(lane_b == 1, nvalid, 0.0)).astype(jnp.int32)

    r = lax.broadcasted_iota(jnp.int32, (tm, tm), 0)
    cc = lax.broadcasted_iota(jnp.int32, (tm, tm), 1)
    lower = jnp.where(cc < r, 1.0, 0.0).astype(BF16)
    cum = jnp.dot(lower, c.astype(BF16), preferred_element_type=F32)
    tot = cum + base_s[0:1, :]
    p0 = jnp.sum(jnp.where(oh0, tot, 0.0), axis=1, keepdims=True)
    p1 = jnp.sum(jnp.where(oh1, tot, 0.0), axis=1, keepdims=True)
    pos = jnp.where(lane_f == 0.0, p0, jnp.where(lane_f == 1.0, p1, 0.0))
    pos_ref[...] = pos.astype(jnp.int32)
    base_s[0:1, :] = base_s[0:1, :] + csum


def _route(ri, cnt, *, n_blocks, n_exp, bm):
    t = ri.shape[0]
    tm = min(TOKEN_TILE, t)
    nbp = -(-n_blocks // SUBLANES) * SUBLANES
    kern = functools.partial(_route_kernel, tm=tm, bm=bm, n_exp=n_exp)
    return pl.pallas_call(
        kern,
        grid=(t // tm,),
        in_specs=[pl.BlockSpec((tm, LANES), lambda i: (i, 0)), _full(cnt.shape)],
        out_specs=[pl.BlockSpec((tm, LANES), lambda i: (i, 0)), _full((nbp, LANES))],
        out_shape=[jax.ShapeDtypeStruct((t, LANES), jnp.int32), jax.ShapeDtypeStruct((nbp, LANES), jnp.int32)],
        scratch_shapes=[pltpu.VMEM((SUBLANES, LANES), F32)],
        compiler_params=_cparams(("arbitrary",)),
        name="moe_route",
    )(ri, cnt)


def _scatter_kernel(pos_ref, nv_ref, x_ref, xs_out, zbuf, sem, zsem, *, tm, tiles, n_blocks, bm):
    step = pl.program_id(0)
    blk_rows = bm * tiles

    def zero_copy(i):
        dst = pl.multiple_of(i * blk_rows, blk_rows)
        return pltpu.make_async_copy(zbuf, xs_out.at[pl.ds(dst, blk_rows), :], zsem)

    @pl.when(step == 0)
    def _():
        zbuf[...] = jnp.zeros(zbuf.shape, F32)

        def zstart(i, carry):
            @pl.when(nv_ref[i] < bm)
            def _():
                zero_copy(i).start()
            return carry

        def zwait(i, carry):
            @pl.when(nv_ref[i] < bm)
            def _():
                zero_copy(i).wait()
            return carry

        lax.fori_loop(0, n_blocks, zstart, 0)
        lax.fori_loop(0, n_blocks, zwait, 0)

    def row_copy(t, k):
        src = pl.multiple_of(t * tiles, tiles)
        dst = pl.multiple_of(pos_ref[2 * t + k] * tiles, tiles)
        return pltpu.make_async_copy(x_ref.at[pl.ds(src, tiles), :], xs_out.at[pl.ds(dst, tiles), :], sem)

    def start(t, carry):
        row_copy(t, 0).start(priority=0)
        row_copy(t, 1).start(priority=1)
        return carry

    def wait(t, carry):
        row_copy(t, 0).wait()
        row_copy(t, 1).wait()
        return carry

    lax.fori_loop(0, tm, start, 0, unroll=DMA_UNROLL)
    lax.fori_loop(0, tm, wait, 0, unroll=DMA_UNROLL)


def _scatter(pos_flat, nvalid, x1t, n_blocks, tiles, bm):
    t = x1t.shape[0] // tiles
    tm = min(TOKEN_TILE, t)
    kern = functools.partial(_scatter_kernel, tm=tm, tiles=tiles, n_blocks=n_blocks, bm=bm)
    return pl.pallas_call(
        kern,
        grid=(t // tm,),
        in_specs=[
            pl.BlockSpec((2 * tm,), lambda i: (i,), memory_space=pltpu.SMEM),
            pl.BlockSpec(memory_space=pltpu.SMEM),
            pl.BlockSpec((tm * tiles, LANES), lambda i: (i, 0)),
        ],
        out_specs=pl.BlockSpec(memory_space=pl.ANY),
        out_shape=jax.ShapeDtypeStruct((n_blocks * bm * tiles, LANES), F32),
        scratch_shapes=[pltpu.VMEM((bm * tiles, LANES), F32),
                        pltpu.SemaphoreType.DMA(()), pltpu.SemaphoreType.DMA(())],
        compiler_params=_cparams(("arbitrary",)),
        name="moe_scatter",
    )(pos_flat, nvalid, x1t)


def _expert_kernel(be_ref, nv_ref, x_ref, wg_ref, wu_ref, wd_ref, y_ref, wg_s, wu_s, wd_s, *, tiles):
    i = pl.program_id(0)
    n_valid = nv_ref[i]

    @pl.when((i == 0) | (be_ref[i] != be_ref[jnp.maximum(i - 1, 0)]))
    def _():
        wg_s[...] = wg_ref[0].astype(BF16)
        wu_s[...] = wu_ref[0].astype(BF16)
        wd_s[...] = wd_ref[0].astype(BF16)

    @pl.when(n_valid == 0)
    def _():
        y_ref[...] = jnp.zeros(y_ref.shape, F32)

    @pl.when(n_valid > 0)
    def _():
        xb = _load_token_tiles(x_ref, x_ref.shape[0] // tiles, tiles).astype(BF16)
        g = jnp.dot(xb, wg_s[...], preferred_element_type=F32)
        u = jnp.dot(xb, wu_s[...], preferred_element_type=F32)
        h = g * jax.nn.sigmoid(g) * u
        _store_token_tiles(y_ref, jnp.dot(h.astype(BF16), wd_s[...], preferred_element_type=F32))


def _experts(block_expert, nvalid, xs, wg, wu, wd, bm):
    d, de = wg.shape[1], wg.shape[2]
    tiles = d // LANES
    blk_rows = bm * tiles
    nb = xs.shape[0] // blk_rows
    grid_spec = pltpu.PrefetchScalarGridSpec(
        num_scalar_prefetch=2,
        grid=(nb,),
        in_specs=[
            pl.BlockSpec((blk_rows, LANES), lambda i, be, nv: (i, 0)),
            pl.BlockSpec((1, d, de), lambda i, be, nv: (be[i], 0, 0)),
            pl.BlockSpec((1, d, de), lambda i, be, nv: (be[i], 0, 0)),
            pl.BlockSpec((1, de, d), lambda i, be, nv: (be[i], 0, 0)),
        ],
        out_specs=pl.BlockSpec((blk_rows, LANES), lambda i, be, nv: (i, 0)),
        scratch_shapes=[pltpu.VMEM((d, de), BF16), pltpu.VMEM((d, de), BF16), pltpu.VMEM((de, d), BF16)],
    )
    return pl.pallas_call(
        functools.partial(_expert_kernel, tiles=tiles),
        grid_spec=grid_spec,
        out_shape=jax.ShapeDtypeStruct(xs.shape, F32),
        compiler_params=_cparams(("arbitrary",)),
        name="moe_experts",
    )(block_expert, nvalid, xs, wg, wu, wd)


def _final_kernel(pos_ref, posn_ref, x1_ref, ri_ref, p_ref, ys_hbm, wpg_ref, wpp_ref, g2_ref, b2_ref,
                  pg_ref, out_ref, ybuf, sems, *, tm, tiles):
    step = pl.program_id(0)
    n_steps = pl.num_programs(0)
    slot = step % 2

    def row_copy(tab_ref, t, k, sl):
        src = pl.multiple_of(tab_ref[2 * t + k] * tiles, tiles)
        dst = pl.multiple_of(t * tiles, tiles)
        return pltpu.make_async_copy(ys_hbm.at[pl.ds(src, tiles), :], ybuf.at[sl, k, pl.ds(dst, tiles), :],
                                     sems.at[sl])

    def start_tile(tab_ref, sl):
        def body(t, carry):
            row_copy(tab_ref, t, 0, sl).start(priority=0)
            row_copy(tab_ref, t, 1, sl).start(priority=1)
            return carry

        lax.fori_loop(0, tm, body, 0, unroll=DMA_UNROLL)

    @pl.when(step == 0)
    def _():
        start_tile(pos_ref, 0)

    def wait_tile(sl):
        def body(t, carry):
            row_copy(pos_ref, t, 0, sl).wait()
            row_copy(pos_ref, t, 1, sl).wait()
            return carry

        lax.fori_loop(0, tm, body, 0, unroll=DMA_UNROLL)

    wait_tile(slot)

    per = tm // FINAL_ISSUE_GROUPS

    def issue(g):
        for t in range(g * per, (g + 1) * per):
            row_copy(posn_ref, t, 0, 1 - slot).start(priority=0)
            row_copy(posn_ref, t, 1, 1 - slot).start(priority=1)

    ri = ri_ref[...]
    issue(0)
    y0 = _load_token_tiles(ybuf.at[slot, 0], tm, tiles)
    issue(1)
    y1 = _load_token_tiles(ybuf.at[slot, 1], tm, tiles)
    issue(2)
    ffn = ri[:, 2:3] * y0 + ri[:, 3:4] * y1
    issue(3)
    x2 = _layer_norm(DN_ALPHA * x1_ref[...] + ffn, g2_ref[...], b2_ref[...])
    issue(4)
    gate = jax.nn.sigmoid(jnp.dot(x2.astype(BF16), wpg_ref[...], preferred_element_type=F32))
    issue(5)
    proj = jnp.dot(p_ref[...].astype(BF16), wpp_ref[...], preferred_element_type=F32)
    issue(6)
    out_ref[...] = x2 + _rms_norm(gate * proj, pg_ref[...])
    issue(7)

    @pl.when(step == n_steps - 1)
    def _():
        wait_tile(1 - slot)


def _final(pos_flat, x1, ri, p, ys, wpg, wpp, g2, b2, pg):
    t, d = x1.shape
    tm = min(TOKEN_TILE, t)
    n_steps = t // tm
    tiles = d // LANES
    row = lambda w: pl.BlockSpec((tm, w), lambda i: (i, 0))
    consts = [wpg, wpp, g2, b2, pg]
    kern = functools.partial(_final_kernel, tm=tm, tiles=tiles)
    return pl.pallas_call(
        kern,
        grid=(n_steps,),
        in_specs=[pl.BlockSpec((2 * tm,), lambda i: (i,), memory_space=pltpu.SMEM),
                  pl.BlockSpec((2 * tm,), lambda i: (jnp.minimum(i + 1, n_steps - 1),), memory_space=pltpu.SMEM),
                  row(d), row(LANES), row(p.shape[1]), pl.BlockSpec(memory_space=pl.ANY)]
                 + [_full(a.shape) for a in consts],
        out_specs=row(d),
        out_shape=jax.ShapeDtypeStruct((t, d), F32),
        scratch_shapes=[pltpu.VMEM((2, 2, tm * tiles, LANES), F32), pltpu.SemaphoreType.DMA((2,))],
        compiler_params=_cparams(("arbitrary",)),
        name="combine_ln2_ple",
    )(pos_flat, pos_flat, x1, ri, p, ys, *consts)


def _lane_patterns(pair_layout):
    lane = np.arange(LANES)
    if pair_layout:
        valid = (lane < 2 * QK_ROPE).astype(np.float32)
    else:
        valid = np.ones(LANES, np.float32)
    sign = np.where((lane % QK_ROPE) < HALF_ROPE, -1.0, 1.0).astype(np.float32) * valid
    return lane % HALF_ROPE, sign, valid


def _prep_weights(w_in, w_uq, w_uk, w_uv, w_group, w_router):
    d = w_in.shape[0]
    ql = w_uq.shape[0]
    kvl, n_heads, _ = w_uk.shape
    o_kv, o_kr = ql, ql + kvl
    o_conv = o_kr + QK_ROPE
    c_conv = (w_in.shape[1] - o_conv - 2 * d) // 2
    o_ga = o_conv + 2 * c_conv
    o_gb = o_ga + d
    kr = w_in[:, o_kr:o_conv]
    zpad = jnp.zeros((d, LANES - 2 * QK_ROPE), w_in.dtype)
    win_r = jnp.concatenate([
        w_in[:, :o_kv], w_in[:, o_kv:o_kr], w_in[:, o_conv:o_ga], w_in[:, o_ga:o_gb], w_in[:, o_gb:],
        kr, kr, zpad], axis=1).astype(BF16)
    assert win_r.shape[1] == C_END and c_conv == C_CB - C_CA and kvl == C_CA - C_KV and ql == C_KV

    hd = QK_NOPE + QK_ROPE
    wq = w_uq.reshape(ql, n_heads, hd)
    nope = wq[:, :, :QK_NOPE].reshape(ql, n_heads * QK_NOPE)
    x1 = wq[:, :, QK_NOPE:QK_NOPE + HALF_ROPE]
    x2 = wq[:, :, QK_NOPE + HALF_ROPE:]
    rope_a = jnp.concatenate([x1, x2], axis=2)
    wuq_s = jnp.concatenate([nope, rope_a.reshape(ql, -1)], axis=1).astype(BF16)

    def pair_layout(r):
        r = r.reshape(ql, n_heads // 2, 2 * QK_ROPE)
        z = jnp.zeros((ql, n_heads // 2, LANES - 2 * QK_ROPE), r.dtype)
        return jnp.concatenate([r, z], axis=2).reshape(ql, -1)

    wuq_p = jnp.concatenate([nope, pair_layout(rope_a)], axis=1).astype(BF16)

    wuk_flat = w_uk.reshape(kvl, n_heads * QK_NOPE)
    wuv_flat = w_uv.reshape(kvl, n_heads * V_HEAD)
    wukv = jnp.concatenate([wuk_flat, wuv_flat], axis=1).astype(BF16)
    eye = jnp.eye(n_heads, dtype=w_uk.dtype)
    wk_bd = jnp.einsum('lhn,hg->hngl', w_uk, eye).reshape(n_heads * QK_NOPE, n_heads * kvl).astype(BF16)
    wv_bd = jnp.einsum('lhv,hg->hlgv', w_uv, eye).reshape(n_heads * kvl, n_heads * V_HEAD).astype(BF16)

    n_exp = w_router.shape[1]
    wgr = jnp.concatenate([w_group, w_router,
                           jnp.zeros((d, LANES - N_GROUPS - n_exp), w_group.dtype)], axis=1)
    wgh = wgr.astype(BF16)
    wgl = (wgr - wgh.astype(F32)).astype(BF16)
    return win_r, wuq_p, wuq_s, wukv, wk_bd, wv_bd, wgh, wgl


def _moe(x1t, ri, cnt, wg, wu, wd, n_exp):
    t = ri.shape[0]
    tiles = x1t.shape[0] // t
    n_assign = 2 * t
    bm = MOE_BLOCK if n_assign >= 2 * MOE_BLOCK * n_exp else MOE_BLOCK_MIN
    n_blocks = (n_assign + n_exp * (bm - 1) + bm - 1) // bm
    pos, be = _route(ri, cnt, n_blocks=n_blocks, n_exp=n_exp, bm=bm)
    pos_flat = pos[:, :2].reshape(-1)
    block_expert = be[:n_blocks, 0]
    nvalid = be[:n_blocks, 1]
    xs = _scatter(pos_flat, nvalid, x1t, n_blocks, tiles, bm)
    ys = _experts(block_expert, nvalid, xs, wg, wu, wd, bm)
    return pos_flat, ys


def kernel(x_prompt, x_sample, cache_ckv, cache_krope, state_conv, page_table, p_prompt, p_sample, w_in, q_norm_g, w_uq, kv_norm_g, w_uk, w_uv, w_o_attn, w_dw, b_dw, conv_ln_g, conv_ln_b, w_pw2, w_out, ln1_g, ln1_b, w_group, w_router, w_gate, w_up, w_down, ln2_g, ln2_b, w_ple_gate, w_ple_proj, ple_norm_g):
    assert w_in.shape[0] == DEPTH
    b, s, d = x_prompt.shape
    bd, sd, _ = x_sample.shape
    assert sd == 1
    n_pages = page_table.shape[1]
    page_rows = cache_ckv.shape[2]
    past = n_pages * page_rows
    n_heads = w_uk.shape[2]
    kvl = w_uk.shape[1]
    width = w_dw.shape[1]
    n_exp = w_router.shape[2]
    c_conv = w_dw.shape[2]

    win_r, wuq_p, wuq_s, wukv, wk_bd, wv_bd, wgh, wgl = _prep_weights(
        w_in[0], w_uq[0], w_uk[0], w_uv[0], w_group[0], w_router[0])
    qg, kvg = q_norm_g, kv_norm_g
    wo, wpw, wout = w_o_attn[0].astype(BF16), w_pw2[0].astype(BF16), w_out[0].astype(BF16)
    wgate, wup, wdown = w_gate[0], w_up[0], w_down[0]
    wpg, wpp = w_ple_gate[0].astype(BF16), w_ple_proj[0].astype(BF16)
    w_dw_pad = jnp.concatenate([w_dw[0], jnp.zeros((CONV_HALO - width, c_conv), F32)], axis=0)

    inv16 = 1.0 / (ROPE_THETA ** (jnp.arange(HALF_ROPE, dtype=F32) / HALF_ROPE))

    def tables(pair_layout, n_rows, offset, step):
        idx, sign, valid = _lane_patterns(pair_layout)
        inv_l = (inv16[idx] * valid)[None, :]
        return _rope_tables(n_rows, offset, step, inv_l, jnp.asarray(sign)[None, :], jnp.asarray(valid)[None, :])

    def trunk_tail(x2d, o, y_or_u, sa, sb, p2d, conv=None):
        x1, x1t, ri, cnt = _mix(x2d, o, y_or_u, sa, sb, wo, wpw, wout, conv_ln_g, conv_ln_b, ln1_g, ln1_b,
                           wgh, wgl, conv=conv)
        pos_flat, ys = _moe(x1t, ri, cnt, wgate, wup, wdown, n_exp)
        return _final(pos_flat, x1, ri, p2d, ys, wpg, wpp, ln2_g, ln2_b, ple_norm_g)

    xs_ = x_sample.reshape(bd, d)
    cos_s, sin_s = tables(False, SUBLANES, past, 0)
    qn_s, qr_s, ckv_s, kr_s, u_s, sa_s, sb_s = _inproj(
        xs_, win_r, qg, kvg, wuq_s, None, cos_s, sin_s, n_rep=n_heads * QK_ROPE // LANES, seq=1, make_kv=False)
    q_lat = _dense(qn_s, wk_bd, F32).reshape(bd, n_heads, kvl)
    o_lat = _sample_attention(page_table, q_lat, qr_s.reshape(bd, n_heads, QK_ROPE),
                              ckv_s.reshape(bd, 1, kvl), kr_s.reshape(bd, 1, QK_ROPE),
                              cache_ckv[0], jnp.swapaxes(cache_krope[0], 1, 2))
    o_s = _dense(o_lat.reshape(bd, n_heads * kvl), wv_bd, BF16)
    ext_s = jnp.concatenate([state_conv[0], u_s[:, None, :]], axis=1)
    y_s = _conv_sample(jnp.transpose(ext_s, (1, 0, 2)), w_dw_pad, b_dw, width=width)
    out_s = trunk_tail(xs_, o_s, y_s, sa_s, sb_s, p_sample[0].reshape(bd, -1))
    new_conv_s = ext_s[:, 1:, :]

    xp = x_prompt.reshape(b * s, d)
    cos_p, sin_p = tables(True, s, 0, 1)
    qn, qr, kn, v, krd, ckv_p, kr_p, u_p, sa_p, sb_p = _inproj(
        xp, win_r, qg, kvg, wuq_p, wukv, cos_p, sin_p, n_rep=n_heads // 2, seq=s, make_kv=True)
    o_p = _prompt_attention(qn, qr, kn, krd, v, batch=b, seq=s)
    out_p = trunk_tail(xp, o_p, u_p, sa_p, sb_p, p_prompt[0].reshape(b * s, -1),
                       conv=(w_dw_pad, b_dw, s, width))
    u_p3 = u_p.reshape(b, s, c_conv)
    new_conv_p = u_p3[:, s - (width - 1):, :]

    return (out_p.reshape(b, s, d), out_s.reshape(bd, 1, d),
            ckv_p.reshape(1, b, s, kvl), kr_p.reshape(1, b, s, QK_ROPE), new_conv_p[None],
            ckv_s.reshape(1, bd, 1, kvl), kr_s.reshape(1, bd, 1, QK_ROPE), new_conv_s[None])
```
